```python
import math
import jax, jax.numpy as jnp
from jax import lax
import numpy as np

D_MODEL = 1024
BATCH = 8
SEQ = 4096
DEPTH = 4

GRID_W = 64
CTX_LEN = 256
CHUNK = 128

GM_HEADS = 4
GM_HD = 64
GM_W = GM_HEADS * GM_HD

ATT_HEADS = 8
KV_HEADS = 2
HEAD_DIM = 64
Q_PER_KV = ATT_HEADS // KV_HEADS
ATT_W = ATT_HEADS * HEAD_DIM
KV_W = KV_HEADS * HEAD_DIM
WINDOW = 128
ROPE_BASE = 10000.0

SSM_HEADS = 4
SSM_HD = 64
SSM_W = SSM_HEADS * SSM_HD
SSM_GROUPS = 2
SSM_STATE = 128
BC_W = SSM_GROUPS * SSM_STATE
CONV_K = 5
XBC_W = SSM_W + 2 * BC_W

MIX_W = GM_W + ATT_W + SSM_W
IN_SIZES = (GM_W, GM_W, ATT_W, KV_W, KV_W, SSM_W, XBC_W, 2 * SSM_HEADS)
IN_W = 2 * GM_W + ATT_W + 2 * KV_W + SSM_W + XBC_W + 2 * SSM_HEADS

D_FF = 2816
N_EXPERTS = 8
TOP_K = 2
N_DENSE = (DEPTH + 1) // 2
N_MOE = DEPTH // 2
EPS = 1e-6

kernel_name = "hybrid_parallel_mixer_dit_trunk"


def rms_norm(x, g):
    x32 = x.astype(jnp.float32)
    y = x32 * lax.rsqrt(jnp.mean(jnp.square(x32), axis=-1, keepdims=True) + EPS)
    return (y * g.astype(jnp.float32)).astype(x.dtype)


def rope_1d(x, pos):
    nf = x.shape[-1] // 2
    freqs = ROPE_BASE ** (-jnp.arange(nf, dtype=jnp.float32) / nf)
    ang = pos.astype(jnp.float32)[:, None] * freqs[None, :]
    cos = jnp.cos(ang)[:, None, :]
    sin = jnp.sin(ang)[:, None, :]
    x32 = x.astype(jnp.float32)
    x1, x2 = x32[..., :nf], x32[..., nf:]
    return jnp.concatenate([x1 * cos - x2 * sin, x2 * cos + x1 * sin], axis=-1).astype(x.dtype)


def axial_rope(x):
    seq_len = x.shape[1]
    rows = seq_len // GRID_W
    row = jnp.repeat(jnp.arange(rows), GRID_W)
    col = jnp.tile(jnp.arange(GRID_W), rows)
    half = HEAD_DIM // 2
    return jnp.concatenate([rope_1d(x[..., :half], row), rope_1d(x[..., half:], col)], axis=-1)


def chunk_gmlp(u, v, g_v, w_s, b_s):
    b, t, _ = u.shape
    u = jax.nn.gelu(u)
    vh = rms_norm(jax.nn.gelu(v).reshape(b, t // CHUNK, CHUNK, GM_HEADS, GM_HD), g_v.reshape(GM_HEADS, GM_HD))
    s = jnp.einsum('hij,bcjhd->bcihd', w_s, vh) + b_s.T[:, :, None]
    return u * s.reshape(b, t, GM_W)


def sink_softmax(logits, sink):
    s = jnp.broadcast_to(sink.astype(jnp.float32)[None, :, :, None, None], logits.shape[:-1] + (1,))
    return jax.nn.softmax(jnp.concatenate([logits, s], axis=-1), axis=-1)[..., :-1]


def window_attention(q, k, v, kc, vc, sink):
    b, l = q.shape[0], q.shape[1]
    nb = l // WINDOW
    scale = HEAD_DIM ** -0.5
    pad = ((0, 0), (WINDOW, WINDOW), (0, 0), (0, 0))

    def band(t):
        tp = jnp.pad(t, pad).reshape(b, nb + 2, WINDOW, KV_HEADS, HEAD_DIM)
        return jnp.concatenate([tp[:, :-2], tp[:, 1:-1], tp[:, 2:]], axis=2)

    qb = jnp.moveaxis(q.reshape(b, nb, WINDOW, KV_HEADS, Q_PER_KV, HEAD_DIM), 1, 0)
    kb = jnp.moveaxis(band(k), 1, 0)
    vb = jnp.moveaxis(band(v), 1, 0)
    qi = jnp.arange(WINDOW)[:, None]
    ki = jnp.arange(3 * WINDOW)[None, :]
    rel = ki - qi

    def block(args):
        qn, kn, vn, n = args
        j = n * WINDOW + ki - WINDOW
        valid = (rel >= 0) & (rel <= 2 * WINDOW) & (j >= 0) & (j < l)
        s_loc = jnp.einsum('bqkgd,bskd->bkgqs', qn, kn).astype(jnp.float32) * scale
        s_loc = jnp.where(valid, s_loc, -jnp.inf)
        s_ctx = jnp.einsum('bqkgd,bskd->bkgqs', qn, kc).astype(jnp.float32) * scale
        p = sink_softmax(jnp.concatenate([s_loc, s_ctx], axis=-1), sink).astype(vn.dtype)
        return (jnp.einsum('bkgqs,bskd->bqkgd', p[..., :3 * WINDOW], vn)
                + jnp.einsum('bkgqs,bskd->bqkgd', p[..., 3 * WINDOW:], vc))

    out = lax.map(block, (qb, kb, vb, jnp.arange(nb)))
    return jnp.moveaxis(out, 0, 1).reshape(b, l, ATT_W)


def context_attention(qc, kc, vc, sink):
    b, cl = qc.shape[0], qc.shape[1]
    s = jnp.einsum('bqkgd,bskd->bkgqs', qc, kc).astype(jnp.float32) * (HEAD_DIM ** -0.5)
    p = sink_softmax(s, sink).astype(vc.dtype)
    return jnp.einsum('bkgqs,bskd->bqkgd', p, vc).reshape(b, cl, ATT_W)


def dw_conv(x, w, bias):
    y = lax.conv_general_dilated(x, w[:, None, :], window_strides=(1,),
                                 padding=[(CONV_K // 2, CONV_K // 2)],
                                 dimension_numbers=('NWC', 'WIO', 'NWC'),
                                 feature_group_count=x.shape[-1])
    return y + bias


def ssd_scan(xh, dt, a, bm, cm, s0, want_y):
    b, t, nh, hp = xh.shape
    ns = bm.shape[-1]
    nc = t // CHUNK
    xq = xh.reshape(b, nc, CHUNK, nh, hp)
    dq = dt.reshape(b, nc, CHUNK, nh)
    bq = bm.reshape(b, nc, CHUNK, nh, ns)
    cq = cm.reshape(b, nc, CHUNK, nh, ns)
    a_cum = jnp.cumsum(dq * a, axis=2)
    a_last = a_cum[:, :, -1]
    w_end = jnp.exp(a_last[:, :, None] - a_cum) * dq
    states = jnp.einsum('bcjhn,bcjh,bcjhp->bchpn', bq, w_end, xq)

    def step(s, inp):
        st, al = inp
        return jnp.exp(al)[:, :, None, None] * s + st, s

    s_fin, s_in = lax.scan(step, s0, (jnp.moveaxis(states, 1, 0), jnp.moveaxis(a_last, 1, 0)))
    if not want_y:
        return None, s_fin
    s_in = jnp.moveaxis(s_in, 0, 1)
    tri = jnp.tril(jnp.ones((CHUNK, CHUNK), dtype=bool))[None, None, :, :, None]
    seg = jnp.exp(jnp.where(tri, a_cum[:, :, :, None, :] - a_cum[:, :, None, :, :], -jnp.inf))
    mix = jnp.einsum('bcihn,bcjhn->bcijh', cq, bq) * seg * dq[:, :, None]
    y = (jnp.einsum('bcijh,bcjhp->bcihp', mix, xq)
         + jnp.einsum('bcihn,bchpn->bcihp', cq, s_in) * jnp.exp(a_cum)[..., None])
    return y.reshape(b, t, nh, hp), s_fin


def ssd_direction(xh, dt, a, bm, cm, s0, want_y, reverse):
    if reverse:
        xh, dt, bm, cm = (jnp.flip(t, axis=1) for t in (xh, dt, bm, cm))
    y, s = ssd_scan(xh, dt, a, bm, cm, s0, want_y)
    if reverse and y is not None:
        y = jnp.flip(y, axis=1)
    return y, s


def ssd_prep(xbc, dt_raw, conv_w, conv_b):
    b, t, _ = xbc.shape
    xbc = jax.nn.silu(dw_conv(xbc, conv_w, conv_b)).astype(jnp.float32)
    xs, bm, cm = jnp.split(xbc, [SSM_W, SSM_W + BC_W], axis=-1)
    rep = SSM_HEADS // SSM_GROUPS
    xh = xs.reshape(b, t, SSM_HEADS, SSM_HD)
    bm = jnp.repeat(bm.reshape(b, t, SSM_GROUPS, SSM_STATE), rep, axis=2)
    cm = jnp.repeat(cm.reshape(b, t, SSM_GROUPS, SSM_STATE), rep, axis=2)
    return xh, bm, cm, dt_raw.reshape(b, t, 2, SSM_HEADS).astype(jnp.float32)


def ssd_mixer(z, xbc, dt_raw, z_c, xbc_c, dt_raw_c, conv_w, conv_b, dt_bias, a_log, d_skip, norm_g, need_ctx):
    xh, bm, cm, dtr = ssd_prep(xbc, dt_raw, conv_w, conv_b)
    xhc, bmc, cmc, dtrc = ssd_prep(xbc_c, dt_raw_c, conv_w, conv_b)
    dsk = d_skip.astype(jnp.float32)[:, None]
    y_lat = dsk * xh
    y_ctx = dsk * xhc
    s0 = jnp.zeros((xh.shape[0], SSM_HEADS, SSM_HD, SSM_STATE), jnp.float32)
    for d in range(2):
        a = -jnp.exp(a_log[d].astype(jnp.float32))
        bias = dt_bias[d].astype(jnp.float32)
        dt = jax.nn.softplus(dtr[:, :, d] + bias)
        dtc = jax.nn.softplus(dtrc[:, :, d] + bias)
        yc, sc = ssd_direction(xhc, dtc, a, bmc, cmc, s0, need_ctx, d == 1)
        yl, _ = ssd_direction(xh, dt, a, bm, cm, sc, True, d == 1)
        y_lat = y_lat + yl
        if need_ctx:
            y_ctx = y_ctx + yc

    def gate_norm(y, zz):
        b, t = zz.shape[0], zz.shape[1]
        return rms_norm(y.reshape(b, t, SSM_W) * jax.nn.silu(zz.astype(jnp.float32)), norm_g).astype(zz.dtype)

    out_c = gate_norm(y_ctx, z_c) if need_ctx else None
    return gate_norm(y_lat, z), out_c


def swiglu(h, wg, wu, wd):
    return (jax.nn.silu(h @ wg) * (h @ wu)) @ wd


def moe_ffn(h, w_r, wg, wu, wd):
    logits = (h @ w_r).astype(jnp.float32)
    top_v, top_i = lax.top_k(logits, TOP_K)
    gates = jax.nn.softmax(top_v, axis=-1)
    combine = jnp.sum(jax.nn.one_hot(top_i, N_EXPERTS, dtype=jnp.float32) * gates[..., None], axis=-2).astype(h.dtype)
    out = jnp.zeros_like(h)
    for e in range(N_EXPERTS):
        out = out + combine[..., e:e + 1] * swiglu(h, wg[e], wu[e], wd[e])
    return out


def qk_heads(t, n_heads, g):
    return rms_norm(t.reshape(t.shape[0], t.shape[1], n_heads, HEAD_DIM), g)


def setup_inputs(seed: int = 0) -> dict:
    key = jax.random.key(seed)
    ks = jax.random.split(key, 32)
    f32 = jnp.float32

    def nrm(k, shape, scale):
        return jax.random.normal(k, shape, f32) * scale

    def gain(k, shape):
        return 1.0 + 0.02 * jax.random.normal(k, shape, f32)

    dt0 = jnp.exp(jax.random.uniform(ks[20], (DEPTH, 2, SSM_HEADS), f32, math.log(1e-3), math.log(1e-1)))
    return {
        "x": nrm(ks[0], (BATCH, SEQ, D_MODEL), 1.0),
        "c": nrm(ks[1], (BATCH, D_MODEL), 1.0),
        "ctx": nrm(ks[2], (BATCH, CTX_LEN, D_MODEL), 1.0),
        "c_ctx": nrm(ks[3], (D_MODEL,), 1.0),
        "w_mod": nrm(ks[4], (DEPTH, D_MODEL, 6 * D_MODEL), 0.5 * D_MODEL ** -0.5),
        "b_mod": nrm(ks[5], (DEPTH, 6 * D_MODEL), 0.02),
        "norm1_g": gain(ks[6], (DEPTH, D_MODEL)),
        "norm2_g": gain(ks[7], (DEPTH, D_MODEL)),
        "w_in": nrm(ks[8], (DEPTH, D_MODEL, IN_W), D_MODEL ** -0.5),
        "w_out": nrm(ks[9], (DEPTH, MIX_W, D_MODEL), MIX_W ** -0.5),
        "gm_v_g": gain(ks[10], (DEPTH, GM_W)),
        "gm_ws": nrm(ks[11], (DEPTH, GM_HEADS, CHUNK, CHUNK), 0.5 * CHUNK ** -0.5),
        "gm_bs": gain(ks[12], (DEPTH, GM_HEADS, CHUNK)),
        "att_q_g": gain(ks[13], (DEPTH, HEAD_DIM)),
        "att_k_g": gain(ks[14], (DEPTH, HEAD_DIM)),
        "att_sink": nrm(ks[15], (DEPTH, ATT_HEADS), 0.5),
        "ssm_conv_w": nrm(ks[16], (DEPTH, CONV_K, XBC_W), CONV_K ** -0.5),
        "ssm_conv_b": nrm(ks[17], (DEPTH, XBC_W), 0.02),
        "ssm_dt_bias": dt0 + jnp.log(-jnp.expm1(-dt0)),
        "ssm_a_log": jnp.log(jax.random.uniform(ks[18], (DEPTH, 2, SSM_HEADS), f32, 1.0, 16.0)),
        "ssm_d": 1.0 + 0.1 * jax.random.normal(ks[19], (DEPTH, SSM_HEADS), f32),
        "ssm_norm_g": gain(ks[21], (DEPTH, SSM_W)),
        "ffn_w_gate": nrm(ks[22], (N_DENSE, D_MODEL, D_FF), D_MODEL ** -0.5),
        "ffn_w_up": nrm(ks[23], (N_DENSE, D_MODEL, D_FF), D_MODEL ** -0.5),
        "ffn_w_down": nrm(ks[24], (N_DENSE, D_FF, D_MODEL), D_FF ** -0.5),
        "moe_router": nrm(ks[25], (N_MOE, D_MODEL, N_EXPERTS), D_MODEL ** -0.5),
        "moe_w_gate": nrm(ks[26], (N_MOE, N_EXPERTS, D_MODEL, D_FF), D_MODEL ** -0.5),
        "moe_w_up": nrm(ks[27], (N_MOE, N_EXPERTS, D_MODEL, D_FF), D_MODEL ** -0.5),
        "moe_w_down": nrm(ks[28], (N_MOE, N_EXPERTS, D_FF, D_MODEL), D_FF ** -0.5),
    }


def reference(x, c, ctx, c_ctx, w_mod, b_mod, norm1_g, norm2_g, w_in, w_out, gm_v_g, gm_ws, gm_bs,
              att_q_g, att_k_g, att_sink, ssm_conv_w, ssm_conv_b, ssm_dt_bias, ssm_a_log, ssm_d,
              ssm_norm_g, ffn_w_gate, ffn_w_up, ffn_w_down, moe_router, moe_w_gate, moe_w_up, moe_w_down):
    b, l = x.shape[0], x.shape[1]
    cl = ctx.shape[1]
    split_idx = [int(v) for v in np.cumsum(IN_SIZES)[:-1]]
    c_silu = jax.nn.silu(c)
    cc_silu = jax.nn.silu(c_ctx)
    xc = ctx
    for i in range(DEPTH):
        need_ctx = i < DEPTH - 1
        mod = jnp.split((c_silu @ w_mod[i] + b_mod[i])[:, None, :], 6, axis=-1)
        modc = jnp.split(cc_silu @ w_mod[i] + b_mod[i], 6, axis=-1)

        h = rms_norm(x, norm1_g[i]) * (1 + mod[1]) + mod[0]
        hc = rms_norm(xc, norm1_g[i]) * (1 + modc[1]) + modc[0]
        gu, gv, pq, pk, pv, pz, pxbc, pdt = jnp.split(h @ w_in[i], split_idx, axis=-1)
        cu, cv, cq, ck, cvv, cz, cxbc, cdt = jnp.split(hc @ w_in[i], split_idx, axis=-1)

        gm = chunk_gmlp(gu, gv, gm_v_g[i], gm_ws[i], gm_bs[i])

        sink = att_sink[i].reshape(KV_HEADS, Q_PER_KV)
        q = axial_rope(qk_heads(pq, ATT_HEADS, att_q_g[i])).reshape(b, l, KV_HEADS, Q_PER_KV, HEAD_DIM)
        k = axial_rope(qk_heads(pk, KV_HEADS, att_k_g[i]))
        v = pv.reshape(b, l, KV_HEADS, HEAD_DIM)
        kc = qk_heads(ck, KV_HEADS, att_k_g[i])
        vc = cvv.reshape(b, cl, KV_HEADS, HEAD_DIM)
        att = window_attention(q, k, v, kc, vc, sink)

        ssm, ssm_c = ssd_mixer(pz, pxbc, pdt, cz, cxbc, cdt, ssm_conv_w[i], ssm_conv_b[i], ssm_dt_bias[i],
                               ssm_a_log[i], ssm_d[i], ssm_norm_g[i], need_ctx)

        x = x + mod[2] * (jnp.concatenate([gm, att, ssm], axis=-1) @ w_out[i])
        if need_ctx:
            gm_c = chunk_gmlp(cu, cv, gm_v_g[i], gm_ws[i], gm_bs[i])
            qc = qk_heads(cq, ATT_HEADS, att_q_g[i]).reshape(b, cl, KV_HEADS, Q_PER_KV, HEAD_DIM)
            att_c = context_attention(qc, kc, vc, sink)
            xc = xc + modc[2] * (jnp.concatenate([gm_c, att_c, ssm_c], axis=-1) @ w_out[i])

        hf = rms_norm(x, norm2_g[i]) * (1 + mod[4]) + mod[3]
        if need_ctx:
            hfc = rms_norm(xc, norm2_g[i]) * (1 + modc[4]) + modc[3]
            tokens = jnp.concatenate([hfc, hf], axis=1)
        else:
            tokens = hf
        j = i // 2
        if i % 2 == 0:
            f = swiglu(tokens, ffn_w_gate[j], ffn_w_up[j], ffn_w_down[j])
        else:
            f = moe_ffn(tokens, moe_router[j], moe_w_gate[j], moe_w_up[j], moe_w_down[j])
        if need_ctx:
            xc = xc + modc[5] * f[:, :cl]
            f = f[:, cl:]
        x = x + mod[5] * f
    return x
```

```python
import functools
import math

import numpy as np
import jax
import jax.numpy as jnp
from jax import lax
from jax.experimental import pallas as pl
from jax.experimental.pallas import tpu as pltpu

F32 = jnp.float32
BF16 = jnp.bfloat16
I32 = jnp.int32
HIGHEST = lax.Precision.HIGHEST

D_MODEL = 1024
CHUNK = 128
GM_HEADS, GM_HD, GM_W = 4, 64, 256
ATT_HEADS, KV_HEADS, HEAD_DIM = 8, 2, 64
ATT_W, KV_W = 512, 128
GRID_W = 64
ROPE_BASE = 10000.0
SSM_HEADS, SSM_HD, SSM_W = 4, 64, 256
SSM_STATE, BC_W, CONV_K, XBC_W = 128, 256, 5, 768
D_FF = 2816
N_EXPERTS = 8
EPS = 1e-6
LANES = 128
DT_PAD = 128
C_GU, C_GV, C_Q, C_K, C_V, C_Z, C_XBC, C_DT, IN_WP = 0, 256, 512, 1024, 1152, 1280, 1536, 2304, 2432
FF_BLOCK = 256
EXPERT_TILE = 512
VMEM_LIMIT = 56 * 1024 * 1024


def _cparams(sem):
    return pltpu.CompilerParams(dimension_semantics=sem, vmem_limit_bytes=VMEM_LIMIT)


def _silu(x):
    return x * jax.nn.sigmoid(x)


def _mod_kernel(c_ref, w_ref, b_ref, o_ref):
    cs = _silu(c_ref[...])
    o_ref[0, 0] = jnp.dot(cs, w_ref[0], precision=HIGHEST, preferred_element_type=F32) + b_ref[0, 0]


def _modulation(cvec, w_mod, b_mod):
    depth = w_mod.shape[0]
    r = cvec.shape[0]
    return pl.pallas_call(
        _mod_kernel,
        grid=(depth, 6),
        in_specs=[
            pl.BlockSpec((r, D_MODEL), lambda i, n: (0, 0)),
            pl.BlockSpec((1, D_MODEL, D_MODEL), lambda i, n: (i, 0, n)),
            pl.BlockSpec((1, 1, 1, D_MODEL), lambda i, n: (i, n, 0, 0)),
        ],
        out_specs=pl.BlockSpec((1, 1, r, D_MODEL), lambda i, n: (i, n, 0, 0)),
        out_shape=jax.ShapeDtypeStruct((depth, 6, r, D_MODEL), F32),
        compiler_params=_cparams(("arbitrary", "arbitrary")),
        name="modulation",
    )(cvec, w_mod, b_mod.reshape(depth, 6, 1, D_MODEL))


def _mod_rows(mod_ref, tile_row0, rows, cl, lo):
    ridx = tile_row0 + lax.broadcasted_iota(I32, (rows, 1), 0)
    is_ctx = ridx < cl
    m = mod_ref[0]
    return [jnp.where(is_ctx, m[8 + k:9 + k], m[k:k + 1]) for k in lo]


def _inproj_kernel(x_ref, mod_ref, g1_ref, w_ref, cos_ref, sin_ref, gq_ref, gk_ref, gvg_ref, seg_ref, seg64_ref,
                   gu_ref, gv_ref, q_ref, k_ref, v_ref, z_ref, xbc_ref, dt_ref, *, tr, cl):
    x = x_ref[0]
    ms = jnp.mean(x * x, axis=-1, keepdims=True)
    y = x * lax.rsqrt(ms + EPS) * g1_ref[...]
    shift, scale = _mod_rows(mod_ref, pl.program_id(1) * tr, tr, cl, (0, 1))
    h = (y * (1.0 + scale) + shift).astype(BF16)

    def proj(a, b):
        return jnp.dot(h, w_ref[:, a:b], preferred_element_type=F32)

    guv = proj(C_GU, C_Q)
    gu_ref[0] = jax.nn.gelu(guv[:, :GM_W]).astype(BF16)
    gv = jax.nn.gelu(guv[:, GM_W:])
    ssv = jnp.dot((gv * gv).astype(BF16), seg64_ref[...], preferred_element_type=F32)
    gv_ref[0] = (gv * lax.rsqrt(ssv * (1.0 / GM_HD) + EPS) * gvg_ref[...]).astype(BF16)

    cos = cos_ref[...]
    sin = sin_ref[...]

    def norm_rope(t, gain):
        ss = jnp.dot((t * t).astype(BF16), seg_ref[...], preferred_element_type=F32)
        tn = t * lax.rsqrt(ss * (1.0 / HEAD_DIM) + EPS) * gain
        return tn * cos + pltpu.roll(tn, LANES // 2, axis=1) * sin

    qkv = proj(C_Q, C_Z)
    gq = gq_ref[...] * (HEAD_DIM ** -0.5)
    for c in range(ATT_W // LANES):
        q_ref[0, :, c * LANES:(c + 1) * LANES] = norm_rope(qkv[:, c * LANES:(c + 1) * LANES], gq).astype(BF16)
    k_ref[0] = norm_rope(qkv[:, ATT_W:ATT_W + KV_W], gk_ref[...]).astype(BF16)
    v_ref[0] = qkv[:, ATT_W + KV_W:].astype(BF16)
    z_ref[0] = proj(C_Z, C_XBC).astype(BF16)
    xbc_ref[0] = proj(C_XBC, C_DT).astype(BF16)
    dt_ref[0] = proj(C_DT, IN_WP)


def _in_projection(xs, mod, g1, w_in, cos, sin, gq, gk, gvg, seg, seg64, *, cl, nt):
    b, s, _ = xs.shape
    tr = s // nt
    tok = lambda w: pl.BlockSpec((1, tr, w), lambda bi, j: (bi, j, 0))
    full = lambda a: pl.BlockSpec(a.shape, lambda bi, j: (0,) * a.ndim)
    widths = (GM_W, GM_W, ATT_W, KV_W, KV_W, SSM_W, XBC_W)
    return pl.pallas_call(
        functools.partial(_inproj_kernel, tr=tr, cl=cl),
        grid=(b, nt),
        in_specs=[tok(D_MODEL), pl.BlockSpec((1, 16, D_MODEL), lambda bi, j: (bi, 0, 0)), full(g1), full(w_in),
                  pl.BlockSpec((tr, LANES), lambda bi, j: (j, 0)), pl.BlockSpec((tr, LANES), lambda bi, j: (j, 0)),
                  full(gq), full(gk), full(gvg), full(seg), full(seg64)],
        out_specs=[tok(w) for w in widths] + [tok(DT_PAD)],
        out_shape=[jax.ShapeDtypeStruct((b, s, w), BF16) for w in widths]
        + [jax.ShapeDtypeStruct((b, s, DT_PAD), F32)],
        compiler_params=_cparams(("parallel", "parallel")),
        name="in_projection",
    )(xs, mod, g1, w_in, cos, sin, gq, gk, gvg, seg, seg64)


def _gmlp_kernel(gu_ref, gv_ref, ws_ref, bs_ref, o_ref, *, nch):
    head = lax.broadcasted_iota(I32, (CHUNK, GM_W), 1) >> 6

    def body(c, carry):
        r0 = pl.multiple_of(c * CHUNK, CHUNK)
        v = gv_ref[0, pl.ds(r0, CHUNK), :]
        vbd = jnp.concatenate([jnp.where(head == hh, v, jnp.zeros_like(v)) for hh in range(GM_HEADS)], axis=0)
        sp = jnp.dot(ws_ref[...], vbd, preferred_element_type=F32) + bs_ref[...]
        o_ref[0, pl.ds(r0, CHUNK), :] = (gu_ref[0, pl.ds(r0, CHUNK), :].astype(F32) * sp).astype(BF16)
        return carry

    lax.fori_loop(0, nch, body, 0)


def _gmlp(gu, gv, ws_cat, bs_exp):
    b, s, _ = gu.shape
    seq = pl.BlockSpec((1, s, GM_W), lambda bi: (bi, 0, 0))
    return pl.pallas_call(
        functools.partial(_gmlp_kernel, nch=s // CHUNK),
        grid=(b,),
        in_specs=[seq, seq, pl.BlockSpec(ws_cat.shape, lambda bi: (0, 0)), pl.BlockSpec(bs_exp.shape, lambda bi: (0, 0))],
        out_specs=seq,
        out_shape=jax.ShapeDtypeStruct((b, s, GM_W), BF16),
        compiler_params=_cparams(("parallel",)),
        name="gmlp",
    )(gu, gv, ws_cat, bs_exp)


def _attn_kernel(q_ref, k_ref, v_ref, sink_ref, o_ref, *, nch, cl):
    ncb = cl // CHUNK
    nslab = ATT_W // LANES
    rows = 2 * nslab * CHUNK
    nkeys = cl + 3 * CHUNK
    lane = lax.broadcasted_iota(I32, (CHUNK, LANES), 1)
    first_head = ((lane >> 5) & 1) == 0
    low_half = lane < LANES // 2
    kc = k_ref[0, 0:cl, :]
    vc = v_ref[0, 0:cl, :]
    sink = sink_ref[...]
    qi = lax.broadcasted_iota(I32, (rows, nkeys), 0) & (CHUNK - 1)
    kj = lax.broadcasted_iota(I32, (rows, nkeys), 1)
    in_prev = jnp.logical_and(kj >= cl, kj < cl + CHUNK)
    in_next = kj >= cl + 2 * CHUNK
    tri_prev = jnp.logical_and(in_prev, kj - cl >= qi)
    tri_next = jnp.logical_and(in_next, kj - (cl + 2 * CHUNK) <= qi)
    in_own = jnp.logical_and(kj >= cl + CHUNK, kj < cl + 2 * CHUNK)

    def body(n, carry):
        r0 = pl.multiple_of(n * CHUNK, CHUNK)
        rp = pl.multiple_of(jnp.maximum(n - 1, ncb) * CHUNK, CHUNK)
        rn = pl.multiple_of(jnp.minimum(n + 1, nch - 1) * CHUNK, CHUNK)
        q = q_ref[0, pl.ds(r0, CHUNK), :]
        zero = jnp.zeros((CHUNK, LANES), BF16)
        blocks = []
        for c in range(nslab):
            qc = q[:, c * LANES:(c + 1) * LANES]
            blocks += [jnp.where(first_head, qc, zero), jnp.where(first_head, zero, qc)]
        qs = jnp.concatenate(blocks, axis=0)
        kk = jnp.concatenate([kc, k_ref[0, pl.ds(rp, CHUNK), :], k_ref[0, pl.ds(r0, CHUNK), :],
                              k_ref[0, pl.ds(rn, CHUNK), :]], axis=0)
        vv = jnp.concatenate([vc, v_ref[0, pl.ds(rp, CHUNK), :], v_ref[0, pl.ds(r0, CHUNK), :],
                              v_ref[0, pl.ds(rn, CHUNK), :]], axis=0)
        sc = lax.dot_general(qs, kk, (((1,), (1,)), ((), ())), preferred_element_type=F32)
        local = jnp.logical_or(in_own, jnp.logical_or(jnp.logical_and(tri_prev, n - 1 >= ncb),
                                                      jnp.logical_and(tri_next, n + 1 <= nch - 1)))
        valid = jnp.logical_or(kj < cl, jnp.logical_and(local, n >= ncb))
        sc = jnp.where(valid, sc, -jnp.inf)
        m = jnp.maximum(jnp.max(sc, axis=-1, keepdims=True), sink)
        e = jnp.exp(sc - m)
        den = jnp.sum(e, axis=-1, keepdims=True) + jnp.exp(sink - m)
        p = (e / den).astype(BF16)
        o = jnp.dot(p, vv, preferred_element_type=F32)
        for c in range(nslab):
            oa = o[(2 * c) * CHUNK:(2 * c + 1) * CHUNK]
            ob = o[(2 * c + 1) * CHUNK:(2 * c + 2) * CHUNK]
            o_ref[0, pl.ds(r0, CHUNK), c * LANES:(c + 1) * LANES] = jnp.where(low_half, oa, ob).astype(BF16)
        return carry

    lax.fori_loop(0, nch, body, 0)


def _attention(q, k, v, sink_col, *, cl):
    b, s, _ = q.shape
    seq = lambda w: pl.BlockSpec((1, s, w), lambda bi: (bi, 0, 0))
    return pl.pallas_call(
        functools.partial(_attn_kernel, nch=s // CHUNK, cl=cl),
        grid=(b,),
        in_specs=[seq(ATT_W), seq(KV_W), seq(KV_W), pl.BlockSpec(sink_col.shape, lambda bi: (0, 0))],
        out_specs=seq(ATT_W),
        out_shape=jax.ShapeDtypeStruct((b, s, ATT_W), BF16),
        compiler_params=_cparams(("parallel",)),
        name="attention",
    )(q, k, v, sink_col)


def _ssd_kernel(xbc_ref, dt_ref, cw_ref, cb_ref, dtb_ref, alog_ref, dsk_ref, y_ref, xc_ref, st_ref, *, nch, cl):
    ncb = cl // CHUNK
    s = nch * CHUNK
    halo = 16

    def conv_body(c, carry):
        r0 = pl.multiple_of(c * CHUNK, CHUNK)
        has_prev = jnp.logical_and(c != 0, c != ncb)
        has_next = jnp.logical_and(c != ncb - 1, c != nch - 1)
        rp = pl.multiple_of(jnp.maximum(r0 - halo, 0), halo)
        rn = pl.multiple_of(jnp.minimum(r0 + CHUNK, s - halo), halo)
        prev = jnp.where(has_prev, xbc_ref[0, pl.ds(rp, halo), :].astype(F32), 0.0)
        nxt = jnp.where(has_next, xbc_ref[0, pl.ds(rn, halo), :].astype(F32), 0.0)
        cat = jnp.concatenate([prev, xbc_ref[0, pl.ds(r0, CHUNK), :].astype(F32), nxt], axis=0)
        acc = jnp.zeros((CHUNK, XBC_W), F32) + cb_ref[...]
        for kk in range(CONV_K):
            off = halo - CONV_K // 2 + kk
            acc = acc + cw_ref[kk:kk + 1, :] * cat[off:off + CHUNK]
        xc_ref[pl.ds(r0, CHUNK), :] = _silu(acc).astype(BF16)
        return carry

    lax.fori_loop(0, nch, conv_body, 0)

    ti = lax.broadcasted_iota(I32, (CHUNK, CHUNK), 0)
    tj = lax.broadcasted_iota(I32, (CHUNK, CHUNK), 1)
    low_half = tj < LANES // 2
    a_neg = -jnp.exp(alog_ref[...])
    dtb = dtb_ref[...]

    def scan(direction):
        tri = (ti >= tj) if direction == 0 else (tj >= ti)
        tri_f = tri.astype(F32)
        last = CHUNK - 1 if direction == 0 else 0
        st_ref[...] = jnp.zeros_like(st_ref)

        def body(step, carry):
            if direction == 0:
                c = step
            else:
                c = jnp.where(step < ncb, ncb - 1 - step, nch - 1 - (step - ncb))
            r0 = pl.multiple_of(c * CHUNK, CHUNK)
            xcv = xc_ref[pl.ds(r0, CHUNK), :]
            dtv = jax.nn.softplus(dt_ref[0, pl.ds(r0, CHUNK), :] + dtb)
            acum = jnp.dot(tri_f, dtv * a_neg, precision=HIGHEST, preferred_element_type=F32)
            acum_t = acum.T
            outs = []
            for g in range(2):
                xg = xcv[:, g * LANES:(g + 1) * LANES].astype(F32)
                bg = xcv[:, SSM_W + g * LANES:SSM_W + (g + 1) * LANES]
                cg = xcv[:, SSM_W + BC_W + g * LANES:SSM_W + BC_W + (g + 1) * LANES]
                gram = lax.dot_general(cg, bg, (((1,), (1,)), ((), ())), preferred_element_type=F32)
                mixes, dtc, eac, wend, dec = [], [], [], [], []
                for hh in range(2):
                    col = direction * SSM_HEADS + 2 * g + hh
                    ac = acum[:, col:col + 1]
                    al = acum[last:last + 1, col:col + 1]
                    seg = jnp.exp(jnp.where(tri, ac - acum_t[col:col + 1, :], -jnp.inf))
                    mixes.append((gram * seg).astype(BF16))
                    dtc.append(dtv[:, col:col + 1])
                    eac.append(jnp.exp(ac))
                    wend.append(jnp.exp(al - ac) * dtv[:, col:col + 1])
                    dec.append(jnp.exp(al))
                pair = lambda t: jnp.where(low_half, t[0], t[1])
                xdt = (xg * pair(dtc)).astype(BF16)
                y_intra = jnp.where(low_half, jnp.dot(mixes[0], xdt, preferred_element_type=F32),
                                    jnp.dot(mixes[1], xdt, preferred_element_type=F32))
                stg = st_ref[:, g * LANES:(g + 1) * LANES]
                y_inter = jnp.dot(cg, stg.astype(BF16), preferred_element_type=F32) * pair(eac)
                xw = (xg * pair(wend)).astype(BF16)
                contrib = lax.dot_general(bg, xw, (((0,), (0,)), ((), ())), preferred_element_type=F32)
                st_ref[:, g * LANES:(g + 1) * LANES] = stg * pair(dec)[0:1, :] + contrib
                yg = y_intra + y_inter
                if direction == 0:
                    yg = yg + dsk_ref[:, g * LANES:(g + 1) * LANES] * xg
                outs.append(yg)
            yc = jnp.concatenate(outs, axis=1)
            if direction == 0:
                y_ref[0, pl.ds(r0, CHUNK), :] = yc
            else:
                y_ref[0, pl.ds(r0, CHUNK), :] = y_ref[0, pl.ds(r0, CHUNK), :] + yc
            return carry

        lax.fori_loop(0, nch, body, 0)

    scan(0)
    scan(1)


def _ssd(xbc, dt, cw, cb, dtb, alog, dsk, *, cl):
    b, s, _ = xbc.shape
    full = lambda a: pl.BlockSpec(a.shape, lambda bi: (0,) * a.ndim)
    return pl.pallas_call(
        functools.partial(_ssd_kernel, nch=s // CHUNK, cl=cl),
        grid=(b,),
        in_specs=[pl.BlockSpec((1, s, XBC_W), lambda bi: (bi, 0, 0)), pl.BlockSpec((1, s, DT_PAD), lambda bi: (bi, 0, 0)),
                  full(cw), full(cb), full(dtb), full(alog), full(dsk)],
        out_specs=pl.BlockSpec((1, s, SSM_W), lambda bi: (bi, 0, 0)),
        out_shape=jax.ShapeDtypeStruct((b, s, SSM_W), F32),
        scratch_shapes=[pltpu.VMEM((s, XBC_W), BF16), pltpu.VMEM((SSM_STATE, SSM_W), F32)],
        compiler_params=_cparams(("parallel",)),
        name="ssd",
    )(xbc, dt, cw, cb, dtb, alog, dsk)


def _outproj_kernel(x_ref, mod_ref, gm_ref, att_ref, y_ref, z_ref, ng_ref, g2_ref, w_ref, xo_ref, hf_ref, *, tr, cl):
    gated = y_ref[0] * _silu(z_ref[0].astype(F32))
    ssm = gated * lax.rsqrt(jnp.mean(gated * gated, axis=-1, keepdims=True) + EPS) * ng_ref[...]
    mix = jnp.concatenate([gm_ref[0], att_ref[0], ssm.astype(BF16)], axis=1)
    gate1, shift2, scale2 = _mod_rows(mod_ref, pl.program_id(1) * tr, tr, cl, (2, 3, 4))
    x = x_ref[0] + gate1 * jnp.dot(mix, w_ref[...], preferred_element_type=F32)
    xo_ref[0] = x
    ms = jnp.mean(x * x, axis=-1, keepdims=True)
    hf_ref[0] = (x * lax.rsqrt(ms + EPS) * g2_ref[...] * (1.0 + scale2) + shift2).astype(hf_ref.dtype)


def _out_projection(xs, mod, gm, att, y, z, ng, g2, w_out, *, cl, nt, hf_dtype):
    b, s, _ = xs.shape
    tr = s // nt
    tok = lambda w: pl.BlockSpec((1, tr, w), lambda bi, j: (bi, j, 0))
    full = lambda a: pl.BlockSpec(a.shape, lambda bi, j: (0,) * a.ndim)
    return pl.pallas_call(
        functools.partial(_outproj_kernel, tr=tr, cl=cl),
        grid=(b, nt),
        in_specs=[tok(D_MODEL), pl.BlockSpec((1, 16, D_MODEL), lambda bi, j: (bi, 0, 0)), tok(GM_W), tok(ATT_W),
                  tok(SSM_W), tok(SSM_W), full(ng), full(g2), full(w_out)],
        out_specs=[tok(D_MODEL), tok(D_MODEL)],
        out_shape=[jax.ShapeDtypeStruct((b, s, D_MODEL), F32), jax.ShapeDtypeStruct((b, s, D_MODEL), hf_dtype)],
        compiler_params=_cparams(("parallel", "parallel")),
        name="out_projection",
    )(xs, mod, gm, att, y, z, ng, g2, w_out)


def _swiglu_rows(h, wg_ref, wu_ref, wd_ref, act_ref):
    for f in range(D_FF // FF_BLOCK):
        cols = slice(f * FF_BLOCK, (f + 1) * FF_BLOCK)
        g = jnp.dot(h, wg_ref[0, :, cols], preferred_element_type=F32)
        u = jnp.dot(h, wu_ref[0, :, cols], preferred_element_type=F32)
        act_ref[:, cols] = (_silu(g) * u).astype(BF16)
    return jnp.dot(act_ref[...], wd_ref[0], preferred_element_type=F32)


def _ffn_kernel(x_ref, hf_ref, mod_ref, wg_ref, wu_ref, wd_ref, xo_ref, act_ref, *, tr, cl):
    (gate2,) = _mod_rows(mod_ref, pl.program_id(1) * tr, tr, cl, (5,))
    xo_ref[0] = x_ref[0] + gate2 * _swiglu_rows(hf_ref[0], wg_ref, wu_ref, wd_ref, act_ref)


def _dense_ffn(xs, hf, mod, wg, wu, wd, *, cl, nt):
    b, s, _ = xs.shape
    tr = s // nt
    tok = pl.BlockSpec((1, tr, D_MODEL), lambda bi, j: (bi, j, 0))
    full = lambda a: pl.BlockSpec(a.shape, lambda bi, j: (0,) * a.ndim)
    return pl.pallas_call(
        functools.partial(_ffn_kernel, tr=tr, cl=cl),
        grid=(b, nt),
        in_specs=[tok, tok, pl.BlockSpec((1, 16, D_MODEL), lambda bi, j: (bi, 0, 0)), full(wg), full(wu), full(wd)],
        out_specs=tok,
        out_shape=jax.ShapeDtypeStruct((b, s, D_MODEL), F32),
        scratch_shapes=[pltpu.VMEM((tr, D_FF), BF16)],
        compiler_params=_cparams(("parallel", "parallel")),
        name="dense_ffn",
    )(xs, hf, mod, wg, wu, wd)


def _router_kernel(hf_ref, wr_ref, tri_ref, mi_ref, mf_ref, cnt_ref, run_ref, *, trc):
    @pl.when(pl.program_id(0) == 0)
    def _():
        run_ref[...] = jnp.zeros_like(run_ref)

    logits = lax.dot_general(wr_ref[...], hf_ref[...], (((1,), (1,)), ((), ())), precision=HIGHEST,
                             preferred_element_type=F32)
    eidx = lax.broadcasted_iota(I32, (N_EXPERTS, trc), 0).astype(F32)
    m1 = jnp.max(logits, axis=0, keepdims=True)
    i1 = jnp.min(jnp.where(logits == m1, eidx, float(N_EXPERTS)), axis=0, keepdims=True)
    rest = jnp.where(eidx == i1, -jnp.inf, logits)
    m2 = jnp.max(rest, axis=0, keepdims=True)
    i2 = jnp.min(jnp.where(rest == m2, eidx, float(N_EXPERTS)), axis=0, keepdims=True)
    e2 = jnp.exp(m2 - m1)
    g1 = 1.0 / (1.0 + e2)
    g2 = e2 / (1.0 + e2)
    oh1 = (eidx == i1).astype(F32)
    oh2 = (eidx == i2).astype(F32)
    sel = oh1 + oh2
    before = jnp.dot(sel.astype(BF16), tri_ref[...], preferred_element_type=F32) + run_ref[:, 0:1]
    r1 = jnp.sum(oh1 * before, axis=0, keepdims=True)
    r2 = jnp.sum(oh2 * before, axis=0, keepdims=True)
    run_ref[...] = run_ref[...] + jnp.sum(sel, axis=1, keepdims=True)
    zi = jnp.zeros((4, trc), I32)
    mi_ref[0] = jnp.concatenate([i1.astype(I32), i2.astype(I32), r1.astype(I32), r2.astype(I32), zi], axis=0)
    mf_ref[0] = jnp.concatenate([g1, g2, jnp.zeros((6, trc), F32)], axis=0)
    cnt_ref[...] = run_ref[...].astype(I32)


def _router(hf_flat, wr_t, tri, *, trc):
    t = hf_flat.shape[0]
    nt = t // trc
    meta = pl.BlockSpec((1, 8, trc), lambda i: (i, 0, 0))
    return pl.pallas_call(
        functools.partial(_router_kernel, trc=trc),
        grid=(nt,),
        in_specs=[pl.BlockSpec((trc, D_MODEL), lambda i: (i, 0)), pl.BlockSpec(wr_t.shape, lambda i: (0, 0)),
                  pl.BlockSpec(tri.shape, lambda i: (0, 0))],
        out_specs=[meta, meta, pl.BlockSpec((N_EXPERTS, LANES), lambda i: (0, 0))],
        out_shape=[jax.ShapeDtypeStruct((nt, 8, trc), I32), jax.ShapeDtypeStruct((nt, 8, trc), F32),
                   jax.ShapeDtypeStruct((N_EXPERTS, LANES), I32)],
        scratch_shapes=[pltpu.VMEM((N_EXPERTS, LANES), F32)],
        compiler_params=_cparams(("arbitrary",)),
        name="moe_router",
    )(hf_flat, wr_t, tri)


def _row_copy(src_ref, src_row, dst_ref, dst_row, sem):
    return pltpu.make_async_copy(src_ref.at[pl.ds(src_row, 1), :], dst_ref.at[pl.ds(dst_row, 1), :], sem)


def _dispatch_kernel(pad_lo_ref, pad_n_ref, nact_ref, slot_ref, hf_ref, xs_ref, zero_ref, sem, zsem, *, trc, ntile):
    @pl.when(pl.program_id(0) == 0)
    def _():
        zero_ref[...] = jnp.zeros_like(zero_ref)

        def tile_copy(i):
            return pltpu.make_async_copy(zero_ref, xs_ref.at[pl.ds(i * EXPERT_TILE, EXPERT_TILE), :], zsem)

        def fill_tile(i, carry):
            tile_copy(i).start()
            return carry

        def drain_tile(i, carry):
            tile_copy(i).wait()
            return carry

        lax.fori_loop(nact_ref[0], ntile, fill_tile, 0)
        lax.fori_loop(nact_ref[0], ntile, drain_tile, 0)
        for e in range(N_EXPERTS):
            lo = pad_lo_ref[e]
            n = pad_n_ref[e]

            def fill(r, carry):
                _row_copy(zero_ref, 0, xs_ref, lo + r, zsem).start()
                return carry

            def drain(r, carry):
                _row_copy(zero_ref, 0, xs_ref, lo + r, zsem).wait()
                return carry

            lax.fori_loop(0, n, fill, 0)
            lax.fori_loop(0, n, drain, 0)

    def issue(r, carry):
        _row_copy(hf_ref, r, xs_ref, slot_ref[0, 0, r], sem).start()
        _row_copy(hf_ref, r, xs_ref, slot_ref[0, 1, r], sem).start()
        return carry

    def drain(r, carry):
        _row_copy(hf_ref, r, xs_ref, slot_ref[0, 0, r], sem).wait()
        _row_copy(hf_ref, r, xs_ref, slot_ref[0, 1, r], sem).wait()
        return carry

    lax.fori_loop(0, trc, issue, 0)
    lax.fori_loop(0, trc, drain, 0)


def _dispatch(pad_lo, pad_n, nact, slots, hf_flat, nslot, *, trc):
    t = hf_flat.shape[0]
    nt = t // trc
    grid_spec = pltpu.PrefetchScalarGridSpec(
        num_scalar_prefetch=3,
        grid=(nt,),
        in_specs=[pl.BlockSpec((1, 2, trc), lambda i, lo, n, na: (i, 0, 0), memory_space=pltpu.SMEM),
                  pl.BlockSpec((trc, D_MODEL), lambda i, lo, n, na: (i, 0))],
        out_specs=pl.BlockSpec(memory_space=pl.ANY),
        scratch_shapes=[pltpu.VMEM((EXPERT_TILE, D_MODEL), F32), pltpu.SemaphoreType.DMA(()),
                        pltpu.SemaphoreType.DMA(())],
    )
    return pl.pallas_call(
        functools.partial(_dispatch_kernel, trc=trc, ntile=nslot // EXPERT_TILE),
        grid_spec=grid_spec,
        out_shape=jax.ShapeDtypeStruct((nslot, D_MODEL), F32),
        compiler_params=_cparams(("arbitrary",)),
        name="moe_dispatch",
    )(pad_lo, pad_n, nact, slots, hf_flat)


def _expert_kernel(texp_ref, nact_ref, xs_ref, wg_ref, wu_ref, wd_ref, ys_ref, act_ref):
    active = pl.program_id(0) < nact_ref[0]

    @pl.when(active)
    def _():
        ys_ref[...] = _swiglu_rows(xs_ref[...].astype(BF16), wg_ref, wu_ref, wd_ref, act_ref)

    @pl.when(jnp.logical_not(active))
    def _():
        ys_ref[...] = jnp.zeros_like(ys_ref)


def _expert_ffn(tile_expert, nact, xs, wg, wu, wd):
    nslot = xs.shape[0]
    ntile = nslot // EXPERT_TILE
    wspec = lambda shp: pl.BlockSpec((1,) + shp, lambda i, te, na: (te[i], 0, 0))
    grid_spec = pltpu.PrefetchScalarGridSpec(
        num_scalar_prefetch=2,
        grid=(ntile,),
        in_specs=[pl.BlockSpec((EXPERT_TILE, D_MODEL), lambda i, te, na: (i, 0)),
                  wspec((D_MODEL, D_FF)), wspec((D_MODEL, D_FF)), wspec((D_FF, D_MODEL))],
        out_specs=pl.BlockSpec((EXPERT_TILE, D_MODEL), lambda i, te, na: (i, 0)),
        scratch_shapes=[pltpu.VMEM((EXPERT_TILE, D_FF), BF16)],
    )
    return pl.pallas_call(
        _expert_kernel,
        grid_spec=grid_spec,
        out_shape=jax.ShapeDtypeStruct((nslot, D_MODEL), F32),
        compiler_params=_cparams(("arbitrary",)),
        name="moe_experts",
    )(tile_expert, nact, xs, wg, wu, wd)


def _combine_kernel(slot_ref, x_ref, gate_ref, mod_ref, ys_ref, xo_ref, buf_ref, sem, *, trc, nt, cl):
    def issue(r, carry):
        _row_copy(ys_ref, slot_ref[0, 0, r], buf_ref.at[0], r, sem).start()
        _row_copy(ys_ref, slot_ref[0, 1, r], buf_ref.at[1], r, sem).start()
        return carry

    def drain(r, carry):
        _row_copy(ys_ref, slot_ref[0, 0, r], buf_ref.at[0], r, sem).wait()
        _row_copy(ys_ref, slot_ref[0, 1, r], buf_ref.at[1], r, sem).wait()
        return carry

    lax.fori_loop(0, trc, issue, 0)
    lax.fori_loop(0, trc, drain, 0)
    (gate2,) = _mod_rows(mod_ref, (pl.program_id(0) % nt) * trc, trc, cl, (5,))
    gt = gate_ref[0]
    f = gt[:, 0:1] * buf_ref[0] + gt[:, 1:2] * buf_ref[1]
    xo_ref[...] = x_ref[...] + gate2 * f


def _combine(slots, x_flat, gates_t, mod, ys, *, trc, nt, cl):
    t = x_flat.shape[0]
    ntile = t // trc
    tok = pl.BlockSpec((trc, D_MODEL), lambda i: (i, 0))
    return pl.pallas_call(
        functools.partial(_combine_kernel, trc=trc, nt=nt, cl=cl),
        grid=(ntile,),
        in_specs=[pl.BlockSpec((1, 2, trc), lambda i: (i, 0, 0), memory_space=pltpu.SMEM), tok,
                  pl.BlockSpec((1, trc, 2), lambda i: (i, 0, 0)),
                  pl.BlockSpec((1, 16, D_MODEL), lambda i: (i // nt, 0, 0)), pl.BlockSpec(memory_space=pl.ANY)],
        out_specs=tok,
        out_shape=jax.ShapeDtypeStruct((t, D_MODEL), F32),
        scratch_shapes=[pltpu.VMEM((2, trc, D_MODEL), F32), pltpu.SemaphoreType.DMA(())],
        compiler_params=_cparams(("arbitrary",)),
        name="moe_combine",
    )(slots, x_flat, gates_t, mod, ys)


def _moe_ffn(xs, hf, mod, wr_t, tri, wg, wu, wd, *, cl, nt):
    b, s, _ = xs.shape
    t = b * s
    trr = tri.shape[0]
    trc = s // nt
    hf_flat = hf.reshape(t, D_MODEL)
    meta_i, meta_f, counts = _router(hf_flat, wr_t, tri, trc=trr)
    counts = counts[:, 0]
    padded = (counts + EXPERT_TILE - 1) // EXPERT_TILE * EXPERT_TILE
    ends = jnp.cumsum(padded)
    starts = ends - padded
    slots = starts[meta_i[:, 0:2, :]] + meta_i[:, 2:4, :]
    nslot = 2 * t + N_EXPERTS * EXPERT_TILE
    ntile = nslot // EXPERT_TILE
    nact = (ends[-1] // EXPERT_TILE).astype(I32)
    tile_lo = jnp.minimum(jnp.arange(ntile, dtype=I32), nact - 1) * EXPERT_TILE
    tile_expert = jnp.minimum(jnp.sum(tile_lo[:, None] >= ends[None, :], axis=1), N_EXPERTS - 1).astype(I32)
    nact = nact.reshape(1)
    xs_sorted = _dispatch((starts + counts).astype(I32), (padded - counts).astype(I32), nact, slots, hf_flat, nslot,
                          trc=trr)
    ys = _expert_ffn(tile_expert, nact, xs_sorted, wg, wu, wd)
    retile = lambda a: jnp.swapaxes(jnp.swapaxes(a, 0, 1).reshape(2, t // trc, trc), 0, 1)
    gates_t = jnp.swapaxes(retile(meta_f[:, 0:2, :]), 1, 2)
    out = _combine(retile(slots), xs.reshape(t, D_MODEL), gates_t, mod, ys, trc=trc, nt=nt, cl=cl)
    return out.reshape(b, s, D_MODEL)


def _slab_perm(t, lead):
    nh = t.shape[-1] // HEAD_DIM
    per = nh // 2
    t = t.reshape(lead + (2, per, 2, 2, 16))
    t = jnp.moveaxis(t, (-5, -4, -3, -2, -1), (-3, -5, -2, -4, -1))
    return t.reshape(lead + (nh * HEAD_DIM,))


def _rope_tables(s, cl):
    l = s - cl
    pos = np.arange(l)
    lane = np.arange(LANES)
    i = lane % 32
    freq = ROPE_BASE ** (-(i % 16).astype(np.float32) / 16.0)
    p = np.where(i[None, :] < 16, (pos // GRID_W)[:, None], (pos % GRID_W)[:, None]).astype(np.float32)
    ang = p * freq[None, :].astype(np.float32)
    sign = np.where(lane < LANES // 2, -1.0, 1.0).astype(np.float32)
    cos = np.concatenate([np.ones((cl, LANES), np.float32), np.cos(ang)], axis=0)
    sin = np.concatenate([np.zeros((cl, LANES), np.float32), np.sin(ang) * sign[None, :]], axis=0)
    return jnp.asarray(cos, F32), jnp.asarray(sin, F32)


def kernel(x, c, ctx, c_ctx, w_mod, b_mod, norm1_g, norm2_g, w_in, w_out, gm_v_g, gm_ws, gm_bs, att_q_g, att_k_g, att_sink, ssm_conv_w, ssm_conv_b, ssm_dt_bias, ssm_a_log, ssm_d, ssm_norm_g, ffn_w_gate, ffn_w_up, ffn_w_down, moe_router, moe_w_gate, moe_w_up, moe_w_down):
    b, l, _ = x.shape
    cl = ctx.shape[1]
    s = cl + l
    depth = w_mod.shape[0]
    nt = 8
    nt_moe = 4
    assert s % (nt * 16) == 0 and cl % CHUNK == 0 and l % CHUNK == 0 and l % GRID_W == 0 and b < 16

    cvec = jnp.concatenate([c, c_ctx[None, :], jnp.zeros((16 - b - 1, D_MODEL), F32)], axis=0)
    mods = _modulation(cvec, w_mod, b_mod)
    lat = jnp.moveaxis(mods[:, :, :b, :], 2, 1)
    con = jnp.broadcast_to(mods[:, None, :, b, :], (depth, b, 6, D_MODEL))
    pad2 = jnp.zeros((depth, b, 2, D_MODEL), F32)
    modtab = jnp.concatenate([lat, pad2, con, pad2], axis=2)

    wi = w_in
    w_in_p = jnp.concatenate([
        wi[:, :, 0:2 * GM_W],
        _slab_perm(wi[:, :, 512:1024], (depth, D_MODEL)),
        _slab_perm(wi[:, :, 1024:1152], (depth, D_MODEL)),
        wi[:, :, 1152:2304],
        jnp.pad(wi[:, :, 2304:2312], ((0, 0), (0, 0), (0, DT_PAD - 2 * SSM_HEADS))),
    ], axis=2).astype(BF16)
    wo_att = w_out[:, GM_W:GM_W + ATT_W, :].reshape(depth, 2, 4, HEAD_DIM, D_MODEL)
    wo_att = jnp.swapaxes(wo_att, 1, 2).reshape(depth, ATT_W, D_MODEL)
    w_out_p = jnp.concatenate([w_out[:, :GM_W], wo_att, w_out[:, GM_W + ATT_W:]], axis=1).astype(BF16)
    gq = _slab_perm(jnp.tile(att_q_g, (1, 2)), (depth,))[:, None, :]
    gk = _slab_perm(jnp.tile(att_k_g, (1, 2)), (depth,))[:, None, :]
    lane = np.arange(LANES)
    seg = jnp.asarray(((lane[:, None] // 32) % 2 == (lane[None, :] // 32) % 2), BF16)
    lane2 = np.arange(GM_W)
    seg64 = jnp.asarray(lane2[:, None] // GM_HD == lane2[None, :] // GM_HD, BF16)
    cos, sin = _rope_tables(s, cl)
    ws_cat = jnp.swapaxes(gm_ws, 1, 2).reshape(depth, CHUNK, GM_HEADS * CHUNK).astype(BF16)
    bs_exp = jnp.repeat(jnp.swapaxes(gm_bs, 1, 2), GM_HD, axis=2)
    sink_heads = att_sink.reshape(depth, 2, 4).swapaxes(1, 2).reshape(depth, ATT_HEADS)
    sink_col = jnp.repeat(sink_heads, CHUNK, axis=1)[:, :, None]
    cw = jnp.pad(ssm_conv_w, ((0, 0), (0, 8 - CONV_K), (0, 0)))
    cb = ssm_conv_b[:, None, :]
    pad_dt = lambda t: jnp.pad(t.reshape(depth, 1, 2 * SSM_HEADS), ((0, 0), (0, 0), (0, DT_PAD - 2 * SSM_HEADS)))
    dtb = pad_dt(ssm_dt_bias)
    alog = pad_dt(ssm_a_log)
    dsk = jnp.repeat(ssm_d, SSM_HD, axis=1)[:, None, :]
    wr_t = jnp.swapaxes(moe_router, 1, 2)
    trr = 1024 if (b * s) % 1024 == 0 else 512
    assert (b * s) % trr == 0
    tidx = np.arange(trr)
    tri = jnp.asarray(tidx[:, None] < tidx[None, :], BF16)
    ffn_g, ffn_u, ffn_d = ffn_w_gate.astype(BF16), ffn_w_up.astype(BF16), ffn_w_down.astype(BF16)
    moe_g, moe_u, moe_d = moe_w_gate.astype(BF16), moe_w_up.astype(BF16), moe_w_down.astype(BF16)

    xs = jnp.concatenate([ctx, x], axis=1)
    for i in range(depth):
        moe = i % 2 == 1
        mod = modtab[i]
        gu, gv, q, k, v, z, xbc, dt = _in_projection(
            xs, mod, norm1_g[i][None, :], w_in_p[i], cos, sin, gq[i], gk[i], gm_v_g[i][None, :], seg, seg64,
            cl=cl, nt=nt)
        gm = _gmlp(gu, gv, ws_cat[i], bs_exp[i])
        att = _attention(q, k, v, sink_col[i], cl=cl)
        y = _ssd(xbc, dt, cw[i], cb[i], dtb[i], alog[i], dsk[i], cl=cl)
        xs, hf = _out_projection(xs, mod, gm, att, y, z, ssm_norm_g[i][None, :], norm2_g[i][None, :], w_out_p[i],
                                 cl=cl, nt=nt, hf_dtype=F32 if moe else BF16)
        j = i // 2
        if moe:
            xs = _moe_ffn(xs, hf, mod, wr_t[j], tri, moe_g[j], moe_u[j], moe_d[j], cl=cl, nt=nt_moe)
        else:
            xs = _dense_ffn(xs, hf, mod, ffn_g[j:j + 1], ffn_u[j:j + 1], ffn_d[j:j + 1], cl=cl, nt=nt)
    return xs[:, cl:, :]
```

```python
import functools
import math

import numpy as np
import jax
import jax.numpy as jnp
from jax import lax
from jax.experimental import pallas as pl
from jax.experimental.pallas import tpu as pltpu

F32 = jnp.float32
BF16 = jnp.bfloat16
I32 = jnp.int32
HIGHEST = lax.Precision.HIGHEST

D_MODEL = 1024
CHUNK = 128
GM_HEADS, GM_HD, GM_W = 4, 64, 256
ATT_HEADS, KV_HEADS, HEAD_DIM = 8, 2, 64
ATT_W, KV_W = 512, 128
GRID_W = 64
ROPE_BASE = 10000.0
SSM_HEADS, SSM_HD, SSM_W = 4, 64, 256
SSM_STATE, BC_W, CONV_K, XBC_W = 128, 256, 5, 768
D_FF = 2816
N_EXPERTS = 8
EPS = 1e-6
LANES = 128
DT_PAD = 128
C_GU, C_GV, C_Q, C_K, C_V, C_Z, C_XBC, C_DT, IN_WP = 0, 256, 512, 1024, 1152, 1280, 1536, 2304, 2432
FF_BLOCK = 256
EXPERT_TILE = 512
VMEM_LIMIT = 56 * 1024 * 1024


def _cparams(sem):
    return pltpu.CompilerParams(dimension_semantics=sem, vmem_limit_bytes=VMEM_LIMIT)


def _silu(x):
    return x * jax.nn.sigmoid(x)


def _mod_kernel(c_ref, w_ref, b_ref, o_ref):
    cs = _silu(c_ref[...])
    o_ref[0, 0] = jnp.dot(cs, w_ref[0], precision=HIGHEST, preferred_element_type=F32) + b_ref[0, 0]


def _modulation(cvec, w_mod, b_mod):
    depth = w_mod.shape[0]
    r = cvec.shape[0]
    return pl.pallas_call(
        _mod_kernel,
        grid=(depth, 6),
        in_specs=[
            pl.BlockSpec((r, D_MODEL), lambda i, n: (0, 0)),
            pl.BlockSpec((1, D_MODEL, D_MODEL), lambda i, n: (i, 0, n)),
            pl.BlockSpec((1, 1, 1, D_MODEL), lambda i, n: (i, n, 0, 0)),
        ],
        out_specs=pl.BlockSpec((1, 1, r, D_MODEL), lambda i, n: (i, n, 0, 0)),
        out_shape=jax.ShapeDtypeStruct((depth, 6, r, D_MODEL), F32),
        compiler_params=_cparams(("arbitrary", "arbitrary")),
        name="modulation",
    )(cvec, w_mod, b_mod.reshape(depth, 6, 1, D_MODEL))


def _mod_rows(mod_ref, tile_row0, rows, cl, lo):
    ridx = tile_row0 + lax.broadcasted_iota(I32, (rows, 1), 0)
    is_ctx = ridx < cl
    m = mod_ref[0]
    return [jnp.where(is_ctx, m[8 + k:9 + k], m[k:k + 1]) for k in lo]


def _inproj_kernel(x_ref, mod_ref, g1_ref, w_ref, cos_ref, sin_ref, gq_ref, gk_ref, gvg_ref, seg_ref, seg64_ref,
                   gu_ref, gv_ref, q_ref, k_ref, v_ref, z_ref, xbc_ref, dt_ref, *, tr, cl):
    x = x_ref[0]
    ms = jnp.mean(x * x, axis=-1, keepdims=True)
    y = x * lax.rsqrt(ms + EPS) * g1_ref[...]
    shift, scale = _mod_rows(mod_ref, pl.program_id(1) * tr, tr, cl, (0, 1))
    h = (y * (1.0 + scale) + shift).astype(BF16)

    def proj(a, b):
        return jnp.dot(h, w_ref[:, a:b], preferred_element_type=F32)

    guv = proj(C_GU, C_Q)
    gu_ref[0] = jax.nn.gelu(guv[:, :GM_W]).astype(BF16)
    gv = jax.nn.gelu(guv[:, GM_W:])
    ssv = jnp.dot((gv * gv).astype(BF16), seg64_ref[...], preferred_element_type=F32)
    gv_ref[0] = (gv * lax.rsqrt(ssv * (1.0 / GM_HD) + EPS) * gvg_ref[...]).astype(BF16)

    cos = cos_ref[...]
    sin = sin_ref[...]

    def norm_rope(t, gain):
        ss = jnp.dot((t * t).astype(BF16), seg_ref[...], preferred_element_type=F32)
        tn = t * lax.rsqrt(ss * (1.0 / HEAD_DIM) + EPS) * gain
        return tn * cos + pltpu.roll(tn, LANES // 2, axis=1) * sin

    qkv = proj(C_Q, C_Z)
    gq = gq_ref[...] * (HEAD_DIM ** -0.5)
    for c in range(ATT_W // LANES):
        q_ref[0, :, c * LANES:(c + 1) * LANES] = norm_rope(qkv[:, c * LANES:(c + 1) * LANES], gq).astype(BF16)
    k_ref[0] = norm_rope(qkv[:, ATT_W:ATT_W + KV_W], gk_ref[...]).astype(BF16)
    v_ref[0] = qkv[:, ATT_W + KV_W:].astype(BF16)
    z_ref[0] = proj(C_Z, C_XBC).astype(BF16)
    xbc_ref[0] = proj(C_XBC, C_DT).astype(BF16)
    dt_ref[0] = proj(C_DT, IN_WP)


def _in_projection(xs, mod, g1, w_in, cos, sin, gq, gk, gvg, seg, seg64, *, cl, nt):
    b, s, _ = xs.shape
    tr = s // nt
    tok = lambda w: pl.BlockSpec((1, tr, w), lambda bi, j: (bi, j, 0))
    full = lambda a: pl.BlockSpec(a.shape, lambda bi, j: (0,) * a.ndim)
    widths = (GM_W, GM_W, ATT_W, KV_W, KV_W, SSM_W, XBC_W)
    return pl.pallas_call(
        functools.partial(_inproj_kernel, tr=tr, cl=cl),
        grid=(b, nt),
        in_specs=[tok(D_MODEL), pl.BlockSpec((1, 16, D_MODEL), lambda bi, j: (bi, 0, 0)), full(g1), full(w_in),
                  pl.BlockSpec((tr, LANES), lambda bi, j: (j, 0)), pl.BlockSpec((tr, LANES), lambda bi, j: (j, 0)),
                  full(gq), full(gk), full(gvg), full(seg), full(seg64)],
        out_specs=[tok(w) for w in widths] + [tok(DT_PAD)],
        out_shape=[jax.ShapeDtypeStruct((b, s, w), BF16) for w in widths]
        + [jax.ShapeDtypeStruct((b, s, DT_PAD), F32)],
        compiler_params=_cparams(("parallel", "parallel")),
        name="in_projection",
    )(xs, mod, g1, w_in, cos, sin, gq, gk, gvg, seg, seg64)


def _gmlp_kernel(gu_ref, gv_ref, ws_ref, bs_ref, o_ref, *, nch):
    head = lax.broadcasted_iota(I32, (CHUNK, GM_W), 1) >> 6

    def body(c, carry):
        r0 = pl.multiple_of(c * CHUNK, CHUNK)
        v = gv_ref[0, pl.ds(r0, CHUNK), :]
        vbd = jnp.concatenate([jnp.where(head == hh, v, jnp.zeros_like(v)) for hh in range(GM_HEADS)], axis=0)
        sp = jnp.dot(ws_ref[...], vbd, preferred_element_type=F32) + bs_ref[...]
        o_ref[0, pl.ds(r0, CHUNK), :] = (gu_ref[0, pl.ds(r0, CHUNK), :].astype(F32) * sp).astype(BF16)
        return carry

    lax.fori_loop(0, nch, body, 0)


def _gmlp(gu, gv, ws_cat, bs_exp):
    b, s, _ = gu.shape
    seq = pl.BlockSpec((1, s, GM_W), lambda bi: (bi, 0, 0))
    return pl.pallas_call(
        functools.partial(_gmlp_kernel, nch=s // CHUNK),
        grid=(b,),
        in_specs=[seq, seq, pl.BlockSpec(ws_cat.shape, lambda bi: (0, 0)), pl.BlockSpec(bs_exp.shape, lambda bi: (0, 0))],
        out_specs=seq,
        out_shape=jax.ShapeDtypeStruct((b, s, GM_W), BF16),
        compiler_params=_cparams(("parallel",)),
        name="gmlp",
    )(gu, gv, ws_cat, bs_exp)


def _attn_kernel(q_ref, k_ref, v_ref, sink_ref, o_ref, *, nch, cl):
    ncb = cl // CHUNK
    nslab = ATT_W // LANES
    lane = lax.broadcasted_iota(I32, (CHUNK, LANES), 1)
    first_head = ((lane >> 5) & 1) == 0
    low_half = lane < LANES // 2
    kc = k_ref[0, 0:cl, :]
    vc = v_ref[0, 0:cl, :]
    qi = lax.broadcasted_iota(I32, (CHUNK, CHUNK), 0)
    kl = lax.broadcasted_iota(I32, (CHUNK, CHUNK), 1)
    ninf = jnp.full((CHUNK, CHUNK), -jnp.inf, F32)
    zeros = jnp.zeros((CHUNK, CHUNK), F32)
    bias_prev = jnp.where(kl >= qi, zeros, ninf)
    bias_next = jnp.where(kl <= qi, zeros, ninf)

    def body(n, carry):
        r0 = pl.multiple_of(n * CHUNK, CHUNK)
        rp = pl.multiple_of(jnp.maximum(n - 1, ncb) * CHUNK, CHUNK)
        rn = pl.multiple_of(jnp.minimum(n + 1, nch - 1) * CHUNK, CHUNK)
        q = q_ref[0, pl.ds(r0, CHUNK), :]
        kk = jnp.concatenate([kc, k_ref[0, pl.ds(rp, CHUNK), :], k_ref[0, pl.ds(r0, CHUNK), :],
                              k_ref[0, pl.ds(rn, CHUNK), :]], axis=0)
        vv = jnp.concatenate([vc, v_ref[0, pl.ds(rp, CHUNK), :], v_ref[0, pl.ds(r0, CHUNK), :],
                              v_ref[0, pl.ds(rn, CHUNK), :]], axis=0)
        is_lat = n >= ncb
        bias = jnp.concatenate([
            jnp.where(jnp.logical_and(is_lat, n - 1 >= ncb), bias_prev, ninf),
            jnp.where(is_lat, zeros, ninf),
            jnp.where(jnp.logical_and(is_lat, n + 1 <= nch - 1), bias_next, ninf)], axis=1)
        zero = jnp.zeros((CHUNK, LANES), BF16)
        blocks = []
        for c in range(nslab):
            qc = q[:, c * LANES:(c + 1) * LANES]
            blocks += [jnp.where(first_head, qc, zero), jnp.where(first_head, zero, qc)]
        qs = jnp.concatenate(blocks, axis=0)
        sc = lax.dot_general(qs, kk, (((1,), (1,)), ((), ())), preferred_element_type=F32)
        s_ctx = sc[:, :cl]
        s_loc = sc[:, cl:] + jnp.concatenate([bias] * (2 * nslab), axis=0)
        sk = sink_ref[...]

        def lane_tiles(fn, a, b):
            tiles = [a[:, t * LANES:(t + 1) * LANES] for t in range(a.shape[1] // LANES)]
            tiles += [b[:, t * LANES:(t + 1) * LANES] for t in range(b.shape[1] // LANES)]
            return functools.reduce(fn, tiles)

        m = jnp.maximum(jnp.max(lane_tiles(jnp.maximum, s_ctx, s_loc), axis=-1, keepdims=True), sk)
        e_ctx = jnp.exp(s_ctx - m)
        e_loc = jnp.exp(s_loc - m)
        den = jnp.sum(lane_tiles(jnp.add, e_ctx, e_loc), axis=-1, keepdims=True) + jnp.exp(sk - m)
        p = jnp.concatenate([e_ctx.astype(BF16), e_loc.astype(BF16)], axis=1)
        o = jnp.dot(p, vv, preferred_element_type=F32) * (1.0 / den)
        for c in range(nslab):
            oa = o[(2 * c) * CHUNK:(2 * c + 1) * CHUNK]
            ob = o[(2 * c + 1) * CHUNK:(2 * c + 2) * CHUNK]
            o_ref[0, pl.ds(r0, CHUNK), c * LANES:(c + 1) * LANES] = jnp.where(low_half, oa, ob).astype(BF16)
        return carry

    lax.fori_loop(0, nch, body, 0)


def _attention(q, k, v, sink_col, *, cl):
    b, s, _ = q.shape
    seq = lambda w: pl.BlockSpec((1, s, w), lambda bi: (bi, 0, 0))
    return pl.pallas_call(
        functools.partial(_attn_kernel, nch=s // CHUNK, cl=cl),
        grid=(b,),
        in_specs=[seq(ATT_W), seq(KV_W), seq(KV_W), pl.BlockSpec(sink_col.shape, lambda bi: (0, 0))],
        out_specs=seq(ATT_W),
        out_shape=jax.ShapeDtypeStruct((b, s, ATT_W), BF16),
        compiler_params=_cparams(("parallel",)),
        name="attention",
    )(q, k, v, sink_col)


def _ssd_kernel(xbc_ref, dt_ref, cw_ref, cb_ref, dtb_ref, alog_ref, dsk_ref, y_ref,
                xc_ref, dtv_ref, ac_ref, act_ref, yb_ref, st_ref, *, nch, cl):
    ncb = cl // CHUNK
    s = nch * CHUNK
    halo = 16

    def conv_body(c, carry):
        r0 = pl.multiple_of(c * CHUNK, CHUNK)
        has_prev = jnp.logical_and(c != 0, c != ncb)
        has_next = jnp.logical_and(c != ncb - 1, c != nch - 1)
        rp = pl.multiple_of(jnp.maximum(r0 - halo, 0), halo)
        rn = pl.multiple_of(jnp.minimum(r0 + CHUNK, s - halo), halo)
        prev = jnp.where(has_prev, xbc_ref[0, pl.ds(rp, halo), :].astype(F32), 0.0)
        nxt = jnp.where(has_next, xbc_ref[0, pl.ds(rn, halo), :].astype(F32), 0.0)
        cat = jnp.concatenate([prev, xbc_ref[0, pl.ds(r0, CHUNK), :].astype(F32), nxt], axis=0)
        acc = jnp.zeros((CHUNK, XBC_W), F32) + cb_ref[...]
        for kk in range(CONV_K):
            off = halo - CONV_K // 2 + kk
            acc = acc + cw_ref[kk:kk + 1, :] * cat[off:off + CHUNK]
        xc_ref[pl.ds(r0, CHUNK), :] = _silu(acc).astype(BF16)
        dtv = jax.nn.softplus(dt_ref[0, pl.ds(r0, CHUNK), :] + dtb)
        dta = dtv * a_neg
        pre = jnp.dot(tri_f, dta, precision=HIGHEST, preferred_element_type=F32)
        acum = jnp.where(tj < SSM_HEADS, pre, pre[CHUNK - 1:CHUNK, :] - pre + dta)
        dtv_ref[pl.ds(r0, CHUNK), :] = dtv
        ac_ref[pl.ds(r0, CHUNK), :] = acum
        act_ref[pl.ds(r0, CHUNK), :] = acum.T
        return carry

    ti = lax.broadcasted_iota(I32, (CHUNK, CHUNK), 0)
    tj = lax.broadcasted_iota(I32, (CHUNK, CHUNK), 1)
    low_half = tj < LANES // 2
    tri_f = (ti >= tj).astype(F32)
    a_neg = -jnp.exp(alog_ref[...])
    dtb = dtb_ref[...]
    lax.fori_loop(0, nch, conv_body, 0)

    def chunk_update(c, direction, out_ref, out_lead):
        tri = (ti >= tj) if direction == 0 else (tj >= ti)
        last = CHUNK - 1 if direction == 0 else 0
        r0 = pl.multiple_of(c * CHUNK, CHUNK)
        xcv = xc_ref[pl.ds(r0, CHUNK), :]
        dtv = dtv_ref[pl.ds(r0, CHUNK), :]
        acum = ac_ref[pl.ds(r0, CHUNK), :]
        acum_t = act_ref[pl.ds(r0, CHUNK), :]
        outs = []
        for g in range(2):
            xg = xcv[:, g * LANES:(g + 1) * LANES].astype(F32)
            bg = xcv[:, SSM_W + g * LANES:SSM_W + (g + 1) * LANES]
            cg = xcv[:, SSM_W + BC_W + g * LANES:SSM_W + BC_W + (g + 1) * LANES]
            gram = lax.dot_general(cg, bg, (((1,), (1,)), ((), ())), preferred_element_type=F32)
            mixes, dtc, eac, wend, dec = [], [], [], [], []
            for hh in range(2):
                col = direction * SSM_HEADS + 2 * g + hh
                ac = acum[:, col:col + 1]
                al = acum[last:last + 1, col:col + 1]
                seg = jnp.exp(jnp.where(tri, ac - acum_t[col:col + 1, :], -jnp.inf))
                mixes.append((gram * seg).astype(BF16))
                dtc.append(dtv[:, col:col + 1])
                eac.append(jnp.exp(ac))
                wend.append(jnp.exp(al - ac) * dtv[:, col:col + 1])
                dec.append(jnp.exp(al))
            pair = lambda t: jnp.where(low_half, t[0], t[1])
            xdt = (xg * pair(dtc)).astype(BF16)
            y_intra = jnp.where(low_half, jnp.dot(mixes[0], xdt, preferred_element_type=F32),
                                jnp.dot(mixes[1], xdt, preferred_element_type=F32))
            stg = st_ref[direction, :, g * LANES:(g + 1) * LANES]
            y_inter = jnp.dot(cg, stg.astype(BF16), preferred_element_type=F32) * pair(eac)
            xw = (xg * pair(wend)).astype(BF16)
            contrib = lax.dot_general(bg, xw, (((0,), (0,)), ((), ())), preferred_element_type=F32)
            st_ref[direction, :, g * LANES:(g + 1) * LANES] = stg * pair(dec)[0:1, :] + contrib
            yg = y_intra + y_inter
            if direction == 0:
                yg = yg + dsk_ref[:, g * LANES:(g + 1) * LANES] * xg
            outs.append(yg)
        out_ref[out_lead + (pl.ds(r0, CHUNK), slice(None))] = jnp.concatenate(outs, axis=1)

    st_ref[...] = jnp.zeros_like(st_ref)

    def scan_body(step, carry):
        chunk_update(step, 0, y_ref, (0,))
        cb = jnp.where(step < ncb, ncb - 1 - step, nch - 1 - (step - ncb))
        chunk_update(cb, 1, yb_ref, ())
        return carry

    lax.fori_loop(0, nch, scan_body, 0)
    y_ref[0] = y_ref[0] + yb_ref[...]


def _ssd(xbc, dt, cw, cb, dtb, alog, dsk, *, cl):
    b, s, _ = xbc.shape
    full = lambda a: pl.BlockSpec(a.shape, lambda bi: (0,) * a.ndim)
    return pl.pallas_call(
        functools.partial(_ssd_kernel, nch=s // CHUNK, cl=cl),
        grid=(b,),
        in_specs=[pl.BlockSpec((1, s, XBC_W), lambda bi: (bi, 0, 0)), pl.BlockSpec((1, s, DT_PAD), lambda bi: (bi, 0, 0)),
                  full(cw), full(cb), full(dtb), full(alog), full(dsk)],
        out_specs=pl.BlockSpec((1, s, SSM_W), lambda bi: (bi, 0, 0)),
        out_shape=jax.ShapeDtypeStruct((b, s, SSM_W), F32),
        scratch_shapes=[pltpu.VMEM((s, XBC_W), BF16), pltpu.VMEM((s, DT_PAD), F32), pltpu.VMEM((s, DT_PAD), F32),
                        pltpu.VMEM((s, DT_PAD), F32), pltpu.VMEM((s, SSM_W), F32),
                        pltpu.VMEM((2, SSM_STATE, SSM_W), F32)],
        compiler_params=_cparams(("parallel",)),
        name="ssd",
    )(xbc, dt, cw, cb, dtb, alog, dsk)


TOK_ROWS = D_MODEL // LANES


def _load_tokens(ref, lead, n):
    return jnp.concatenate([ref[lead + (pl.ds(j, n, stride=TOK_ROWS), slice(None))] for j in range(TOK_ROWS)], axis=1)


def _store_tokens(ref, lead, val):
    for j in range(TOK_ROWS):
        ref[lead + (pl.ds(j, val.shape[0], stride=TOK_ROWS), slice(None))] = val[:, j * LANES:(j + 1) * LANES]


def _outproj_kernel(x_ref, mod_ref, gm_ref, att_ref, y_ref, z_ref, ng_ref, g2_ref, w_ref, xo_ref, hf_ref, *, tr, cl,
                    token_tiles):
    gated = y_ref[0] * _silu(z_ref[0].astype(F32))
    ssm = gated * lax.rsqrt(jnp.mean(gated * gated, axis=-1, keepdims=True) + EPS) * ng_ref[...]
    mix = jnp.concatenate([gm_ref[0], att_ref[0], ssm.astype(BF16)], axis=1)
    gate1, shift2, scale2 = _mod_rows(mod_ref, pl.program_id(1) * tr, tr, cl, (2, 3, 4))
    x = x_ref[0] + gate1 * jnp.dot(mix, w_ref[...], preferred_element_type=F32)
    xo_ref[0] = x
    ms = jnp.mean(x * x, axis=-1, keepdims=True)
    hf = x * lax.rsqrt(ms + EPS) * g2_ref[...] * (1.0 + scale2) + shift2
    if token_tiles:
        _store_tokens(hf_ref, (0,), hf)
    else:
        hf_ref[0] = hf.astype(hf_ref.dtype)


def _out_projection(xs, mod, gm, att, y, z, ng, g2, w_out, *, cl, nt, token_tiles):
    b, s, _ = xs.shape
    tr = s // nt
    tok = lambda w: pl.BlockSpec((1, tr, w), lambda bi, j: (bi, j, 0))
    full = lambda a: pl.BlockSpec(a.shape, lambda bi, j: (0,) * a.ndim)
    if token_tiles:
        hf_spec = pl.BlockSpec((1, tr * TOK_ROWS, LANES), lambda bi, j: (bi, j, 0))
        hf_shape = jax.ShapeDtypeStruct((b, s * TOK_ROWS, LANES), F32)
    else:
        hf_spec, hf_shape = tok(D_MODEL), jax.ShapeDtypeStruct((b, s, D_MODEL), BF16)
    return pl.pallas_call(
        functools.partial(_outproj_kernel, tr=tr, cl=cl, token_tiles=token_tiles),
        grid=(b, nt),
        in_specs=[tok(D_MODEL), pl.BlockSpec((1, 16, D_MODEL), lambda bi, j: (bi, 0, 0)), tok(GM_W), tok(ATT_W),
                  tok(SSM_W), tok(SSM_W), full(ng), full(g2), full(w_out)],
        out_specs=[tok(D_MODEL), hf_spec],
        out_shape=[jax.ShapeDtypeStruct((b, s, D_MODEL), F32), hf_shape],
        compiler_params=_cparams(("parallel", "parallel")),
        name="out_projection",
    )(xs, mod, gm, att, y, z, ng, g2, w_out)


def _swiglu_rows(h, wg_ref, wu_ref, wd_ref, act_ref):
    for f in range(D_FF // FF_BLOCK):
        cols = slice(f * FF_BLOCK, (f + 1) * FF_BLOCK)
        g = jnp.dot(h, wg_ref[0, :, cols], preferred_element_type=F32)
        u = jnp.dot(h, wu_ref[0, :, cols], preferred_element_type=F32)
        act_ref[:, cols] = (_silu(g) * u).astype(BF16)
    return jnp.dot(act_ref[...], wd_ref[0], preferred_element_type=F32)


def _ffn_kernel(x_ref, hf_ref, mod_ref, wg_ref, wu_ref, wd_ref, xo_ref, act_ref, *, tr, cl):
    (gate2,) = _mod_rows(mod_ref, pl.program_id(1) * tr, tr, cl, (5,))
    xo_ref[0] = x_ref[0] + gate2 * _swiglu_rows(hf_ref[0], wg_ref, wu_ref, wd_ref, act_ref)


def _dense_ffn(xs, hf, mod, wg, wu, wd, *, cl, nt):
    b, s, _ = xs.shape
    tr = s // nt
    tok = pl.BlockSpec((1, tr, D_MODEL), lambda bi, j: (bi, j, 0))
    full = lambda a: pl.BlockSpec(a.shape, lambda bi, j: (0,) * a.ndim)
    return pl.pallas_call(
        functools.partial(_ffn_kernel, tr=tr, cl=cl),
        grid=(b, nt),
        in_specs=[tok, tok, pl.BlockSpec((1, 16, D_MODEL), lambda bi, j: (bi, 0, 0)), full(wg), full(wu), full(wd)],
        out_specs=tok,
        out_shape=jax.ShapeDtypeStruct((b, s, D_MODEL), F32),
        scratch_shapes=[pltpu.VMEM((tr, D_FF), BF16)],
        compiler_params=_cparams(("parallel", "parallel")),
        name="dense_ffn",
    )(xs, hf, mod, wg, wu, wd)


def _router_kernel(hf_ref, wr_ref, tri_ref, mi_ref, mf_ref, cnt_ref, run_ref, *, trc):
    @pl.when(pl.program_id(0) == 0)
    def _():
        run_ref[...] = jnp.zeros_like(run_ref)

    logits = lax.dot_general(wr_ref[...], _load_tokens(hf_ref, (), trc), (((1,), (1,)), ((), ())), precision=HIGHEST,
                             preferred_element_type=F32)
    eidx = lax.broadcasted_iota(I32, (N_EXPERTS, trc), 0).astype(F32)
    m1 = jnp.max(logits, axis=0, keepdims=True)
    i1 = jnp.min(jnp.where(logits == m1, eidx, float(N_EXPERTS)), axis=0, keepdims=True)
    rest = jnp.where(eidx == i1, -jnp.inf, logits)
    m2 = jnp.max(rest, axis=0, keepdims=True)
    i2 = jnp.min(jnp.where(rest == m2, eidx, float(N_EXPERTS)), axis=0, keepdims=True)
    e2 = jnp.exp(m2 - m1)
    g1 = 1.0 / (1.0 + e2)
    g2 = e2 / (1.0 + e2)
    oh1 = (eidx == i1).astype(F32)
    oh2 = (eidx == i2).astype(F32)
    sel = oh1 + oh2
    before = jnp.dot(sel.astype(BF16), tri_ref[...], preferred_element_type=F32) + run_ref[:, 0:1]
    r1 = jnp.sum(oh1 * before, axis=0, keepdims=True)
    r2 = jnp.sum(oh2 * before, axis=0, keepdims=True)
    run_ref[...] = run_ref[...] + jnp.sum(sel, axis=1, keepdims=True)
    zi = jnp.zeros((4, trc), I32)
    mi_ref[0] = jnp.concatenate([i1.astype(I32), i2.astype(I32), r1.astype(I32), r2.astype(I32), zi], axis=0)
    mf_ref[0] = jnp.concatenate([g1, g2, jnp.zeros((6, trc), F32)], axis=0)
    cnt_ref[...] = run_ref[...].astype(I32)


def _router(hf_tiles, wr_t, tri, *, trc):
    t = hf_tiles.shape[0] // TOK_ROWS
    nt = t // trc
    meta = pl.BlockSpec((1, 8, trc), lambda i: (i, 0, 0))
    return pl.pallas_call(
        functools.partial(_router_kernel, trc=trc),
        grid=(nt,),
        in_specs=[pl.BlockSpec((trc * TOK_ROWS, LANES), lambda i: (i, 0)), pl.BlockSpec(wr_t.shape, lambda i: (0, 0)),
                  pl.BlockSpec(tri.shape, lambda i: (0, 0))],
        out_specs=[meta, meta, pl.BlockSpec((N_EXPERTS, LANES), lambda i: (0, 0))],
        out_shape=[jax.ShapeDtypeStruct((nt, 8, trc), I32), jax.ShapeDtypeStruct((nt, 8, trc), F32),
                   jax.ShapeDtypeStruct((N_EXPERTS, LANES), I32)],
        scratch_shapes=[pltpu.VMEM((N_EXPERTS, LANES), F32)],
        compiler_params=_cparams(("arbitrary",)),
        name="moe_router",
    )(hf_tiles, wr_t, tri)


DMA_UNROLL = 8


def _token_copy(src_ref, src_tok, dst_ref, dst_tok, sem):
    src = src_ref.at[pl.ds(pl.multiple_of(src_tok * TOK_ROWS, TOK_ROWS), TOK_ROWS), :]
    dst = dst_ref.at[pl.ds(pl.multiple_of(dst_tok * TOK_ROWS, TOK_ROWS), TOK_ROWS), :]
    return pltpu.make_async_copy(src, dst, sem)


def _for_tokens(n, fn):
    def block(i, carry):
        for u in range(DMA_UNROLL):
            fn(i * DMA_UNROLL + u)
        return carry

    lax.fori_loop(0, n // DMA_UNROLL, block, 0)


def _dispatch_kernel(pad_lo_ref, pad_n_ref, nact_ref, slot_ref, hf_ref, xs_ref, zero_ref, sem, zsem, *, trc, ntile):
    @pl.when(pl.program_id(0) == 0)
    def _():
        zero_ref[...] = jnp.zeros_like(zero_ref)
        tile_rows = EXPERT_TILE * TOK_ROWS

        def tile_copy(i):
            return pltpu.make_async_copy(
                zero_ref, xs_ref.at[pl.ds(pl.multiple_of(i * tile_rows, tile_rows), tile_rows), :], zsem)

        def fill_tile(i, carry):
            tile_copy(i).start()
            return carry

        def drain_tile(i, carry):
            tile_copy(i).wait()
            return carry

        lax.fori_loop(nact_ref[0], ntile, fill_tile, 0)
        lax.fori_loop(nact_ref[0], ntile, drain_tile, 0)
        for e in range(N_EXPERTS):
            lo = pad_lo_ref[e]
            n = pad_n_ref[e]

            def fill(r, carry):
                _token_copy(zero_ref, 0, xs_ref, lo + r, zsem).start()
                return carry

            def drain(r, carry):
                _token_copy(zero_ref, 0, xs_ref, lo + r, zsem).wait()
                return carry

            lax.fori_loop(0, n, fill, 0)
            lax.fori_loop(0, n, drain, 0)

    copies = lambda r: [_token_copy(hf_ref, r, xs_ref, slot_ref[0, k, r], sem) for k in range(2)]
    _for_tokens(trc, lambda r: [cp.start() for cp in copies(r)])
    _for_tokens(trc, lambda r: [cp.wait() for cp in copies(r)])


def _dispatch(pad_lo, pad_n, nact, slots, hf_tiles, nslot, *, trc):
    t = hf_tiles.shape[0] // TOK_ROWS
    nt = t // trc
    grid_spec = pltpu.PrefetchScalarGridSpec(
        num_scalar_prefetch=3,
        grid=(nt,),
        in_specs=[pl.BlockSpec((1, 2, trc), lambda i, lo, n, na: (i, 0, 0), memory_space=pltpu.SMEM),
                  pl.BlockSpec((trc * TOK_ROWS, LANES), lambda i, lo, n, na: (i, 0))],
        out_specs=pl.BlockSpec(memory_space=pl.ANY),
        scratch_shapes=[pltpu.VMEM((EXPERT_TILE * TOK_ROWS, LANES), F32), pltpu.SemaphoreType.DMA(()),
                        pltpu.SemaphoreType.DMA(())],
    )
    return pl.pallas_call(
        functools.partial(_dispatch_kernel, trc=trc, ntile=nslot // EXPERT_TILE),
        grid_spec=grid_spec,
        out_shape=jax.ShapeDtypeStruct((nslot * TOK_ROWS, LANES), F32),
        compiler_params=_cparams(("arbitrary",)),
        name="moe_dispatch",
    )(pad_lo, pad_n, nact, slots, hf_tiles)


def _expert_kernel(texp_ref, nact_ref, xs_ref, wg_ref, wu_ref, wd_ref, ys_ref, act_ref):
    active = pl.program_id(0) < nact_ref[0]

    @pl.when(active)
    def _():
        h = _load_tokens(xs_ref, (), EXPERT_TILE).astype(BF16)
        _store_tokens(ys_ref, (), _swiglu_rows(h, wg_ref, wu_ref, wd_ref, act_ref))

    @pl.when(jnp.logical_not(active))
    def _():
        ys_ref[...] = jnp.zeros_like(ys_ref)


def _expert_ffn(tile_expert, nact, xs, wg, wu, wd):
    ntile = xs.shape[0] // (EXPERT_TILE * TOK_ROWS)
    wspec = lambda shp: pl.BlockSpec((1,) + shp, lambda i, te, na: (te[i], 0, 0))
    tile = pl.BlockSpec((EXPERT_TILE * TOK_ROWS, LANES), lambda i, te, na: (i, 0))
    grid_spec = pltpu.PrefetchScalarGridSpec(
        num_scalar_prefetch=2,
        grid=(ntile,),
        in_specs=[tile, wspec((D_MODEL, D_FF)), wspec((D_MODEL, D_FF)), wspec((D_FF, D_MODEL))],
        out_specs=tile,
        scratch_shapes=[pltpu.VMEM((EXPERT_TILE, D_FF), BF16)],
    )
    return pl.pallas_call(
        _expert_kernel,
        grid_spec=grid_spec,
        out_shape=jax.ShapeDtypeStruct(xs.shape, F32),
        compiler_params=_cparams(("arbitrary",)),
        name="moe_experts",
    )(tile_expert, nact, xs, wg, wu, wd)


def _combine_kernel(slot_ref, x_ref, gate_ref, mod_ref, ys_ref, xo_ref, buf_ref, sem, *, trc, nt, cl):
    copies = lambda r: [_token_copy(ys_ref, slot_ref[0, k, r], buf_ref.at[k], r, sem) for k in range(2)]
    _for_tokens(trc, lambda r: [cp.start() for cp in copies(r)])
    _for_tokens(trc, lambda r: [cp.wait() for cp in copies(r)])
    (gate2,) = _mod_rows(mod_ref, (pl.program_id(0) % nt) * trc, trc, cl, (5,))
    gt = gate_ref[0]
    f = gt[:, 0:1] * _load_tokens(buf_ref, (0,), trc) + gt[:, 1:2] * _load_tokens(buf_ref, (1,), trc)
    xo_ref[...] = x_ref[...] + gate2 * f


def _combine(slots, x_flat, gates_t, mod, ys, *, trc, nt, cl):
    t = x_flat.shape[0]
    ntile = t // trc
    tok = pl.BlockSpec((trc, D_MODEL), lambda i: (i, 0))
    return pl.pallas_call(
        functools.partial(_combine_kernel, trc=trc, nt=nt, cl=cl),
        grid=(ntile,),
        in_specs=[pl.BlockSpec((1, 2, trc), lambda i: (i, 0, 0), memory_space=pltpu.SMEM), tok,
                  pl.BlockSpec((1, trc, 2), lambda i: (i, 0, 0)),
                  pl.BlockSpec((1, 16, D_MODEL), lambda i: (i // nt, 0, 0)), pl.BlockSpec(memory_space=pl.ANY)],
        out_specs=tok,
        out_shape=jax.ShapeDtypeStruct((t, D_MODEL), F32),
        scratch_shapes=[pltpu.VMEM((2, trc * TOK_ROWS, LANES), F32), pltpu.SemaphoreType.DMA(())],
        compiler_params=_cparams(("arbitrary",)),
        name="moe_combine",
    )(slots, x_flat, gates_t, mod, ys)


def _moe_ffn(xs, hf, mod, wr_t, tri, wg, wu, wd, *, cl, nt):
    b, s, _ = xs.shape
    t = b * s
    trr = tri.shape[0]
    trc = s // nt
    hf_flat = hf.reshape(t * TOK_ROWS, LANES)
    meta_i, meta_f, counts = _router(hf_flat, wr_t, tri, trc=trr)
    counts = counts[:, 0]
    padded = (counts + EXPERT_TILE - 1) // EXPERT_TILE * EXPERT_TILE
    ends = jnp.cumsum(padded)
    starts = ends - padded
    eid = meta_i[:, 0:2, :]
    group_start = sum(jnp.where(eid == e, starts[e], 0) for e in range(N_EXPERTS))
    slots = group_start + meta_i[:, 2:4, :]
    nslot = 2 * t + N_EXPERTS * EXPERT_TILE
    ntile = nslot // EXPERT_TILE
    nact = (ends[-1] // EXPERT_TILE).astype(I32)
    tile_lo = jnp.minimum(jnp.arange(ntile, dtype=I32), nact - 1) * EXPERT_TILE
    tile_expert = jnp.minimum(jnp.sum(tile_lo[:, None] >= ends[None, :], axis=1), N_EXPERTS - 1).astype(I32)
    nact = nact.reshape(1)
    xs_sorted = _dispatch((starts + counts).astype(I32), (padded - counts).astype(I32), nact, slots, hf_flat, nslot,
                          trc=trr)
    ys = _expert_ffn(tile_expert, nact, xs_sorted, wg, wu, wd)
    retile = lambda a: jnp.swapaxes(jnp.swapaxes(a, 0, 1).reshape(2, t // trc, trc), 0, 1)
    gates_t = jnp.swapaxes(retile(meta_f[:, 0:2, :]), 1, 2)
    out = _combine(retile(slots), xs.reshape(t, D_MODEL), gates_t, mod, ys, trc=trc, nt=nt, cl=cl)
    return out.reshape(b, s, D_MODEL)


def _slab_perm(t, lead):
    nh = t.shape[-1] // HEAD_DIM
    per = nh // 2
    t = t.reshape(lead + (2, per, 2, 2, 16))
    t = jnp.moveaxis(t, (-5, -4, -3, -2, -1), (-3, -5, -2, -4, -1))
    return t.reshape(lead + (nh * HEAD_DIM,))


def _rope_tables(s, cl):
    l = s - cl
    pos = np.arange(l)
    lane = np.arange(LANES)
    i = lane % 32
    freq = ROPE_BASE ** (-(i % 16).astype(np.float32) / 16.0)
    p = np.where(i[None, :] < 16, (pos // GRID_W)[:, None], (pos % GRID_W)[:, None]).astype(np.float32)
    ang = p * freq[None, :].astype(np.float32)
    sign = np.where(lane < LANES // 2, -1.0, 1.0).astype(np.float32)
    cos = np.concatenate([np.ones((cl, LANES), np.float32), np.cos(ang)], axis=0)
    sin = np.concatenate([np.zeros((cl, LANES), np.float32), np.sin(ang) * sign[None, :]], axis=0)
    return jnp.asarray(cos, F32), jnp.asarray(sin, F32)


def kernel(x, c, ctx, c_ctx, w_mod, b_mod, norm1_g, norm2_g, w_in, w_out, gm_v_g, gm_ws, gm_bs, att_q_g, att_k_g, att_sink, ssm_conv_w, ssm_conv_b, ssm_dt_bias, ssm_a_log, ssm_d, ssm_norm_g, ffn_w_gate, ffn_w_up, ffn_w_down, moe_router, moe_w_gate, moe_w_up, moe_w_down):
    b, l, _ = x.shape
    cl = ctx.shape[1]
    s = cl + l
    depth = w_mod.shape[0]
    nt = 8
    nt_moe = 4
    assert s % (nt * 16) == 0 and cl % CHUNK == 0 and l % CHUNK == 0 and l % GRID_W == 0 and b < 16

    cvec = jnp.concatenate([c, c_ctx[None, :], jnp.zeros((16 - b - 1, D_MODEL), F32)], axis=0)
    mods = _modulation(cvec, w_mod, b_mod)
    lat = jnp.moveaxis(mods[:, :, :b, :], 2, 1)
    con = jnp.broadcast_to(mods[:, None, :, b, :], (depth, b, 6, D_MODEL))
    pad2 = jnp.zeros((depth, b, 2, D_MODEL), F32)
    modtab = jnp.concatenate([lat, pad2, con, pad2], axis=2)

    wi = w_in
    w_in_p = jnp.concatenate([
        wi[:, :, 0:2 * GM_W],
        _slab_perm(wi[:, :, 512:1024], (depth, D_MODEL)),
        _slab_perm(wi[:, :, 1024:1152], (depth, D_MODEL)),
        wi[:, :, 1152:2304],
        jnp.pad(wi[:, :, 2304:2312], ((0, 0), (0, 0), (0, DT_PAD - 2 * SSM_HEADS))),
    ], axis=2).astype(BF16)
    wo_att = w_out[:, GM_W:GM_W + ATT_W, :].reshape(depth, 2, 4, HEAD_DIM, D_MODEL)
    wo_att = jnp.swapaxes(wo_att, 1, 2).reshape(depth, ATT_W, D_MODEL)
    w_out_p = jnp.concatenate([w_out[:, :GM_W], wo_att, w_out[:, GM_W + ATT_W:]], axis=1).astype(BF16)
    gq = _slab_perm(jnp.tile(att_q_g, (1, 2)), (depth,))[:, None, :]
    gk = _slab_perm(jnp.tile(att_k_g, (1, 2)), (depth,))[:, None, :]
    lane = np.arange(LANES)
    seg = jnp.asarray(((lane[:, None] // 32) % 2 == (lane[None, :] // 32) % 2), BF16)
    lane2 = np.arange(GM_W)
    seg64 = jnp.asarray(lane2[:, None] // GM_HD == lane2[None, :] // GM_HD, BF16)
    cos, sin = _rope_tables(s, cl)
    ws_cat = jnp.swapaxes(gm_ws, 1, 2).reshape(depth, CHUNK, GM_HEADS * CHUNK).astype(BF16)
    bs_exp = jnp.repeat(jnp.swapaxes(gm_bs, 1, 2), GM_HD, axis=2)
    sink_heads = att_sink.reshape(depth, 2, 4).swapaxes(1, 2).reshape(depth, ATT_HEADS)
    sink_col = jnp.repeat(sink_heads, CHUNK, axis=1)[:, :, None]
    cw = jnp.pad(ssm_conv_w, ((0, 0), (0, 8 - CONV_K), (0, 0)))
    cb = ssm_conv_b[:, None, :]
    pad_dt = lambda t: jnp.pad(t.reshape(depth, 1, 2 * SSM_HEADS), ((0, 0), (0, 0), (0, DT_PAD - 2 * SSM_HEADS)))
    dtb = pad_dt(ssm_dt_bias)
    alog = pad_dt(ssm_a_log)
    dsk = jnp.repeat(ssm_d, SSM_HD, axis=1)[:, None, :]
    wr_t = jnp.swapaxes(moe_router, 1, 2)
    trr = 1024 if (b * s) % 1024 == 0 else 512
    assert (b * s) % trr == 0
    tidx = np.arange(trr)
    tri = jnp.asarray(tidx[:, None] < tidx[None, :], BF16)
    ffn_g, ffn_u, ffn_d = ffn_w_gate.astype(BF16), ffn_w_up.astype(BF16), ffn_w_down.astype(BF16)
    moe_g, moe_u, moe_d = moe_w_gate.astype(BF16), moe_w_up.astype(BF16), moe_w_down.astype(BF16)

    xs = jnp.concatenate([ctx, x], axis=1)
    for i in range(depth):
        moe = i % 2 == 1
        mod = modtab[i]
        gu, gv, q, k, v, z, xbc, dt = _in_projection(
            xs, mod, norm1_g[i][None, :], w_in_p[i], cos, sin, gq[i], gk[i], gm_v_g[i][None, :], seg, seg64,
            cl=cl, nt=nt)
        gm = _gmlp(gu, gv, ws_cat[i], bs_exp[i])
        att = _attention(q, k, v, sink_col[i], cl=cl)
        y = _ssd(xbc, dt, cw[i], cb[i], dtb[i], alog[i], dsk[i], cl=cl)
        xs, hf = _out_projection(xs, mod, gm, att, y, z, ssm_norm_g[i][None, :], norm2_g[i][None, :], w_out_p[i],
                                 cl=cl, nt=nt, token_tiles=moe)
        j = i // 2
        if moe:
            xs = _moe_ffn(xs, hf, mod, wr_t[j], tri, moe_g[j], moe_u[j], moe_d[j], cl=cl, nt=nt_moe)
        else:
            xs = _dense_ffn(xs, hf, mod, ffn_g[j:j + 1], ffn_u[j:j + 1], ffn_d[j:j + 1], cl=cl, nt=nt)
    return xs[:, cl:, :]
```

```python
import functools
import math

import numpy as np
import jax
import jax.numpy as jnp
from jax import lax
from jax.experimental import pallas as pl
from jax.experimental.pallas import tpu as pltpu

F32 = jnp.float32
BF16 = jnp.bfloat16
I32 = jnp.int32
HIGHEST = lax.Precision.HIGHEST

D_MODEL = 1024
CHUNK = 128
GM_HEADS, GM_HD, GM_W = 4, 64, 256
ATT_HEADS, KV_HEADS, HEAD_DIM = 8, 2, 64
ATT_W, KV_W = 512, 128
GRID_W = 64
ROPE_BASE = 10000.0
SSM_HEADS, SSM_HD, SSM_W = 4, 64, 256
SSM_STATE, BC_W, CONV_K, XBC_W = 128, 256, 5, 768
D_FF = 2816
N_EXPERTS = 8
EPS = 1e-6
LANES = 128
DT_PAD = 128
C_GU, C_GV, C_Q, C_K, C_V, C_Z, C_XBC, C_DT, IN_WP = 0, 256, 512, 1024, 1152, 1280, 1536, 2304, 2432
FF_BLOCK = 256
EXPERT_TILE = 512
VMEM_LIMIT = 56 * 1024 * 1024
LOG2E = math.log2(math.e)
SAFE_SCORE_BOUND = 40.0


def _cparams(sem):
    return pltpu.CompilerParams(dimension_semantics=sem, vmem_limit_bytes=VMEM_LIMIT)


def _silu(x):
    return x * jax.nn.sigmoid(x)


def _mod_kernel(c_ref, w_ref, b_ref, o_ref):
    cs = _silu(c_ref[...])
    o_ref[0, 0] = jnp.dot(cs, w_ref[0], precision=HIGHEST, preferred_element_type=F32) + b_ref[0, 0]


def _modulation(cvec, w_mod, b_mod):
    depth = w_mod.shape[0]
    r = cvec.shape[0]
    return pl.pallas_call(
        _mod_kernel,
        grid=(depth, 6),
        in_specs=[
            pl.BlockSpec((r, D_MODEL), lambda i, n: (0, 0)),
            pl.BlockSpec((1, D_MODEL, D_MODEL), lambda i, n: (i, 0, n)),
            pl.BlockSpec((1, 1, 1, D_MODEL), lambda i, n: (i, n, 0, 0)),
        ],
        out_specs=pl.BlockSpec((1, 1, r, D_MODEL), lambda i, n: (i, n, 0, 0)),
        out_shape=jax.ShapeDtypeStruct((depth, 6, r, D_MODEL), F32),
        compiler_params=_cparams(("arbitrary", "arbitrary")),
        name="modulation",
    )(cvec, w_mod, b_mod.reshape(depth, 6, 1, D_MODEL))


def _mod_rows(mod_ref, tile_row0, rows, cl, lo):
    ridx = tile_row0 + lax.broadcasted_iota(I32, (rows, 1), 0)
    is_ctx = ridx < cl
    m = mod_ref[0]
    return [jnp.where(is_ctx, m[8 + k:9 + k], m[k:k + 1]) for k in lo]


def _inproj_kernel(x_ref, mod_ref, g1_ref, w_ref, cos_ref, sin_ref, gq_ref, gk_ref, gvg_ref, seg_ref, seg64_ref,
                   gu_ref, gv_ref, q_ref, k_ref, v_ref, z_ref, xbc_ref, dt_ref, *, tr, cl):
    x = x_ref[0]
    ms = jnp.mean(x * x, axis=-1, keepdims=True)
    y = x * lax.rsqrt(ms + EPS) * g1_ref[...]
    shift, scale = _mod_rows(mod_ref, pl.program_id(1) * tr, tr, cl, (0, 1))
    h = (y * (1.0 + scale) + shift).astype(BF16)

    def proj(a, b):
        return jnp.dot(h, w_ref[:, a:b], preferred_element_type=F32)

    guv = proj(C_GU, C_Q)
    gu_ref[0] = jax.nn.gelu(guv[:, :GM_W]).astype(BF16)
    gv = jax.nn.gelu(guv[:, GM_W:])
    ssv = jnp.dot((gv * gv).astype(BF16), seg64_ref[...], preferred_element_type=F32)
    gv_ref[0] = (gv * lax.rsqrt(ssv * (1.0 / GM_HD) + EPS) * gvg_ref[...]).astype(BF16)

    cos = cos_ref[...]
    sin = sin_ref[...]

    def norm_rope(t, gain):
        ss = jnp.dot((t * t).astype(BF16), seg_ref[...], preferred_element_type=F32)
        tn = t * lax.rsqrt(ss * (1.0 / HEAD_DIM) + EPS) * gain
        return tn * cos + pltpu.roll(tn, LANES // 2, axis=1) * sin

    qkv = proj(C_Q, C_Z)
    gq = gq_ref[...] * (HEAD_DIM ** -0.5 * LOG2E)
    for c in range(ATT_W // LANES):
        q_ref[0, :, c * LANES:(c + 1) * LANES] = norm_rope(qkv[:, c * LANES:(c + 1) * LANES], gq).astype(BF16)
    k_ref[0] = norm_rope(qkv[:, ATT_W:ATT_W + KV_W], gk_ref[...]).astype(BF16)
    v_ref[0] = qkv[:, ATT_W + KV_W:].astype(BF16)
    z_ref[0] = proj(C_Z, C_XBC).astype(BF16)
    xbc_ref[0] = proj(C_XBC, C_DT).astype(BF16)
    dt_ref[0] = proj(C_DT, IN_WP)


def _in_projection(xs, mod, g1, w_in, cos, sin, gq, gk, gvg, seg, seg64, *, cl, nt):
    b, s, _ = xs.shape
    tr = s // nt
    tok = lambda w: pl.BlockSpec((1, tr, w), lambda bi, j: (bi, j, 0))
    full = lambda a: pl.BlockSpec(a.shape, lambda bi, j: (0,) * a.ndim)
    widths = (GM_W, GM_W, ATT_W, KV_W, KV_W, SSM_W, XBC_W)
    return pl.pallas_call(
        functools.partial(_inproj_kernel, tr=tr, cl=cl),
        grid=(b, nt),
        in_specs=[tok(D_MODEL), pl.BlockSpec((1, 16, D_MODEL), lambda bi, j: (bi, 0, 0)), full(g1), full(w_in),
                  pl.BlockSpec((tr, LANES), lambda bi, j: (j, 0)), pl.BlockSpec((tr, LANES), lambda bi, j: (j, 0)),
                  full(gq), full(gk), full(gvg), full(seg), full(seg64)],
        out_specs=[tok(w) for w in widths] + [tok(DT_PAD)],
        out_shape=[jax.ShapeDtypeStruct((b, s, w), BF16) for w in widths]
        + [jax.ShapeDtypeStruct((b, s, DT_PAD), F32)],
        compiler_params=_cparams(("parallel", "parallel")),
        name="in_projection",
    )(xs, mod, g1, w_in, cos, sin, gq, gk, gvg, seg, seg64)


def _gmlp_kernel(gu_ref, gv_ref, ws_ref, bs_ref, o_ref, *, nch):
    head = lax.broadcasted_iota(I32, (CHUNK, GM_W), 1) >> 6

    def body(c, carry):
        r0 = pl.multiple_of(c * CHUNK, CHUNK)
        v = gv_ref[0, pl.ds(r0, CHUNK), :]
        vbd = jnp.concatenate([jnp.where(head == hh, v, jnp.zeros_like(v)) for hh in range(GM_HEADS)], axis=0)
        sp = jnp.dot(ws_ref[...], vbd, preferred_element_type=F32) + bs_ref[...]
        o_ref[0, pl.ds(r0, CHUNK), :] = (gu_ref[0, pl.ds(r0, CHUNK), :].astype(F32) * sp).astype(BF16)
        return carry

    lax.fori_loop(0, nch, body, 0)


def _gmlp(gu, gv, ws_cat, bs_exp):
    b, s, _ = gu.shape
    seq = pl.BlockSpec((1, s, GM_W), lambda bi: (bi, 0, 0))
    return pl.pallas_call(
        functools.partial(_gmlp_kernel, nch=s // CHUNK),
        grid=(b,),
        in_specs=[seq, seq, pl.BlockSpec(ws_cat.shape, lambda bi: (0, 0)), pl.BlockSpec(bs_exp.shape, lambda bi: (0, 0))],
        out_specs=seq,
        out_shape=jax.ShapeDtypeStruct((b, s, GM_W), BF16),
        compiler_params=_cparams(("parallel",)),
        name="gmlp",
    )(gu, gv, ws_cat, bs_exp)


def _attn_kernel(q_ref, k_ref, v_ref, sink_ref, shift_ref, o_ref, *, nch, cl, fixed_shift):
    ncb = cl // CHUNK
    nslab = ATT_W // LANES
    lane = lax.broadcasted_iota(I32, (CHUNK, LANES), 1)
    first_head = ((lane >> 5) & 1) == 0
    low_half = lane < LANES // 2
    kc = k_ref[0, 0:cl, :]
    vc = v_ref[0, 0:cl, :]
    qi = lax.broadcasted_iota(I32, (CHUNK, CHUNK), 0)
    kl = lax.broadcasted_iota(I32, (CHUNK, CHUNK), 1)
    ninf = jnp.full((CHUNK, CHUNK), -jnp.inf, F32)
    zeros = jnp.zeros((CHUNK, CHUNK), F32)
    bias_prev = jnp.where(kl >= qi, zeros, ninf)
    bias_next = jnp.where(kl <= qi, zeros, ninf)

    def body(n, carry):
        r0 = pl.multiple_of(n * CHUNK, CHUNK)
        rp = pl.multiple_of(jnp.maximum(n - 1, ncb) * CHUNK, CHUNK)
        rn = pl.multiple_of(jnp.minimum(n + 1, nch - 1) * CHUNK, CHUNK)
        q = q_ref[0, pl.ds(r0, CHUNK), :]
        kk = jnp.concatenate([kc, k_ref[0, pl.ds(rp, CHUNK), :], k_ref[0, pl.ds(r0, CHUNK), :],
                              k_ref[0, pl.ds(rn, CHUNK), :]], axis=0)
        vv = jnp.concatenate([vc, v_ref[0, pl.ds(rp, CHUNK), :], v_ref[0, pl.ds(r0, CHUNK), :],
                              v_ref[0, pl.ds(rn, CHUNK), :]], axis=0)
        is_lat = n >= ncb
        bias = jnp.concatenate([
            jnp.where(jnp.logical_and(is_lat, n - 1 >= ncb), bias_prev, ninf),
            jnp.where(is_lat, zeros, ninf),
            jnp.where(jnp.logical_and(is_lat, n + 1 <= nch - 1), bias_next, ninf)], axis=1)
        zero = jnp.zeros((CHUNK, LANES), BF16)
        blocks = []
        for c in range(nslab):
            qc = q[:, c * LANES:(c + 1) * LANES]
            blocks += [jnp.where(first_head, qc, zero), jnp.where(first_head, zero, qc)]
        qs = jnp.concatenate(blocks, axis=0)
        sc = lax.dot_general(qs, kk, (((1,), (1,)), ((), ())), preferred_element_type=F32)
        s_ctx = sc[:, :cl]
        s_loc = sc[:, cl:] + jnp.concatenate([bias] * (2 * nslab), axis=0)
        sk = sink_ref[...]

        def lane_tiles(fn, a, b):
            tiles = [a[:, t * LANES:(t + 1) * LANES] for t in range(a.shape[1] // LANES)]
            tiles += [b[:, t * LANES:(t + 1) * LANES] for t in range(b.shape[1] // LANES)]
            return functools.reduce(fn, tiles)

        if fixed_shift:
            m = shift_ref[...]
        else:
            m = jnp.maximum(jnp.max(lane_tiles(jnp.maximum, s_ctx, s_loc), axis=-1, keepdims=True), sk)
        e_ctx = jnp.exp2(s_ctx - m)
        e_loc = jnp.exp2(s_loc - m)
        den = jnp.sum(lane_tiles(jnp.add, e_ctx, e_loc), axis=-1, keepdims=True) + jnp.exp2(sk - m)
        p = jnp.concatenate([e_ctx.astype(BF16), e_loc.astype(BF16)], axis=1)
        o = jnp.dot(p, vv, preferred_element_type=F32) * (1.0 / den)
        for c in range(nslab):
            oa = o[(2 * c) * CHUNK:(2 * c + 1) * CHUNK]
            ob = o[(2 * c + 1) * CHUNK:(2 * c + 2) * CHUNK]
            o_ref[0, pl.ds(r0, CHUNK), c * LANES:(c + 1) * LANES] = jnp.where(low_half, oa, ob).astype(BF16)
        return carry

    lax.fori_loop(0, nch, body, 0)


def _attention(q, k, v, sink_col, shift_col, *, cl, fixed_shift):
    b, s, _ = q.shape
    seq = lambda w: pl.BlockSpec((1, s, w), lambda bi: (bi, 0, 0))
    col = pl.BlockSpec(sink_col.shape, lambda bi: (0, 0))
    return pl.pallas_call(
        functools.partial(_attn_kernel, nch=s // CHUNK, cl=cl, fixed_shift=fixed_shift),
        grid=(b,),
        in_specs=[seq(ATT_W), seq(KV_W), seq(KV_W), col, col],
        out_specs=seq(ATT_W),
        out_shape=jax.ShapeDtypeStruct((b, s, ATT_W), BF16),
        compiler_params=_cparams(("parallel",)),
        name="attention_fixed_shift" if fixed_shift else "attention_row_max",
    )(q, k, v, sink_col, shift_col)


def _ssd_kernel(xbc_ref, dt_ref, cw_ref, cb_ref, dtb_ref, alog_ref, dsk_ref, y_ref,
                xc_ref, dtv_ref, ac_ref, act_ref, yb_ref, st_ref, *, nch, cl):
    ncb = cl // CHUNK
    s = nch * CHUNK
    halo = 16

    def conv_body(c, carry):
        r0 = pl.multiple_of(c * CHUNK, CHUNK)
        has_prev = jnp.logical_and(c != 0, c != ncb)
        has_next = jnp.logical_and(c != ncb - 1, c != nch - 1)
        rp = pl.multiple_of(jnp.maximum(r0 - halo, 0), halo)
        rn = pl.multiple_of(jnp.minimum(r0 + CHUNK, s - halo), halo)
        prev = jnp.where(has_prev, xbc_ref[0, pl.ds(rp, halo), :].astype(F32), 0.0)
        nxt = jnp.where(has_next, xbc_ref[0, pl.ds(rn, halo), :].astype(F32), 0.0)
        cat = jnp.concatenate([prev, xbc_ref[0, pl.ds(r0, CHUNK), :].astype(F32), nxt], axis=0)
        acc = jnp.zeros((CHUNK, XBC_W), F32) + cb_ref[...]
        for kk in range(CONV_K):
            off = halo - CONV_K // 2 + kk
            acc = acc + cw_ref[kk:kk + 1, :] * cat[off:off + CHUNK]
        xc_ref[pl.ds(r0, CHUNK), :] = _silu(acc).astype(BF16)
        dtv = jax.nn.softplus(dt_ref[0, pl.ds(r0, CHUNK), :] + dtb)
        dta = dtv * a_neg
        pre = jnp.dot(tri_f, dta, precision=HIGHEST, preferred_element_type=F32)
        acum = jnp.where(tj < SSM_HEADS, pre, pre[CHUNK - 1:CHUNK, :] - pre + dta)
        dtv_ref[pl.ds(r0, CHUNK), :] = dtv
        ac_ref[pl.ds(r0, CHUNK), :] = acum
        act_ref[pl.ds(r0, CHUNK), :] = acum.T
        return carry

    ti = lax.broadcasted_iota(I32, (CHUNK, CHUNK), 0)
    tj = lax.broadcasted_iota(I32, (CHUNK, CHUNK), 1)
    low_half = tj < LANES // 2
    tri_f = (ti >= tj).astype(F32)
    a_neg = -jnp.exp(alog_ref[...])
    dtb = dtb_ref[...]
    lax.fori_loop(0, nch, conv_body, 0)

    def chunk_update(c, direction, out_ref, out_lead):
        tri = (ti >= tj) if direction == 0 else (tj >= ti)
        last = CHUNK - 1 if direction == 0 else 0
        r0 = pl.multiple_of(c * CHUNK, CHUNK)
        xcv = xc_ref[pl.ds(r0, CHUNK), :]
        dtv = dtv_ref[pl.ds(r0, CHUNK), :]
        acum = ac_ref[pl.ds(r0, CHUNK), :]
        acum_t = act_ref[pl.ds(r0, CHUNK), :]
        outs = []
        for g in range(2):
            xg = xcv[:, g * LANES:(g + 1) * LANES].astype(F32)
            bg = xcv[:, SSM_W + g * LANES:SSM_W + (g + 1) * LANES]
            cg = xcv[:, SSM_W + BC_W + g * LANES:SSM_W + BC_W + (g + 1) * LANES]
            gram = lax.dot_general(cg, bg, (((1,), (1,)), ((), ())), preferred_element_type=F32)
            mixes, dtc, eac, wend, dec = [], [], [], [], []
            for hh in range(2):
                col = direction * SSM_HEADS + 2 * g + hh
                ac = acum[:, col:col + 1]
                al = acum[last:last + 1, col:col + 1]
                seg = jnp.exp(jnp.where(tri, ac - acum_t[col:col + 1, :], -jnp.inf))
                mixes.append((gram * seg).astype(BF16))
                dtc.append(dtv[:, col:col + 1])
                eac.append(jnp.exp(ac))
                wend.append(jnp.exp(al - ac) * dtv[:, col:col + 1])
                dec.append(jnp.exp(al))
            pair = lambda t: jnp.where(low_half, t[0], t[1])
            xdt = (xg * pair(dtc)).astype(BF16)
            y_intra = jnp.where(low_half, jnp.dot(mixes[0], xdt, preferred_element_type=F32),
                                jnp.dot(mixes[1], xdt, preferred_element_type=F32))
            stg = st_ref[direction, :, g * LANES:(g + 1) * LANES]
            y_inter = jnp.dot(cg, stg.astype(BF16), preferred_element_type=F32) * pair(eac)
            xw = (xg * pair(wend)).astype(BF16)
            contrib = lax.dot_general(bg, xw, (((0,), (0,)), ((), ())), preferred_element_type=F32)
            st_ref[direction, :, g * LANES:(g + 1) * LANES] = stg * pair(dec)[0:1, :] + contrib
            yg = y_intra + y_inter
            if direction == 0:
                yg = yg + dsk_ref[:, g * LANES:(g + 1) * LANES] * xg
            outs.append(yg)
        out_ref[out_lead + (pl.ds(r0, CHUNK), slice(None))] = jnp.concatenate(outs, axis=1)

    st_ref[...] = jnp.zeros_like(st_ref)

    def scan_body(step, carry):
        chunk_update(step, 0, y_ref, (0,))
        cb = jnp.where(step < ncb, ncb - 1 - step, nch - 1 - (step - ncb))
        chunk_update(cb, 1, yb_ref, ())
        return carry

    lax.fori_loop(0, nch, scan_body, 0)
    y_ref[0] = y_ref[0] + yb_ref[...]


def _ssd(xbc, dt, cw, cb, dtb, alog, dsk, *, cl):
    b, s, _ = xbc.shape
    full = lambda a: pl.BlockSpec(a.shape, lambda bi: (0,) * a.ndim)
    return pl.pallas_call(
        functools.partial(_ssd_kernel, nch=s // CHUNK, cl=cl),
        grid=(b,),
        in_specs=[pl.BlockSpec((1, s, XBC_W), lambda bi: (bi, 0, 0)), pl.BlockSpec((1, s, DT_PAD), lambda bi: (bi, 0, 0)),
                  full(cw), full(cb), full(dtb), full(alog), full(dsk)],
        out_specs=pl.BlockSpec((1, s, SSM_W), lambda bi: (bi, 0, 0)),
        out_shape=jax.ShapeDtypeStruct((b, s, SSM_W), F32),
        scratch_shapes=[pltpu.VMEM((s, XBC_W), BF16), pltpu.VMEM((s, DT_PAD), F32), pltpu.VMEM((s, DT_PAD), F32),
                        pltpu.VMEM((s, DT_PAD), F32), pltpu.VMEM((s, SSM_W), F32),
                        pltpu.VMEM((2, SSM_STATE, SSM_W), F32)],
        compiler_params=_cparams(("parallel",)),
        name="ssd",
    )(xbc, dt, cw, cb, dtb, alog, dsk)


TOK_ROWS = D_MODEL // LANES


def _load_tokens(ref, lead, n):
    return jnp.concatenate([ref[lead + (pl.ds(j, n, stride=TOK_ROWS), slice(None))] for j in range(TOK_ROWS)], axis=1)


def _store_tokens(ref, lead, val):
    for j in range(TOK_ROWS):
        ref[lead + (pl.ds(j, val.shape[0], stride=TOK_ROWS), slice(None))] = val[:, j * LANES:(j + 1) * LANES]


def _outproj_kernel(x_ref, mod_ref, gm_ref, att_ref, y_ref, z_ref, ng_ref, g2_ref, w_ref, xo_ref, hf_ref, *, tr, cl,
                    token_tiles):
    gated = y_ref[0] * _silu(z_ref[0].astype(F32))
    ssm = gated * lax.rsqrt(jnp.mean(gated * gated, axis=-1, keepdims=True) + EPS) * ng_ref[...]
    mix = jnp.concatenate([gm_ref[0], att_ref[0], ssm.astype(BF16)], axis=1)
    gate1, shift2, scale2 = _mod_rows(mod_ref, pl.program_id(1) * tr, tr, cl, (2, 3, 4))
    x = x_ref[0] + gate1 * jnp.dot(mix, w_ref[...], preferred_element_type=F32)
    xo_ref[0] = x
    ms = jnp.mean(x * x, axis=-1, keepdims=True)
    hf = x * lax.rsqrt(ms + EPS) * g2_ref[...] * (1.0 + scale2) + shift2
    if token_tiles:
        _store_tokens(hf_ref, (0,), hf)
    else:
        hf_ref[0] = hf.astype(hf_ref.dtype)


def _out_projection(xs, mod, gm, att, y, z, ng, g2, w_out, *, cl, nt, token_tiles):
    b, s, _ = xs.shape
    tr = s // nt
    tok = lambda w: pl.BlockSpec((1, tr, w), lambda bi, j: (bi, j, 0))
    full = lambda a: pl.BlockSpec(a.shape, lambda bi, j: (0,) * a.ndim)
    if token_tiles:
        hf_spec = pl.BlockSpec((1, tr * TOK_ROWS, LANES), lambda bi, j: (bi, j, 0))
        hf_shape = jax.ShapeDtypeStruct((b, s * TOK_ROWS, LANES), F32)
    else:
        hf_spec, hf_shape = tok(D_MODEL), jax.ShapeDtypeStruct((b, s, D_MODEL), BF16)
    return pl.pallas_call(
        functools.partial(_outproj_kernel, tr=tr, cl=cl, token_tiles=token_tiles),
        grid=(b, nt),
        in_specs=[tok(D_MODEL), pl.BlockSpec((1, 16, D_MODEL), lambda bi, j: (bi, 0, 0)), tok(GM_W), tok(ATT_W),
                  tok(SSM_W), tok(SSM_W), full(ng), full(g2), full(w_out)],
        out_specs=[tok(D_MODEL), hf_spec],
        out_shape=[jax.ShapeDtypeStruct((b, s, D_MODEL), F32), hf_shape],
        compiler_params=_cparams(("parallel", "parallel")),
        name="out_projection",
    )(xs, mod, gm, att, y, z, ng, g2, w_out)


def _swiglu_rows(h, wg_ref, wu_ref, wd_ref, act_ref):
    for f in range(D_FF // FF_BLOCK):
        cols = slice(f * FF_BLOCK, (f + 1) * FF_BLOCK)
        g = jnp.dot(h, wg_ref[0, :, cols], preferred_element_type=F32)
        u = jnp.dot(h, wu_ref[0, :, cols], preferred_element_type=F32)
        act_ref[:, cols] = (_silu(g) * u).astype(BF16)
    return jnp.dot(act_ref[...], wd_ref[0], preferred_element_type=F32)


def _ffn_kernel(x_ref, hf_ref, mod_ref, wg_ref, wu_ref, wd_ref, xo_ref, act_ref, *, tr, cl):
    (gate2,) = _mod_rows(mod_ref, pl.program_id(1) * tr, tr, cl, (5,))
    xo_ref[0] = x_ref[0] + gate2 * _swiglu_rows(hf_ref[0], wg_ref, wu_ref, wd_ref, act_ref)


def _dense_ffn(xs, hf, mod, wg, wu, wd, *, cl, nt):
    b, s, _ = xs.shape
    tr = s // nt
    tok = pl.BlockSpec((1, tr, D_MODEL), lambda bi, j: (bi, j, 0))
    full = lambda a: pl.BlockSpec(a.shape, lambda bi, j: (0,) * a.ndim)
    return pl.pallas_call(
        functools.partial(_ffn_kernel, tr=tr, cl=cl),
        grid=(b, nt),
        in_specs=[tok, tok, pl.BlockSpec((1, 16, D_MODEL), lambda bi, j: (bi, 0, 0)), full(wg), full(wu), full(wd)],
        out_specs=tok,
        out_shape=jax.ShapeDtypeStruct((b, s, D_MODEL), F32),
        scratch_shapes=[pltpu.VMEM((tr, D_FF), BF16)],
        compiler_params=_cparams(("parallel", "parallel")),
        name="dense_ffn",
    )(xs, hf, mod, wg, wu, wd)


def _router_kernel(hf_ref, wr_ref, tri_ref, mi_ref, mf_ref, cnt_ref, run_ref, *, trc):
    @pl.when(pl.program_id(0) == 0)
    def _():
        run_ref[...] = jnp.zeros_like(run_ref)

    logits = lax.dot_general(wr_ref[...], _load_tokens(hf_ref, (), trc), (((1,), (1,)), ((), ())), precision=HIGHEST,
                             preferred_element_type=F32)
    eidx = lax.broadcasted_iota(I32, (N_EXPERTS, trc), 0).astype(F32)
    m1 = jnp.max(logits, axis=0, keepdims=True)
    i1 = jnp.min(jnp.where(logits == m1, eidx, float(N_EXPERTS)), axis=0, keepdims=True)
    rest = jnp.where(eidx == i1, -jnp.inf, logits)
    m2 = jnp.max(rest, axis=0, keepdims=True)
    i2 = jnp.min(jnp.where(rest == m2, eidx, float(N_EXPERTS)), axis=0, keepdims=True)
    e2 = jnp.exp(m2 - m1)
    g1 = 1.0 / (1.0 + e2)
    g2 = e2 / (1.0 + e2)
    oh1 = (eidx == i1).astype(F32)
    oh2 = (eidx == i2).astype(F32)
    sel = oh1 + oh2
    before = jnp.dot(sel.astype(BF16), tri_ref[...], preferred_element_type=F32) + run_ref[:, 0:1]
    r1 = jnp.sum(oh1 * before, axis=0, keepdims=True)
    r2 = jnp.sum(oh2 * before, axis=0, keepdims=True)
    run_ref[...] = run_ref[...] + jnp.sum(sel, axis=1, keepdims=True)
    zi = jnp.zeros((4, trc), I32)
    mi_ref[0] = jnp.concatenate([i1.astype(I32), i2.astype(I32), r1.astype(I32), r2.astype(I32), zi], axis=0)
    mf_ref[0] = jnp.concatenate([g1, g2, jnp.zeros((6, trc), F32)], axis=0)
    cnt_ref[...] = run_ref[...].astype(I32)


def _router(hf_tiles, wr_t, tri, *, trc):
    t = hf_tiles.shape[0] // TOK_ROWS
    nt = t // trc
    meta = pl.BlockSpec((1, 8, trc), lambda i: (i, 0, 0))
    return pl.pallas_call(
        functools.partial(_router_kernel, trc=trc),
        grid=(nt,),
        in_specs=[pl.BlockSpec((trc * TOK_ROWS, LANES), lambda i: (i, 0)), pl.BlockSpec(wr_t.shape, lambda i: (0, 0)),
                  pl.BlockSpec(tri.shape, lambda i: (0, 0))],
        out_specs=[meta, meta, pl.BlockSpec((N_EXPERTS, LANES), lambda i: (0, 0))],
        out_shape=[jax.ShapeDtypeStruct((nt, 8, trc), I32), jax.ShapeDtypeStruct((nt, 8, trc), F32),
                   jax.ShapeDtypeStruct((N_EXPERTS, LANES), I32)],
        scratch_shapes=[pltpu.VMEM((N_EXPERTS, LANES), F32)],
        compiler_params=_cparams(("arbitrary",)),
        name="moe_router",
    )(hf_tiles, wr_t, tri)


DMA_UNROLL = 8


def _token_copy(src_ref, src_tok, dst_ref, dst_tok, sem):
    src = src_ref.at[pl.ds(pl.multiple_of(src_tok * TOK_ROWS, TOK_ROWS), TOK_ROWS), :]
    dst = dst_ref.at[pl.ds(pl.multiple_of(dst_tok * TOK_ROWS, TOK_ROWS), TOK_ROWS), :]
    return pltpu.make_async_copy(src, dst, sem)


def _for_tokens(n, fn):
    def block(i, carry):
        for u in range(DMA_UNROLL):
            fn(i * DMA_UNROLL + u)
        return carry

    lax.fori_loop(0, n // DMA_UNROLL, block, 0)


def _dispatch_kernel(pad_lo_ref, pad_n_ref, nact_ref, slot_ref, hf_ref, xs_ref, zero_ref, sem, zsem, *, trc, ntile):
    @pl.when(pl.program_id(0) == 0)
    def _():
        zero_ref[...] = jnp.zeros_like(zero_ref)
        tile_rows = EXPERT_TILE * TOK_ROWS

        def tile_copy(i):
            return pltpu.make_async_copy(
                zero_ref, xs_ref.at[pl.ds(pl.multiple_of(i * tile_rows, tile_rows), tile_rows), :], zsem)

        def fill_tile(i, carry):
            tile_copy(i).start()
            return carry

        def drain_tile(i, carry):
            tile_copy(i).wait()
            return carry

        lax.fori_loop(nact_ref[0], ntile, fill_tile, 0)
        lax.fori_loop(nact_ref[0], ntile, drain_tile, 0)
        for e in range(N_EXPERTS):
            lo = pad_lo_ref[e]
            n = pad_n_ref[e]

            def fill(r, carry):
                _token_copy(zero_ref, 0, xs_ref, lo + r, zsem).start()
                return carry

            def drain(r, carry):
                _token_copy(zero_ref, 0, xs_ref, lo + r, zsem).wait()
                return carry

            lax.fori_loop(0, n, fill, 0)
            lax.fori_loop(0, n, drain, 0)

    copies = lambda r: [_token_copy(hf_ref, r, xs_ref, slot_ref[0, k, r], sem) for k in range(2)]
    _for_tokens(trc, lambda r: [cp.start() for cp in copies(r)])
    _for_tokens(trc, lambda r: [cp.wait() for cp in copies(r)])


def _dispatch(pad_lo, pad_n, nact, slots, hf_tiles, nslot, *, trc):
    t = hf_tiles.shape[0] // TOK_ROWS
    nt = t // trc
    grid_spec = pltpu.PrefetchScalarGridSpec(
        num_scalar_prefetch=3,
        grid=(nt,),
        in_specs=[pl.BlockSpec((1, 2, trc), lambda i, lo, n, na: (i, 0, 0), memory_space=pltpu.SMEM),
                  pl.BlockSpec((trc * TOK_ROWS, LANES), lambda i, lo, n, na: (i, 0))],
        out_specs=pl.BlockSpec(memory_space=pl.ANY),
        scratch_shapes=[pltpu.VMEM((EXPERT_TILE * TOK_ROWS, LANES), F32), pltpu.SemaphoreType.DMA(()),
                        pltpu.SemaphoreType.DMA(())],
    )
    return pl.pallas_call(
        functools.partial(_dispatch_kernel, trc=trc, ntile=nslot // EXPERT_TILE),
        grid_spec=grid_spec,
        out_shape=jax.ShapeDtypeStruct((nslot * TOK_ROWS, LANES), F32),
        compiler_params=_cparams(("arbitrary",)),
        name="moe_dispatch",
    )(pad_lo, pad_n, nact, slots, hf_tiles)


def _expert_kernel(texp_ref, nact_ref, xs_ref, wg_ref, wu_ref, wd_ref, ys_ref, act_ref):
    active = pl.program_id(0) < nact_ref[0]

    @pl.when(active)
    def _():
        h = _load_tokens(xs_ref, (), EXPERT_TILE).astype(BF16)
        _store_tokens(ys_ref, (), _swiglu_rows(h, wg_ref, wu_ref, wd_ref, act_ref))

    @pl.when(jnp.logical_not(active))
    def _():
        ys_ref[...] = jnp.zeros_like(ys_ref)


def _expert_ffn(tile_expert, nact, xs, wg, wu, wd):
    ntile = xs.shape[0] // (EXPERT_TILE * TOK_ROWS)
    wspec = lambda shp: pl.BlockSpec((1,) + shp, lambda i, te, na: (te[i], 0, 0))
    tile = pl.BlockSpec((EXPERT_TILE * TOK_ROWS, LANES), lambda i, te, na: (i, 0))
    grid_spec = pltpu.PrefetchScalarGridSpec(
        num_scalar_prefetch=2,
        grid=(ntile,),
        in_specs=[tile, wspec((D_MODEL, D_FF)), wspec((D_MODEL, D_FF)), wspec((D_FF, D_MODEL))],
        out_specs=tile,
        scratch_shapes=[pltpu.VMEM((EXPERT_TILE, D_FF), BF16)],
    )
    return pl.pallas_call(
        _expert_kernel,
        grid_spec=grid_spec,
        out_shape=jax.ShapeDtypeStruct(xs.shape, F32),
        compiler_params=_cparams(("arbitrary",)),
        name="moe_experts",
    )(tile_expert, nact, xs, wg, wu, wd)


def _combine_kernel(slot_ref, x_ref, gate_ref, mod_ref, ys_ref, xo_ref, buf_ref, sem, *, trc, nt, cl):
    copies = lambda r: [_token_copy(ys_ref, slot_ref[0, k, r], buf_ref.at[k], r, sem) for k in range(2)]
    _for_tokens(trc, lambda r: [cp.start() for cp in copies(r)])
    _for_tokens(trc, lambda r: [cp.wait() for cp in copies(r)])
    (gate2,) = _mod_rows(mod_ref, (pl.program_id(0) % nt) * trc, trc, cl, (5,))
    gt = gate_ref[0]
    f = gt[:, 0:1] * _load_tokens(buf_ref, (0,), trc) + gt[:, 1:2] * _load_tokens(buf_ref, (1,), trc)
    xo_ref[...] = x_ref[...] + gate2 * f


def _combine(slots, x_flat, gates_t, mod, ys, *, trc, nt, cl):
    t = x_flat.shape[0]
    ntile = t // trc
    tok = pl.BlockSpec((trc, D_MODEL), lambda i: (i, 0))
    return pl.pallas_call(
        functools.partial(_combine_kernel, trc=trc, nt=nt, cl=cl),
        grid=(ntile,),
        in_specs=[pl.BlockSpec((1, 2, trc), lambda i: (i, 0, 0), memory_space=pltpu.SMEM), tok,
                  pl.BlockSpec((1, trc, 2), lambda i: (i, 0, 0)),
                  pl.BlockSpec((1, 16, D_MODEL), lambda i: (i // nt, 0, 0)), pl.BlockSpec(memory_space=pl.ANY)],
        out_specs=tok,
        out_shape=jax.ShapeDtypeStruct((t, D_MODEL), F32),
        scratch_shapes=[pltpu.VMEM((2, trc * TOK_ROWS, LANES), F32), pltpu.SemaphoreType.DMA(())],
        compiler_params=_cparams(("arbitrary",)),
        name="moe_combine",
    )(slots, x_flat, gates_t, mod, ys)


def _moe_ffn(xs, hf, mod, wr_t, tri, wg, wu, wd, *, cl, nt):
    b, s, _ = xs.shape
    t = b * s
    trr = tri.shape[0]
    trc = s // nt
    hf_flat = hf.reshape(t * TOK_ROWS, LANES)
    meta_i, meta_f, counts = _router(hf_flat, wr_t, tri, trc=trr)
    counts = counts[:, 0]
    padded = (counts + EXPERT_TILE - 1) // EXPERT_TILE * EXPERT_TILE
    ends = jnp.cumsum(padded)
    starts = ends - padded
    eid = meta_i[:, 0:2, :]
    group_start = sum(jnp.where(eid == e, starts[e], 0) for e in range(N_EXPERTS))
    slots = group_start + meta_i[:, 2:4, :]
    nslot = 2 * t + N_EXPERTS * EXPERT_TILE
    ntile = nslot // EXPERT_TILE
    nact = (ends[-1] // EXPERT_TILE).astype(I32)
    tile_lo = jnp.minimum(jnp.arange(ntile, dtype=I32), nact - 1) * EXPERT_TILE
    tile_expert = jnp.minimum(jnp.sum(tile_lo[:, None] >= ends[None, :], axis=1), N_EXPERTS - 1).astype(I32)
    nact = nact.reshape(1)
    xs_sorted = _dispatch((starts + counts).astype(I32), (padded - counts).astype(I32), nact, slots, hf_flat, nslot,
                          trc=trr)
    ys = _expert_ffn(tile_expert, nact, xs_sorted, wg, wu, wd)
    retile = lambda a: jnp.swapaxes(jnp.swapaxes(a, 0, 1).reshape(2, t // trc, trc), 0, 1)
    gates_t = jnp.swapaxes(retile(meta_f[:, 0:2, :]), 1, 2)
    out = _combine(retile(slots), xs.reshape(t, D_MODEL), gates_t, mod, ys, trc=trc, nt=nt, cl=cl)
    return out.reshape(b, s, D_MODEL)


def _slab_perm(t, lead):
    nh = t.shape[-1] // HEAD_DIM
    per = nh // 2
    t = t.reshape(lead + (2, per, 2, 2, 16))
    t = jnp.moveaxis(t, (-5, -4, -3, -2, -1), (-3, -5, -2, -4, -1))
    return t.reshape(lead + (nh * HEAD_DIM,))


def _rope_tables(s, cl):
    l = s - cl
    pos = np.arange(l)
    lane = np.arange(LANES)
    i = lane % 32
    freq = ROPE_BASE ** (-(i % 16).astype(np.float32) / 16.0)
    p = np.where(i[None, :] < 16, (pos // GRID_W)[:, None], (pos % GRID_W)[:, None]).astype(np.float32)
    ang = p * freq[None, :].astype(np.float32)
    sign = np.where(lane < LANES // 2, -1.0, 1.0).astype(np.float32)
    cos = np.concatenate([np.ones((cl, LANES), np.float32), np.cos(ang)], axis=0)
    sin = np.concatenate([np.zeros((cl, LANES), np.float32), np.sin(ang) * sign[None, :]], axis=0)
    return jnp.asarray(cos, F32), jnp.asarray(sin, F32)


def kernel(x, c, ctx, c_ctx, w_mod, b_mod, norm1_g, norm2_g, w_in, w_out, gm_v_g, gm_ws, gm_bs, att_q_g, att_k_g, att_sink, ssm_conv_w, ssm_conv_b, ssm_dt_bias, ssm_a_log, ssm_d, ssm_norm_g, ffn_w_gate, ffn_w_up, ffn_w_down, moe_router, moe_w_gate, moe_w_up, moe_w_down):
    b, l, _ = x.shape
    cl = ctx.shape[1]
    s = cl + l
    depth = w_mod.shape[0]
    nt = 8
    nt_moe = 4
    assert s % (nt * 16) == 0 and cl % CHUNK == 0 and l % CHUNK == 0 and l % GRID_W == 0 and b < 16

    cvec = jnp.concatenate([c, c_ctx[None, :], jnp.zeros((16 - b - 1, D_MODEL), F32)], axis=0)
    mods = _modulation(cvec, w_mod, b_mod)
    lat = jnp.moveaxis(mods[:, :, :b, :], 2, 1)
    con = jnp.broadcast_to(mods[:, None, :, b, :], (depth, b, 6, D_MODEL))
    pad2 = jnp.zeros((depth, b, 2, D_MODEL), F32)
    modtab = jnp.concatenate([lat, pad2, con, pad2], axis=2)

    wi = w_in
    w_in_p = jnp.concatenate([
        wi[:, :, 0:2 * GM_W],
        _slab_perm(wi[:, :, 512:1024], (depth, D_MODEL)),
        _slab_perm(wi[:, :, 1024:1152], (depth, D_MODEL)),
        wi[:, :, 1152:2304],
        jnp.pad(wi[:, :, 2304:2312], ((0, 0), (0, 0), (0, DT_PAD - 2 * SSM_HEADS))),
    ], axis=2).astype(BF16)
    wo_att = w_out[:, GM_W:GM_W + ATT_W, :].reshape(depth, 2, 4, HEAD_DIM, D_MODEL)
    wo_att = jnp.swapaxes(wo_att, 1, 2).reshape(depth, ATT_W, D_MODEL)
    w_out_p = jnp.concatenate([w_out[:, :GM_W], wo_att, w_out[:, GM_W + ATT_W:]], axis=1).astype(BF16)
    gq = _slab_perm(jnp.tile(att_q_g, (1, 2)), (depth,))[:, None, :]
    gk = _slab_perm(jnp.tile(att_k_g, (1, 2)), (depth,))[:, None, :]
    lane = np.arange(LANES)
    seg = jnp.asarray(((lane[:, None] // 32) % 2 == (lane[None, :] // 32) % 2), BF16)
    lane2 = np.arange(GM_W)
    seg64 = jnp.asarray(lane2[:, None] // GM_HD == lane2[None, :] // GM_HD, BF16)
    cos, sin = _rope_tables(s, cl)
    ws_cat = jnp.swapaxes(gm_ws, 1, 2).reshape(depth, CHUNK, GM_HEADS * CHUNK).astype(BF16)
    bs_exp = jnp.repeat(jnp.swapaxes(gm_bs, 1, 2), GM_HD, axis=2)
    sink_heads = att_sink.reshape(depth, 2, 4).swapaxes(1, 2).reshape(depth, ATT_HEADS)
    sink_col = jnp.repeat(sink_heads, CHUNK, axis=1)[:, :, None] * LOG2E
    score_bound = (8.0 * (1.0 + 2.0 ** -6)) * jnp.max(jnp.abs(att_q_g), axis=1) * jnp.max(jnp.abs(att_k_g), axis=1)
    shift_col = jnp.maximum(sink_col, score_bound[:, None, None] * LOG2E)
    cw = jnp.pad(ssm_conv_w, ((0, 0), (0, 8 - CONV_K), (0, 0)))
    cb = ssm_conv_b[:, None, :]
    pad_dt = lambda t: jnp.pad(t.reshape(depth, 1, 2 * SSM_HEADS), ((0, 0), (0, 0), (0, DT_PAD - 2 * SSM_HEADS)))
    dtb = pad_dt(ssm_dt_bias)
    alog = pad_dt(ssm_a_log)
    dsk = jnp.repeat(ssm_d, SSM_HD, axis=1)[:, None, :]
    wr_t = jnp.swapaxes(moe_router, 1, 2)
    trr = 1024 if (b * s) % 1024 == 0 else 512
    assert (b * s) % trr == 0
    tidx = np.arange(trr)
    tri = jnp.asarray(tidx[:, None] < tidx[None, :], BF16)
    ffn_g, ffn_u, ffn_d = ffn_w_gate.astype(BF16), ffn_w_up.astype(BF16), ffn_w_down.astype(BF16)
    moe_g, moe_u, moe_d = moe_w_gate.astype(BF16), moe_w_up.astype(BF16), moe_w_down.astype(BF16)

    xs = jnp.concatenate([ctx, x], axis=1)
    for i in range(depth):
        moe = i % 2 == 1
        mod = modtab[i]
        gu, gv, q, k, v, z, xbc, dt = _in_projection(
            xs, mod, norm1_g[i][None, :], w_in_p[i], cos, sin, gq[i], gk[i], gm_v_g[i][None, :], seg, seg64,
            cl=cl, nt=nt)
        gm = _gmlp(gu, gv, ws_cat[i], bs_exp[i])
        att = lax.cond(score_bound[i] <= SAFE_SCORE_BOUND,
                       functools.partial(_attention, cl=cl, fixed_shift=True),
                       functools.partial(_attention, cl=cl, fixed_shift=False),
                       q, k, v, sink_col[i], shift_col[i])
        y = _ssd(xbc, dt, cw[i], cb[i], dtb[i], alog[i], dsk[i], cl=cl)
        xs, hf = _out_projection(xs, mod, gm, att, y, z, ssm_norm_g[i][None, :], norm2_g[i][None, :], w_out_p[i],
                                 cl=cl, nt=nt, token_tiles=moe)
        j = i // 2
        if moe:
            xs = _moe_ffn(xs, hf, mod, wr_t[j], tri, moe_g[j], moe_u[j], moe_d[j], cl=cl, nt=nt_moe)
        else:
            xs = _dense_ffn(xs, hf, mod, ffn_g[j:j + 1], ffn_u[j:j + 1], ffn_d[j:j + 1], cl=cl, nt=nt)
    return xs[:, cl:, :]
```

```python
import functools
import math

import numpy as np
import jax
import jax.numpy as jnp
from jax import lax
from jax.experimental import pallas as pl
from jax.experimental.pallas import tpu as pltpu

F32 = jnp.float32
BF16 = jnp.bfloat16
I32 = jnp.int32
HIGHEST = lax.Precision.HIGHEST

D_MODEL = 1024
CHUNK = 128
GM_HEADS, GM_HD, GM_W = 4, 64, 256
ATT_HEADS, KV_HEADS, HEAD_DIM = 8, 2, 64
ATT_W, KV_W = 512, 128
GRID_W = 64
ROPE_BASE = 10000.0
SSM_HEADS, SSM_HD, SSM_W = 4, 64, 256
SSM_STATE, BC_W, CONV_K, XBC_W = 128, 256, 5, 768
D_FF = 2816
N_EXPERTS = 8
EPS = 1e-6
LANES = 128
DT_PAD = 128
C_GU, C_GV, C_Q, C_K, C_V, C_Z, C_XBC, C_DT, IN_WP = 0, 256, 512, 1024, 1152, 1280, 1536, 2304, 2432
FF_BLOCK = 256
EXPERT_TILE = 512
VMEM_LIMIT = 56 * 1024 * 1024
LOG2E = math.log2(math.e)
SAFE_SCORE_BOUND = 40.0


def _cparams(sem):
    return pltpu.CompilerParams(dimension_semantics=sem, vmem_limit_bytes=VMEM_LIMIT)


def _silu(x):
    return x * jax.nn.sigmoid(x)


def _mod_kernel(c_ref, w_ref, b_ref, o_ref):
    cs = _silu(c_ref[...])
    o_ref[0, 0] = jnp.dot(cs, w_ref[0], precision=HIGHEST, preferred_element_type=F32) + b_ref[0, 0]


def _modulation(cvec, w_mod, b_mod):
    depth = w_mod.shape[0]
    r = cvec.shape[0]
    return pl.pallas_call(
        _mod_kernel,
        grid=(depth, 6),
        in_specs=[
            pl.BlockSpec((r, D_MODEL), lambda i, n: (0, 0)),
            pl.BlockSpec((1, D_MODEL, D_MODEL), lambda i, n: (i, 0, n)),
            pl.BlockSpec((1, 1, 1, D_MODEL), lambda i, n: (i, n, 0, 0)),
        ],
        out_specs=pl.BlockSpec((1, 1, r, D_MODEL), lambda i, n: (i, n, 0, 0)),
        out_shape=jax.ShapeDtypeStruct((depth, 6, r, D_MODEL), F32),
        compiler_params=_cparams(("arbitrary", "arbitrary")),
        name="modulation",
    )(cvec, w_mod, b_mod.reshape(depth, 6, 1, D_MODEL))


def _mod_rows(mod_ref, tile_row0, rows, cl, lo):
    ridx = tile_row0 + lax.broadcasted_iota(I32, (rows, 1), 0)
    is_ctx = ridx < cl
    m = mod_ref[0]
    return [jnp.where(is_ctx, m[8 + k:9 + k], m[k:k + 1]) for k in lo]


def _inproj_kernel(x_ref, mod_ref, g1_ref, w_ref, cos_ref, sin_ref, gq_ref, gk_ref, gvg_ref, seg_ref, seg64_ref,
                   gu_ref, gv_ref, q_ref, k_ref, v_ref, z_ref, xbc_ref, dt_ref, *, tr, cl):
    x = x_ref[0]
    ms = jnp.mean(x * x, axis=-1, keepdims=True)
    y = x * lax.rsqrt(ms + EPS) * g1_ref[...]
    shift, scale = _mod_rows(mod_ref, pl.program_id(1) * tr, tr, cl, (0, 1))
    h = (y * (1.0 + scale) + shift).astype(BF16)

    def proj(a, b):
        return jnp.dot(h, w_ref[:, a:b], preferred_element_type=F32)

    guv = proj(C_GU, C_Q)
    gu_ref[0] = jax.nn.gelu(guv[:, :GM_W]).astype(BF16)
    gv = jax.nn.gelu(guv[:, GM_W:])
    ssv = jnp.dot((gv * gv).astype(BF16), seg64_ref[...], preferred_element_type=F32)
    gv_ref[0] = (gv * lax.rsqrt(ssv * (1.0 / GM_HD) + EPS) * gvg_ref[...]).astype(BF16)

    cos = cos_ref[...]
    sin = sin_ref[...]

    def norm_rope(t, gain):
        ss = jnp.dot((t * t).astype(BF16), seg_ref[...], preferred_element_type=F32)
        tn = t * lax.rsqrt(ss * (1.0 / HEAD_DIM) + EPS) * gain
        return tn * cos + pltpu.roll(tn, LANES // 2, axis=1) * sin

    qkv = proj(C_Q, C_Z)
    gq = gq_ref[...] * (HEAD_DIM ** -0.5 * LOG2E)
    for c in range(ATT_W // LANES):
        q_ref[0, :, c * LANES:(c + 1) * LANES] = norm_rope(qkv[:, c * LANES:(c + 1) * LANES], gq).astype(BF16)
    k_ref[0] = norm_rope(qkv[:, ATT_W:ATT_W + KV_W], gk_ref[...]).astype(BF16)
    v_ref[0] = qkv[:, ATT_W + KV_W:].astype(BF16)
    z_ref[0] = proj(C_Z, C_XBC).astype(BF16)
    xbc_ref[0] = proj(C_XBC, C_DT).astype(BF16)
    dt_ref[0] = proj(C_DT, IN_WP)


def _in_projection(xs, mod, g1, w_in, cos, sin, gq, gk, gvg, seg, seg64, *, cl, nt):
    b, s, _ = xs.shape
    tr = s // nt
    tok = lambda w: pl.BlockSpec((1, tr, w), lambda bi, j: (bi, j, 0))
    full = lambda a: pl.BlockSpec(a.shape, lambda bi, j: (0,) * a.ndim)
    widths = (GM_W, GM_W, ATT_W, KV_W, KV_W, SSM_W, XBC_W)
    return pl.pallas_call(
        functools.partial(_inproj_kernel, tr=tr, cl=cl),
        grid=(b, nt),
        in_specs=[tok(D_MODEL), pl.BlockSpec((1, 16, D_MODEL), lambda bi, j: (bi, 0, 0)), full(g1), full(w_in),
                  pl.BlockSpec((tr, LANES), lambda bi, j: (j, 0)), pl.BlockSpec((tr, LANES), lambda bi, j: (j, 0)),
                  full(gq), full(gk), full(gvg), full(seg), full(seg64)],
        out_specs=[tok(w) for w in widths] + [tok(DT_PAD)],
        out_shape=[jax.ShapeDtypeStruct((b, s, w), BF16) for w in widths]
        + [jax.ShapeDtypeStruct((b, s, DT_PAD), F32)],
        compiler_params=_cparams(("parallel", "parallel")),
        name="in_projection",
    )(xs, mod, g1, w_in, cos, sin, gq, gk, gvg, seg, seg64)


def _gmlp_kernel(gu_ref, gv_ref, ws_ref, bs_ref, o_ref, *, nch):
    head = lax.broadcasted_iota(I32, (CHUNK, GM_W), 1) >> 6

    def body(c, carry):
        r0 = pl.multiple_of(c * CHUNK, CHUNK)
        v = gv_ref[0, pl.ds(r0, CHUNK), :]
        vbd = jnp.concatenate([jnp.where(head == hh, v, jnp.zeros_like(v)) for hh in range(GM_HEADS)], axis=0)
        sp = jnp.dot(ws_ref[...], vbd, preferred_element_type=F32) + bs_ref[...]
        o_ref[0, pl.ds(r0, CHUNK), :] = (gu_ref[0, pl.ds(r0, CHUNK), :].astype(F32) * sp).astype(BF16)
        return carry

    lax.fori_loop(0, nch, body, 0)


def _gmlp(gu, gv, ws_cat, bs_exp):
    b, s, _ = gu.shape
    seq = pl.BlockSpec((1, s, GM_W), lambda bi: (bi, 0, 0))
    return pl.pallas_call(
        functools.partial(_gmlp_kernel, nch=s // CHUNK),
        grid=(b,),
        in_specs=[seq, seq, pl.BlockSpec(ws_cat.shape, lambda bi: (0, 0)), pl.BlockSpec(bs_exp.shape, lambda bi: (0, 0))],
        out_specs=seq,
        out_shape=jax.ShapeDtypeStruct((b, s, GM_W), BF16),
        compiler_params=_cparams(("parallel",)),
        name="gmlp",
    )(gu, gv, ws_cat, bs_exp)


def _attn_kernel(q_ref, k_ref, v_ref, sink_ref, shift_ref, o_ref, *, nch, cl, fixed_shift):
    ncb = cl // CHUNK
    nslab = ATT_W // LANES
    lane = lax.broadcasted_iota(I32, (CHUNK, LANES), 1)
    first_head = ((lane >> 5) & 1) == 0
    low_half = lane < LANES // 2
    kc = k_ref[0, 0:cl, :]
    vc = v_ref[0, 0:cl, :]
    qi = lax.broadcasted_iota(I32, (CHUNK, CHUNK), 0)
    kl = lax.broadcasted_iota(I32, (CHUNK, CHUNK), 1)
    ninf = jnp.full((CHUNK, CHUNK), -jnp.inf, F32)
    zeros = jnp.zeros((CHUNK, CHUNK), F32)
    bias_prev = jnp.where(kl >= qi, zeros, ninf)
    bias_next = jnp.where(kl <= qi, zeros, ninf)

    def body(n, carry):
        r0 = pl.multiple_of(n * CHUNK, CHUNK)
        rp = pl.multiple_of(jnp.maximum(n - 1, ncb) * CHUNK, CHUNK)
        rn = pl.multiple_of(jnp.minimum(n + 1, nch - 1) * CHUNK, CHUNK)
        q = q_ref[0, pl.ds(r0, CHUNK), :]
        kk = jnp.concatenate([kc, k_ref[0, pl.ds(rp, CHUNK), :], k_ref[0, pl.ds(r0, CHUNK), :],
                              k_ref[0, pl.ds(rn, CHUNK), :]], axis=0)
        vv = jnp.concatenate([vc, v_ref[0, pl.ds(rp, CHUNK), :], v_ref[0, pl.ds(r0, CHUNK), :],
                              v_ref[0, pl.ds(rn, CHUNK), :]], axis=0)
        is_lat = n >= ncb
        bias = jnp.concatenate([
            jnp.where(jnp.logical_and(is_lat, n - 1 >= ncb), bias_prev, ninf),
            jnp.where(is_lat, zeros, ninf),
            jnp.where(jnp.logical_and(is_lat, n + 1 <= nch - 1), bias_next, ninf)], axis=1)
        zero = jnp.zeros((CHUNK, LANES), BF16)
        blocks = []
        for c in range(nslab):
            qc = q[:, c * LANES:(c + 1) * LANES]
            blocks += [jnp.where(first_head, qc, zero), jnp.where(first_head, zero, qc)]
        qs = jnp.concatenate(blocks, axis=0)
        sc = lax.dot_general(qs, kk, (((1,), (1,)), ((), ())), preferred_element_type=F32)
        s_ctx = sc[:, :cl]
        s_loc = sc[:, cl:] + jnp.concatenate([bias] * (2 * nslab), axis=0)
        sk = sink_ref[...]

        def lane_tiles(fn, a, b):
            tiles = [a[:, t * LANES:(t + 1) * LANES] for t in range(a.shape[1] // LANES)]
            tiles += [b[:, t * LANES:(t + 1) * LANES] for t in range(b.shape[1] // LANES)]
            return functools.reduce(fn, tiles)

        if fixed_shift:
            m = shift_ref[...]
        else:
            m = jnp.maximum(jnp.max(lane_tiles(jnp.maximum, s_ctx, s_loc), axis=-1, keepdims=True), sk)
        e_ctx = jnp.exp2(s_ctx - m)
        e_loc = jnp.exp2(s_loc - m)
        den = jnp.sum(lane_tiles(jnp.add, e_ctx, e_loc), axis=-1, keepdims=True) + jnp.exp2(sk - m)
        p = jnp.concatenate([e_ctx.astype(BF16), e_loc.astype(BF16)], axis=1)
        o = jnp.dot(p, vv, preferred_element_type=F32) * (1.0 / den)
        for c in range(nslab):
            oa = o[(2 * c) * CHUNK:(2 * c + 1) * CHUNK]
            ob = o[(2 * c + 1) * CHUNK:(2 * c + 2) * CHUNK]
            o_ref[0, pl.ds(r0, CHUNK), c * LANES:(c + 1) * LANES] = jnp.where(low_half, oa, ob).astype(BF16)
        return carry

    lax.fori_loop(0, nch, body, 0)


def _attention(q, k, v, sink_col, shift_col, *, cl, fixed_shift):
    b, s, _ = q.shape
    seq = lambda w: pl.BlockSpec((1, s, w), lambda bi: (bi, 0, 0))
    col = pl.BlockSpec(sink_col.shape, lambda bi: (0, 0))
    return pl.pallas_call(
        functools.partial(_attn_kernel, nch=s // CHUNK, cl=cl, fixed_shift=fixed_shift),
        grid=(b,),
        in_specs=[seq(ATT_W), seq(KV_W), seq(KV_W), col, col],
        out_specs=seq(ATT_W),
        out_shape=jax.ShapeDtypeStruct((b, s, ATT_W), BF16),
        compiler_params=_cparams(("parallel",)),
        name="attention_fixed_shift" if fixed_shift else "attention_row_max",
    )(q, k, v, sink_col, shift_col)


SSD_HALO = 16
SSD_CAT = 256
SSD_UNROLL = 2
AC_PIECES = 2


def _ssd_kernel(xbc_ref, dt_ref, cw_ref, cb_ref, dtb_ref, alog_ref, dsk_ref, shf_ref, sel_ref, pair_ref, y_ref,
                xc_ref, act_ref, acp_ref, e3_ref, yb_ref, st_ref, *, nch, cl):
    ncb = cl // CHUNK
    s = nch * CHUNK
    halo = SSD_HALO

    def conv_body(c, carry):
        r0 = pl.multiple_of(c * CHUNK, CHUNK)
        has_prev = jnp.logical_and(c != 0, c != ncb)
        has_next = jnp.logical_and(c != ncb - 1, c != nch - 1)
        rp = pl.multiple_of(jnp.maximum(r0 - halo, 0), halo)
        rn = pl.multiple_of(jnp.minimum(r0 + CHUNK, s - halo), halo)
        zero_halo = jnp.zeros((halo, XBC_W), BF16)
        prev = jnp.where(has_prev, xbc_ref[0, pl.ds(rp, halo), :], zero_halo)
        nxt = jnp.where(has_next, xbc_ref[0, pl.ds(rn, halo), :], zero_halo)
        cur = xbc_ref[0, pl.ds(r0, CHUNK), :]
        cat = jnp.concatenate([prev, cur, nxt, jnp.zeros((SSD_CAT - CHUNK - 2 * halo, XBC_W), BF16)], axis=0)
        sh = jnp.dot(shf_ref[...], cat, preferred_element_type=F32)
        acc = cb_ref[...] + cw_ref[CONV_K // 2:CONV_K // 2 + 1, :] * cur.astype(F32)
        for i, kk in enumerate([k for k in range(CONV_K) if k != CONV_K // 2]):
            acc = acc + cw_ref[kk:kk + 1, :] * sh[i * CHUNK:(i + 1) * CHUNK]
        xc_ref[pl.ds(r0, CHUNK), :] = _silu(acc).astype(BF16)
        dtv = jax.nn.softplus(dt_ref[0, pl.ds(r0, CHUNK), :] + dtb)
        dta = dtv * a_neg
        pre = jnp.dot(tri_f, dta, precision=HIGHEST, preferred_element_type=F32)
        fwd_col = tj < SSM_HEADS
        acum = jnp.where(fwd_col, pre, pre[CHUNK - 1:CHUNK, :] - pre + dta)
        a_end = jnp.where(fwd_col[0:1], acum[CHUNK - 1:CHUNK, :], acum[0:1, :])
        hi = acum.astype(BF16)
        lo = (acum - hi.astype(F32)).astype(BF16)
        acp_ref[pl.ds(r0, CHUNK), :] = jnp.concatenate([hi, lo], axis=1)
        act_ref[pl.ds(r0, CHUNK), :] = (hi.astype(F32) + lo.astype(F32)).T
        e3_ref[pl.ds(r0, CHUNK), :] = jnp.concatenate(
            [jnp.exp(acum), jnp.exp(a_end - acum) * dtv, dtv], axis=1).astype(BF16)
        return carry

    ti = lax.broadcasted_iota(I32, (CHUNK, CHUNK), 0)
    tj = lax.broadcasted_iota(I32, (CHUNK, CHUNK), 1)
    low_half = tj < LANES // 2
    tri_f = (ti >= tj).astype(F32)
    a_neg = -jnp.exp(alog_ref[...])
    dtb = dtb_ref[...]
    lax.fori_loop(0, nch, conv_body, 0)

    def chunk_update(c, direction, out_ref, out_lead):
        tri = (ti >= tj) if direction == 0 else (tj >= ti)
        last = CHUNK - 1 if direction == 0 else 0
        r0 = pl.multiple_of(c * CHUNK, CHUNK)
        xcv = xc_ref[pl.ds(r0, CHUNK), :]
        acum_t = act_ref[pl.ds(r0, CHUNK), :]
        acb = jnp.dot(acp_ref[pl.ds(r0, CHUNK), :], sel_ref[direction], preferred_element_type=F32)
        e3 = e3_ref[pl.ds(r0, CHUNK), :]
        prs = jnp.dot(jnp.concatenate([e3[:, i * LANES:(i + 1) * LANES] for i in range(3)], axis=0),
                      pair_ref[direction], preferred_element_type=F32)
        outs = []
        for g in range(2):
            xg = xcv[:, g * LANES:(g + 1) * LANES].astype(F32)
            bg = xcv[:, SSM_W + g * LANES:SSM_W + (g + 1) * LANES]
            cg = xcv[:, SSM_W + BC_W + g * LANES:SSM_W + BC_W + (g + 1) * LANES]
            gram = lax.dot_general(cg, bg, (((1,), (1,)), ((), ())), preferred_element_type=F32)
            eac_p = prs[0:CHUNK, g * LANES:(g + 1) * LANES]
            wend_p = prs[CHUNK:2 * CHUNK, g * LANES:(g + 1) * LANES]
            dt_p = prs[2 * CHUNK:3 * CHUNK, g * LANES:(g + 1) * LANES]
            mixes, dec = [], []
            for hh in range(2):
                h = 2 * g + hh
                col = direction * SSM_HEADS + h
                seg = jnp.exp(jnp.where(tri, acb[:, h * LANES:(h + 1) * LANES] - acum_t[col:col + 1, :], -jnp.inf))
                mixes.append((gram * seg).astype(BF16))
                dec.append(jnp.exp(acum_t[col:col + 1, last:last + 1]))
            xdt = (xg * dt_p).astype(BF16)
            y_intra = jnp.where(low_half, jnp.dot(mixes[0], xdt, preferred_element_type=F32),
                                jnp.dot(mixes[1], xdt, preferred_element_type=F32))
            stg = st_ref[direction, :, g * LANES:(g + 1) * LANES]
            y_inter = jnp.dot(cg, stg.astype(BF16), preferred_element_type=F32) * eac_p
            xw = (xg * wend_p).astype(BF16)
            contrib = lax.dot_general(bg, xw, (((0,), (0,)), ((), ())), preferred_element_type=F32)
            st_ref[direction, :, g * LANES:(g + 1) * LANES] = (
                stg * jnp.where(low_half[0:1], dec[0], dec[1]) + contrib)
            yg = y_intra + y_inter
            if direction == 0:
                yg = yg + dsk_ref[:, g * LANES:(g + 1) * LANES] * xg
            outs.append(yg)
        out_ref[out_lead + (pl.ds(r0, CHUNK), slice(None))] = jnp.concatenate(outs, axis=1)

    st_ref[...] = jnp.zeros_like(st_ref)

    def scan_body(it, carry):
        for u in range(SSD_UNROLL):
            step = it * SSD_UNROLL + u
            chunk_update(step, 0, y_ref, (0,))
            cb = jnp.where(step < ncb, ncb - 1 - step, nch - 1 - (step - ncb))
            chunk_update(cb, 1, yb_ref, ())
        return carry

    lax.fori_loop(0, nch // SSD_UNROLL, scan_body, 0)
    y_ref[0] = y_ref[0] + yb_ref[...]


def _ssd_constants():
    taps = [k for k in range(CONV_K) if k != CONV_K // 2]
    t = np.arange(CHUNK)[:, None]
    r = np.arange(SSD_CAT)[None, :]
    shf = np.concatenate([r == t + SSD_HALO - CONV_K // 2 + k for k in taps], axis=0)
    c = np.arange(LANES)[:, None]
    lane = np.arange(LANES)[None, :]
    sel = np.zeros((2, AC_PIECES * LANES, SSM_HEADS * LANES), bool)
    pair = np.zeros((2, LANES, 2 * LANES), bool)
    for d in range(2):
        for h in range(SSM_HEADS):
            for piece in range(AC_PIECES):
                sel[d, piece * LANES:(piece + 1) * LANES, h * LANES:(h + 1) * LANES] = c == d * SSM_HEADS + h
        for g in range(2):
            pair[d, :, g * LANES:(g + 1) * LANES] = c == d * SSM_HEADS + 2 * g + (lane >= LANES // 2)
    return jnp.asarray(shf, BF16), jnp.asarray(sel, BF16), jnp.asarray(pair, BF16)


def _ssd(xbc, dt, cw, cb, dtb, alog, dsk, consts, *, cl):
    b, s, _ = xbc.shape
    full = lambda a: pl.BlockSpec(a.shape, lambda bi: (0,) * a.ndim)
    return pl.pallas_call(
        functools.partial(_ssd_kernel, nch=s // CHUNK, cl=cl),
        grid=(b,),
        in_specs=[pl.BlockSpec((1, s, XBC_W), lambda bi: (bi, 0, 0)), pl.BlockSpec((1, s, DT_PAD), lambda bi: (bi, 0, 0)),
                  full(cw), full(cb), full(dtb), full(alog), full(dsk)] + [full(a) for a in consts],
        out_specs=pl.BlockSpec((1, s, SSM_W), lambda bi: (bi, 0, 0)),
        out_shape=jax.ShapeDtypeStruct((b, s, SSM_W), F32),
        scratch_shapes=[pltpu.VMEM((s, XBC_W), BF16), pltpu.VMEM((s, DT_PAD), F32),
                        pltpu.VMEM((s, AC_PIECES * LANES), BF16),
                        pltpu.VMEM((s, 3 * LANES), BF16), pltpu.VMEM((s, SSM_W), F32),
                        pltpu.VMEM((2, SSM_STATE, SSM_W), F32)],
        compiler_params=_cparams(("parallel",)),
        name="ssd",
    )(xbc, dt, cw, cb, dtb, alog, dsk, *consts)


TOK_ROWS = D_MODEL // LANES


def _load_tokens(ref, lead, n):
    return jnp.concatenate([ref[lead + (pl.ds(j, n, stride=TOK_ROWS), slice(None))] for j in range(TOK_ROWS)], axis=1)


def _store_tokens(ref, lead, val):
    for j in range(TOK_ROWS):
        ref[lead + (pl.ds(j, val.shape[0], stride=TOK_ROWS), slice(None))] = val[:, j * LANES:(j + 1) * LANES]


def _outproj_kernel(x_ref, mod_ref, gm_ref, att_ref, y_ref, z_ref, ng_ref, g2_ref, w_ref, xo_ref, hf_ref, *, tr, cl,
                    token_tiles):
    gated = y_ref[0] * _silu(z_ref[0].astype(F32))
    ssm = gated * lax.rsqrt(jnp.mean(gated * gated, axis=-1, keepdims=True) + EPS) * ng_ref[...]
    mix = jnp.concatenate([gm_ref[0], att_ref[0], ssm.astype(BF16)], axis=1)
    gate1, shift2, scale2 = _mod_rows(mod_ref, pl.program_id(1) * tr, tr, cl, (2, 3, 4))
    x = x_ref[0] + gate1 * jnp.dot(mix, w_ref[...], preferred_element_type=F32)
    xo_ref[0] = x
    ms = jnp.mean(x * x, axis=-1, keepdims=True)
    hf = x * lax.rsqrt(ms + EPS) * g2_ref[...] * (1.0 + scale2) + shift2
    if token_tiles:
        _store_tokens(hf_ref, (0,), hf)
    else:
        hf_ref[0] = hf.astype(hf_ref.dtype)


def _out_projection(xs, mod, gm, att, y, z, ng, g2, w_out, *, cl, nt, token_tiles):
    b, s, _ = xs.shape
    tr = s // nt
    tok = lambda w: pl.BlockSpec((1, tr, w), lambda bi, j: (bi, j, 0))
    full = lambda a: pl.BlockSpec(a.shape, lambda bi, j: (0,) * a.ndim)
    if token_tiles:
        hf_spec = pl.BlockSpec((1, tr * TOK_ROWS, LANES), lambda bi, j: (bi, j, 0))
        hf_shape = jax.ShapeDtypeStruct((b, s * TOK_ROWS, LANES), F32)
    else:
        hf_spec, hf_shape = tok(D_MODEL), jax.ShapeDtypeStruct((b, s, D_MODEL), BF16)
    return pl.pallas_call(
        functools.partial(_outproj_kernel, tr=tr, cl=cl, token_tiles=token_tiles),
        grid=(b, nt),
        in_specs=[tok(D_MODEL), pl.BlockSpec((1, 16, D_MODEL), lambda bi, j: (bi, 0, 0)), tok(GM_W), tok(ATT_W),
                  tok(SSM_W), tok(SSM_W), full(ng), full(g2), full(w_out)],
        out_specs=[tok(D_MODEL), hf_spec],
        out_shape=[jax.ShapeDtypeStruct((b, s, D_MODEL), F32), hf_shape],
        compiler_params=_cparams(("parallel", "parallel")),
        name="out_projection",
    )(xs, mod, gm, att, y, z, ng, g2, w_out)


def _swiglu_rows(h, wg_ref, wu_ref, wd_ref, act_ref):
    for f in range(D_FF // FF_BLOCK):
        cols = slice(f * FF_BLOCK, (f + 1) * FF_BLOCK)
        g = jnp.dot(h, wg_ref[0, :, cols], preferred_element_type=F32)
        u = jnp.dot(h, wu_ref[0, :, cols], preferred_element_type=F32)
        act_ref[:, cols] = (_silu(g) * u).astype(BF16)
    return jnp.dot(act_ref[...], wd_ref[0], preferred_element_type=F32)


def _ffn_kernel(x_ref, hf_ref, mod_ref, wg_ref, wu_ref, wd_ref, xo_ref, act_ref, *, tr, cl):
    (gate2,) = _mod_rows(mod_ref, pl.program_id(1) * tr, tr, cl, (5,))
    xo_ref[0] = x_ref[0] + gate2 * _swiglu_rows(hf_ref[0], wg_ref, wu_ref, wd_ref, act_ref)


def _dense_ffn(xs, hf, mod, wg, wu, wd, *, cl, nt):
    b, s, _ = xs.shape
    tr = s // nt
    tok = pl.BlockSpec((1, tr, D_MODEL), lambda bi, j: (bi, j, 0))
    full = lambda a: pl.BlockSpec(a.shape, lambda bi, j: (0,) * a.ndim)
    return pl.pallas_call(
        functools.partial(_ffn_kernel, tr=tr, cl=cl),
        grid=(b, nt),
        in_specs=[tok, tok, pl.BlockSpec((1, 16, D_MODEL), lambda bi, j: (bi, 0, 0)), full(wg), full(wu), full(wd)],
        out_specs=tok,
        out_shape=jax.ShapeDtypeStruct((b, s, D_MODEL), F32),
        scratch_shapes=[pltpu.VMEM((tr, D_FF), BF16)],
        compiler_params=_cparams(("parallel", "parallel")),
        name="dense_ffn",
    )(xs, hf, mod, wg, wu, wd)


def _router_kernel(hf_ref, wr_ref, tri_ref, mi_ref, mf_ref, cnt_ref, run_ref, *, trc):
    @pl.when(pl.program_id(0) == 0)
    def _():
        run_ref[...] = jnp.zeros_like(run_ref)

    logits = lax.dot_general(wr_ref[...], _load_tokens(hf_ref, (), trc), (((1,), (1,)), ((), ())), precision=HIGHEST,
                             preferred_element_type=F32)
    eidx = lax.broadcasted_iota(I32, (N_EXPERTS, trc), 0).astype(F32)
    m1 = jnp.max(logits, axis=0, keepdims=True)
    i1 = jnp.min(jnp.where(logits == m1, eidx, float(N_EXPERTS)), axis=0, keepdims=True)
    rest = jnp.where(eidx == i1, -jnp.inf, logits)
    m2 = jnp.max(rest, axis=0, keepdims=True)
    i2 = jnp.min(jnp.where(rest == m2, eidx, float(N_EXPERTS)), axis=0, keepdims=True)
    e2 = jnp.exp(m2 - m1)
    g1 = 1.0 / (1.0 + e2)
    g2 = e2 / (1.0 + e2)
    oh1 = (eidx == i1).astype(F32)
    oh2 = (eidx == i2).astype(F32)
    sel = oh1 + oh2
    before = jnp.dot(sel.astype(BF16), tri_ref[...], preferred_element_type=F32) + run_ref[:, 0:1]
    r1 = jnp.sum(oh1 * before, axis=0, keepdims=True)
    r2 = jnp.sum(oh2 * before, axis=0, keepdims=True)
    run_ref[...] = run_ref[...] + jnp.sum(sel, axis=1, keepdims=True)
    zi = jnp.zeros((4, trc), I32)
    mi_ref[0] = jnp.concatenate([i1.astype(I32), i2.astype(I32), r1.astype(I32), r2.astype(I32), zi], axis=0)
    mf_ref[0] = jnp.concatenate([g1, g2, jnp.zeros((6, trc), F32)], axis=0)
    cnt_ref[...] = run_ref[...].astype(I32)


def _router(hf_tiles, wr_t, tri, *, trc):
    t = hf_tiles.shape[0] // TOK_ROWS
    nt = t // trc
    meta = pl.BlockSpec((1, 8, trc), lambda i: (i, 0, 0))
    return pl.pallas_call(
        functools.partial(_router_kernel, trc=trc),
        grid=(nt,),
        in_specs=[pl.BlockSpec((trc * TOK_ROWS, LANES), lambda i: (i, 0)), pl.BlockSpec(wr_t.shape, lambda i: (0, 0)),
                  pl.BlockSpec(tri.shape, lambda i: (0, 0))],
        out_specs=[meta, meta, pl.BlockSpec((N_EXPERTS, LANES), lambda i: (0, 0))],
        out_shape=[jax.ShapeDtypeStruct((nt, 8, trc), I32), jax.ShapeDtypeStruct((nt, 8, trc), F32),
                   jax.ShapeDtypeStruct((N_EXPERTS, LANES), I32)],
        scratch_shapes=[pltpu.VMEM((N_EXPERTS, LANES), F32)],
        compiler_params=_cparams(("arbitrary",)),
        name="moe_router",
    )(hf_tiles, wr_t, tri)


DMA_UNROLL = 8


def _token_copy(src_ref, src_tok, dst_ref, dst_tok, sem):
    src = src_ref.at[pl.ds(pl.multiple_of(src_tok * TOK_ROWS, TOK_ROWS), TOK_ROWS), :]
    dst = dst_ref.at[pl.ds(pl.multiple_of(dst_tok * TOK_ROWS, TOK_ROWS), TOK_ROWS), :]
    return pltpu.make_async_copy(src, dst, sem)


def _for_tokens(n, fn):
    def block(i, carry):
        for u in range(DMA_UNROLL):
            fn(i * DMA_UNROLL + u)
        return carry

    lax.fori_loop(0, n // DMA_UNROLL, block, 0)


def _dispatch_kernel(pad_lo_ref, pad_n_ref, nact_ref, slot_ref, hf_ref, xs_ref, zero_ref, sem, zsem, *, trc, ntile):
    @pl.when(pl.program_id(0) == 0)
    def _():
        zero_ref[...] = jnp.zeros_like(zero_ref)
        tile_rows = EXPERT_TILE * TOK_ROWS

        def tile_copy(i):
            return pltpu.make_async_copy(
                zero_ref, xs_ref.at[pl.ds(pl.multiple_of(i * tile_rows, tile_rows), tile_rows), :], zsem)

        def fill_tile(i, carry):
            tile_copy(i).start()
            return carry

        def drain_tile(i, carry):
            tile_copy(i).wait()
            return carry

        lax.fori_loop(nact_ref[0], ntile, fill_tile, 0)
        lax.fori_loop(nact_ref[0], ntile, drain_tile, 0)
        for e in range(N_EXPERTS):
            lo = pad_lo_ref[e]
            n = pad_n_ref[e]

            def fill(r, carry):
                _token_copy(zero_ref, 0, xs_ref, lo + r, zsem).start()
                return carry

            def drain(r, carry):
                _token_copy(zero_ref, 0, xs_ref, lo + r, zsem).wait()
                return carry

            lax.fori_loop(0, n, fill, 0)
            lax.fori_loop(0, n, drain, 0)

    copies = lambda r: [_token_copy(hf_ref, r, xs_ref, slot_ref[0, k, r], sem) for k in range(2)]
    _for_tokens(trc, lambda r: [cp.start() for cp in copies(r)])
    _for_tokens(trc, lambda r: [cp.wait() for cp in copies(r)])


def _dispatch(pad_lo, pad_n, nact, slots, hf_tiles, nslot, *, trc):
    t = hf_tiles.shape[0] // TOK_ROWS
    nt = t // trc
    grid_spec = pltpu.PrefetchScalarGridSpec(
        num_scalar_prefetch=3,
        grid=(nt,),
        in_specs=[pl.BlockSpec((1, 2, trc), lambda i, lo, n, na: (i, 0, 0), memory_space=pltpu.SMEM),
                  pl.BlockSpec((trc * TOK_ROWS, LANES), lambda i, lo, n, na: (i, 0))],
        out_specs=pl.BlockSpec(memory_space=pl.ANY),
        scratch_shapes=[pltpu.VMEM((EXPERT_TILE * TOK_ROWS, LANES), F32), pltpu.SemaphoreType.DMA(()),
                        pltpu.SemaphoreType.DMA(())],
    )
    return pl.pallas_call(
        functools.partial(_dispatch_kernel, trc=trc, ntile=nslot // EXPERT_TILE),
        grid_spec=grid_spec,
        out_shape=jax.ShapeDtypeStruct((nslot * TOK_ROWS, LANES), F32),
        compiler_params=_cparams(("arbitrary",)),
        name="moe_dispatch",
    )(pad_lo, pad_n, nact, slots, hf_tiles)


def _expert_kernel(texp_ref, nact_ref, xs_ref, wg_ref, wu_ref, wd_ref, ys_ref, act_ref):
    active = pl.program_id(0) < nact_ref[0]

    @pl.when(active)
    def _():
        h = _load_tokens(xs_ref, (), EXPERT_TILE).astype(BF16)
        _store_tokens(ys_ref, (), _swiglu_rows(h, wg_ref, wu_ref, wd_ref, act_ref))

    @pl.when(jnp.logical_not(active))
    def _():
        ys_ref[...] = jnp.zeros_like(ys_ref)


def _expert_ffn(tile_expert, nact, xs, wg, wu, wd):
    ntile = xs.shape[0] // (EXPERT_TILE * TOK_ROWS)
    wspec = lambda shp: pl.BlockSpec((1,) + shp, lambda i, te, na: (te[i], 0, 0))
    tile = pl.BlockSpec((EXPERT_TILE * TOK_ROWS, LANES), lambda i, te, na: (i, 0))
    grid_spec = pltpu.PrefetchScalarGridSpec(
        num_scalar_prefetch=2,
        grid=(ntile,),
        in_specs=[tile, wspec((D_MODEL, D_FF)), wspec((D_MODEL, D_FF)), wspec((D_FF, D_MODEL))],
        out_specs=tile,
        scratch_shapes=[pltpu.VMEM((EXPERT_TILE, D_FF), BF16)],
    )
    return pl.pallas_call(
        _expert_kernel,
        grid_spec=grid_spec,
        out_shape=jax.ShapeDtypeStruct(xs.shape, F32),
        compiler_params=_cparams(("arbitrary",)),
        name="moe_experts",
    )(tile_expert, nact, xs, wg, wu, wd)


def _combine_kernel(slot_ref, x_ref, gate_ref, mod_ref, ys_ref, xo_ref, buf_ref, sem, *, trc, nt, cl):
    copies = lambda r: [_token_copy(ys_ref, slot_ref[0, k, r], buf_ref.at[k], r, sem) for k in range(2)]
    _for_tokens(trc, lambda r: [cp.start() for cp in copies(r)])
    _for_tokens(trc, lambda r: [cp.wait() for cp in copies(r)])
    (gate2,) = _mod_rows(mod_ref, (pl.program_id(0) % nt) * trc, trc, cl, (5,))
    gt = gate_ref[0]
    f = gt[:, 0:1] * _load_tokens(buf_ref, (0,), trc) + gt[:, 1:2] * _load_tokens(buf_ref, (1,), trc)
    xo_ref[...] = x_ref[...] + gate2 * f


def _combine(slots, x_flat, gates_t, mod, ys, *, trc, nt, cl):
    t = x_flat.shape[0]
    ntile = t // trc
    tok = pl.BlockSpec((trc, D_MODEL), lambda i: (i, 0))
    return pl.pallas_call(
        functools.partial(_combine_kernel, trc=trc, nt=nt, cl=cl),
        grid=(ntile,),
        in_specs=[pl.BlockSpec((1, 2, trc), lambda i: (i, 0, 0), memory_space=pltpu.SMEM), tok,
                  pl.BlockSpec((1, trc, 2), lambda i: (i, 0, 0)),
                  pl.BlockSpec((1, 16, D_MODEL), lambda i: (i // nt, 0, 0)), pl.BlockSpec(memory_space=pl.ANY)],
        out_specs=tok,
        out_shape=jax.ShapeDtypeStruct((t, D_MODEL), F32),
        scratch_shapes=[pltpu.VMEM((2, trc * TOK_ROWS, LANES), F32), pltpu.SemaphoreType.DMA(())],
        compiler_params=_cparams(("arbitrary",)),
        name="moe_combine",
    )(slots, x_flat, gates_t, mod, ys)


def _moe_ffn(xs, hf, mod, wr_t, tri, wg, wu, wd, *, cl, nt):
    b, s, _ = xs.shape
    t = b * s
    trr = tri.shape[0]
    trc = s // nt
    hf_flat = hf.reshape(t * TOK_ROWS, LANES)
    meta_i, meta_f, counts = _router(hf_flat, wr_t, tri, trc=trr)
    counts = counts[:, 0]
    padded = (counts + EXPERT_TILE - 1) // EXPERT_TILE * EXPERT_TILE
    ends = jnp.cumsum(padded)
    starts = ends - padded
    eid = meta_i[:, 0:2, :]
    group_start = sum(jnp.where(eid == e, starts[e], 0) for e in range(N_EXPERTS))
    slots = group_start + meta_i[:, 2:4, :]
    nslot = 2 * t + N_EXPERTS * EXPERT_TILE
    ntile = nslot // EXPERT_TILE
    nact = (ends[-1] // EXPERT_TILE).astype(I32)
    tile_lo = jnp.minimum(jnp.arange(ntile, dtype=I32), nact - 1) * EXPERT_TILE
    tile_expert = jnp.minimum(jnp.sum(tile_lo[:, None] >= ends[None, :], axis=1), N_EXPERTS - 1).astype(I32)
    nact = nact.reshape(1)
    xs_sorted = _dispatch((starts + counts).astype(I32), (padded - counts).astype(I32), nact, slots, hf_flat, nslot,
                          trc=trr)
    ys = _expert_ffn(tile_expert, nact, xs_sorted, wg, wu, wd)
    retile = lambda a: jnp.swapaxes(jnp.swapaxes(a, 0, 1).reshape(2, t // trc, trc), 0, 1)
    gates_t = jnp.swapaxes(retile(meta_f[:, 0:2, :]), 1, 2)
    out = _combine(retile(slots), xs.reshape(t, D_MODEL), gates_t, mod, ys, trc=trc, nt=nt, cl=cl)
    return out.reshape(b, s, D_MODEL)


def _slab_perm(t, lead):
    nh = t.shape[-1] // HEAD_DIM
    per = nh // 2
    t = t.reshape(lead + (2, per, 2, 2, 16))
    t = jnp.moveaxis(t, (-5, -4, -3, -2, -1), (-3, -5, -2, -4, -1))
    return t.reshape(lead + (nh * HEAD_DIM,))


def _rope_tables(s, cl):
    l = s - cl
    pos = np.arange(l)
    lane = np.arange(LANES)
    i = lane % 32
    freq = ROPE_BASE ** (-(i % 16).astype(np.float32) / 16.0)
    p = np.where(i[None, :] < 16, (pos // GRID_W)[:, None], (pos % GRID_W)[:, None]).astype(np.float32)
    ang = p * freq[None, :].astype(np.float32)
    sign = np.where(lane < LANES // 2, -1.0, 1.0).astype(np.float32)
    cos = np.concatenate([np.ones((cl, LANES), np.float32), np.cos(ang)], axis=0)
    sin = np.concatenate([np.zeros((cl, LANES), np.float32), np.sin(ang) * sign[None, :]], axis=0)
    return jnp.asarray(cos, F32), jnp.asarray(sin, F32)


def kernel(x, c, ctx, c_ctx, w_mod, b_mod, norm1_g, norm2_g, w_in, w_out, gm_v_g, gm_ws, gm_bs, att_q_g, att_k_g, att_sink, ssm_conv_w, ssm_conv_b, ssm_dt_bias, ssm_a_log, ssm_d, ssm_norm_g, ffn_w_gate, ffn_w_up, ffn_w_down, moe_router, moe_w_gate, moe_w_up, moe_w_down):
    b, l, _ = x.shape
    cl = ctx.shape[1]
    s = cl + l
    depth = w_mod.shape[0]
    nt = 8
    nt_moe = 4
    assert s % (nt * 16) == 0 and cl % CHUNK == 0 and l % CHUNK == 0 and l % GRID_W == 0 and b < 16

    cvec = jnp.concatenate([c, c_ctx[None, :], jnp.zeros((16 - b - 1, D_MODEL), F32)], axis=0)
    mods = _modulation(cvec, w_mod, b_mod)
    lat = jnp.moveaxis(mods[:, :, :b, :], 2, 1)
    con = jnp.broadcast_to(mods[:, None, :, b, :], (depth, b, 6, D_MODEL))
    pad2 = jnp.zeros((depth, b, 2, D_MODEL), F32)
    modtab = jnp.concatenate([lat, pad2, con, pad2], axis=2)

    wi = w_in
    w_in_p = jnp.concatenate([
        wi[:, :, 0:2 * GM_W],
        _slab_perm(wi[:, :, 512:1024], (depth, D_MODEL)),
        _slab_perm(wi[:, :, 1024:1152], (depth, D_MODEL)),
        wi[:, :, 1152:2304],
        jnp.pad(wi[:, :, 2304:2312], ((0, 0), (0, 0), (0, DT_PAD - 2 * SSM_HEADS))),
    ], axis=2).astype(BF16)
    wo_att = w_out[:, GM_W:GM_W + ATT_W, :].reshape(depth, 2, 4, HEAD_DIM, D_MODEL)
    wo_att = jnp.swapaxes(wo_att, 1, 2).reshape(depth, ATT_W, D_MODEL)
    w_out_p = jnp.concatenate([w_out[:, :GM_W], wo_att, w_out[:, GM_W + ATT_W:]], axis=1).astype(BF16)
    gq = _slab_perm(jnp.tile(att_q_g, (1, 2)), (depth,))[:, None, :]
    gk = _slab_perm(jnp.tile(att_k_g, (1, 2)), (depth,))[:, None, :]
    lane = np.arange(LANES)
    seg = jnp.asarray(((lane[:, None] // 32) % 2 == (lane[None, :] // 32) % 2), BF16)
    lane2 = np.arange(GM_W)
    seg64 = jnp.asarray(lane2[:, None] // GM_HD == lane2[None, :] // GM_HD, BF16)
    cos, sin = _rope_tables(s, cl)
    ws_cat = jnp.swapaxes(gm_ws, 1, 2).reshape(depth, CHUNK, GM_HEADS * CHUNK).astype(BF16)
    bs_exp = jnp.repeat(jnp.swapaxes(gm_bs, 1, 2), GM_HD, axis=2)
    sink_heads = att_sink.reshape(depth, 2, 4).swapaxes(1, 2).reshape(depth, ATT_HEADS)
    sink_col = jnp.repeat(sink_heads, CHUNK, axis=1)[:, :, None] * LOG2E
    score_bound = (8.0 * (1.0 + 2.0 ** -6)) * jnp.max(jnp.abs(att_q_g), axis=1) * jnp.max(jnp.abs(att_k_g), axis=1)
    shift_col = jnp.maximum(sink_col, score_bound[:, None, None] * LOG2E)
    cw = jnp.pad(ssm_conv_w, ((0, 0), (0, 8 - CONV_K), (0, 0)))
    cb = ssm_conv_b[:, None, :]
    pad_dt = lambda t: jnp.pad(t.reshape(depth, 1, 2 * SSM_HEADS), ((0, 0), (0, 0), (0, DT_PAD - 2 * SSM_HEADS)))
    dtb = pad_dt(ssm_dt_bias)
    alog = pad_dt(ssm_a_log)
    dsk = jnp.repeat(ssm_d, SSM_HD, axis=1)[:, None, :]
    ssd_consts = _ssd_constants()
    wr_t = jnp.swapaxes(moe_router, 1, 2)
    trr = 1024 if (b * s) % 1024 == 0 else 512
    assert (b * s) % trr == 0
    tidx = np.arange(trr)
    tri = jnp.asarray(tidx[:, None] < tidx[None, :], BF16)
    ffn_g, ffn_u, ffn_d = ffn_w_gate.astype(BF16), ffn_w_up.astype(BF16), ffn_w_down.astype(BF16)
    moe_g, moe_u, moe_d = moe_w_gate.astype(BF16), moe_w_up.astype(BF16), moe_w_down.astype(BF16)

    xs = jnp.concatenate([ctx, x], axis=1)
    for i in range(depth):
        moe = i % 2 == 1
        mod = modtab[i]
        gu, gv, q, k, v, z, xbc, dt = _in_projection(
            xs, mod, norm1_g[i][None, :], w_in_p[i], cos, sin, gq[i], gk[i], gm_v_g[i][None, :], seg, seg64,
            cl=cl, nt=nt)
        gm = _gmlp(gu, gv, ws_cat[i], bs_exp[i])
        att = lax.cond(score_bound[i] <= SAFE_SCORE_BOUND,
                       functools.partial(_attention, cl=cl, fixed_shift=True),
                       functools.partial(_attention, cl=cl, fixed_shift=False),
                       q, k, v, sink_col[i], shift_col[i])
        y = _ssd(xbc, dt, cw[i], cb[i], dtb[i], alog[i], dsk[i], ssd_consts, cl=cl)
        xs, hf = _out_projection(xs, mod, gm, att, y, z, ssm_norm_g[i][None, :], norm2_g[i][None, :], w_out_p[i],
                                 cl=cl, nt=nt, token_tiles=moe)
        j = i // 2
        if moe:
            xs = _moe_ffn(xs, hf, mod, wr_t[j], tri, moe_g[j], moe_u[j], moe_d[j], cl=cl, nt=nt_moe)
        else:
            xs = _dense_ffn(xs, hf, mod, ffn_g[j:j + 1], ffn_u[j:j + 1], ffn_d[j:j + 1], cl=cl, nt=nt)
    return xs[:, cl:, :]
```

```python
import functools
import math

import numpy as np
import jax
import jax.numpy as jnp
from jax import lax
from jax.experimental import pallas as pl
from jax.experimental.pallas import tpu as pltpu

F32 = jnp.float32
BF16 = jnp.bfloat16
I32 = jnp.int32
HIGHEST = lax.Precision.HIGHEST

D_MODEL = 1024
CHUNK = 128
GM_HEADS, GM_HD, GM_W = 4, 64, 256
ATT_HEADS, KV_HEADS, HEAD_DIM = 8, 2, 64
ATT_W, KV_W = 512, 128
GRID_W = 64
ROPE_BASE = 10000.0
SSM_HEADS, SSM_HD, SSM_W = 4, 64, 256
SSM_STATE, BC_W, CONV_K, XBC_W = 128, 256, 5, 768
D_FF = 2816
N_EXPERTS = 8
EPS = 1e-6
LANES = 128
DT_PAD = 128
C_GU, C_GV, C_Q, C_K, C_V, C_Z, C_XBC, C_DT, IN_WP = 0, 256, 512, 1024, 1152, 1280, 1536, 2304, 2432
FF_BLOCK = 256
EXPERT_TILE = 512
VMEM_LIMIT = 56 * 1024 * 1024
LOG2E = math.log2(math.e)
SAFE_SCORE_BOUND = 40.0


def _cparams(sem):
    return pltpu.CompilerParams(dimension_semantics=sem, vmem_limit_bytes=VMEM_LIMIT)


def _silu(x):
    return x * jax.nn.sigmoid(x)


def _mod_kernel(c_ref, w_ref, b_ref, o_ref):
    cs = _silu(c_ref[...])
    o_ref[0, 0] = jnp.dot(cs, w_ref[0], precision=HIGHEST, preferred_element_type=F32) + b_ref[0, 0]


def _modulation(cvec, w_mod, b_mod):
    depth = w_mod.shape[0]
    r = cvec.shape[0]
    return pl.pallas_call(
        _mod_kernel,
        grid=(depth, 6),
        in_specs=[
            pl.BlockSpec((r, D_MODEL), lambda i, n: (0, 0)),
            pl.BlockSpec((1, D_MODEL, D_MODEL), lambda i, n: (i, 0, n)),
            pl.BlockSpec((1, 1, 1, D_MODEL), lambda i, n: (i, n, 0, 0)),
        ],
        out_specs=pl.BlockSpec((1, 1, r, D_MODEL), lambda i, n: (i, n, 0, 0)),
        out_shape=jax.ShapeDtypeStruct((depth, 6, r, D_MODEL), F32),
        compiler_params=_cparams(("arbitrary", "arbitrary")),
        name="modulation",
    )(cvec, w_mod, b_mod.reshape(depth, 6, 1, D_MODEL))


def _mod_rows(mod_ref, tile_row0, rows, cl, lo):
    ridx = tile_row0 + lax.broadcasted_iota(I32, (rows, 1), 0)
    is_ctx = ridx < cl
    m = mod_ref[0]
    return [jnp.where(is_ctx, m[8 + k:9 + k], m[k:k + 1]) for k in lo]


def _inproj_kernel(x_ref, mod_ref, g1_ref, w_ref, cos_ref, sin_ref, gq_ref, gk_ref, gvg_ref, seg_ref, seg64_ref,
                   gu_ref, gv_ref, q_ref, k_ref, v_ref, z_ref, xbc_ref, dt_ref, *, tr, cl):
    x = x_ref[0]
    ms = jnp.mean(x * x, axis=-1, keepdims=True)
    y = x * lax.rsqrt(ms + EPS) * g1_ref[...]
    shift, scale = _mod_rows(mod_ref, pl.program_id(1) * tr, tr, cl, (0, 1))
    h = (y * (1.0 + scale) + shift).astype(BF16)

    full = jnp.dot(h, w_ref[...], preferred_element_type=F32)

    def proj(a, b):
        return full[:, a:b]

    guv = proj(C_GU, C_Q)
    gu_ref[0] = jax.nn.gelu(guv[:, :GM_W]).astype(BF16)
    gv = jax.nn.gelu(guv[:, GM_W:])
    ssv = jnp.dot((gv * gv).astype(BF16), seg64_ref[...], preferred_element_type=F32)
    gv_ref[0] = (gv * lax.rsqrt(ssv * (1.0 / GM_HD) + EPS) * gvg_ref[...]).astype(BF16)

    cos = cos_ref[...]
    sin = sin_ref[...]

    def norm_rope(t, gain):
        ss = jnp.dot((t * t).astype(BF16), seg_ref[...], preferred_element_type=F32)
        tn = t * lax.rsqrt(ss * (1.0 / HEAD_DIM) + EPS) * gain
        return tn * cos + pltpu.roll(tn, LANES // 2, axis=1) * sin

    qkv = proj(C_Q, C_Z)
    gq = gq_ref[...] * (HEAD_DIM ** -0.5 * LOG2E)
    for c in range(ATT_W // LANES):
        q_ref[0, :, c * LANES:(c + 1) * LANES] = norm_rope(qkv[:, c * LANES:(c + 1) * LANES], gq).astype(BF16)
    k_ref[0] = norm_rope(qkv[:, ATT_W:ATT_W + KV_W], gk_ref[...]).astype(BF16)
    v_ref[0] = qkv[:, ATT_W + KV_W:].astype(BF16)
    z_ref[0] = proj(C_Z, C_XBC).astype(BF16)
    xbc_ref[0] = proj(C_XBC, C_DT).astype(BF16)
    dt_ref[0] = proj(C_DT, IN_WP)


def _in_projection(xs, mod, g1, w_in, cos, sin, gq, gk, gvg, seg, seg64, *, cl, nt):
    b, s, _ = xs.shape
    tr = s // nt
    tok = lambda w: pl.BlockSpec((1, tr, w), lambda bi, j: (bi, j, 0))
    full = lambda a: pl.BlockSpec(a.shape, lambda bi, j: (0,) * a.ndim)
    widths = (GM_W, GM_W, ATT_W, KV_W, KV_W, SSM_W, XBC_W)
    return pl.pallas_call(
        functools.partial(_inproj_kernel, tr=tr, cl=cl),
        grid=(b, nt),
        in_specs=[tok(D_MODEL), pl.BlockSpec((1, 16, D_MODEL), lambda bi, j: (bi, 0, 0)), full(g1), full(w_in),
                  pl.BlockSpec((tr, LANES), lambda bi, j: (j, 0)), pl.BlockSpec((tr, LANES), lambda bi, j: (j, 0)),
                  full(gq), full(gk), full(gvg), full(seg), full(seg64)],
        out_specs=[tok(w) for w in widths] + [tok(DT_PAD)],
        out_shape=[jax.ShapeDtypeStruct((b, s, w), BF16) for w in widths]
        + [jax.ShapeDtypeStruct((b, s, DT_PAD), F32)],
        compiler_params=_cparams(("parallel", "parallel")),
        name="in_projection",
    )(xs, mod, g1, w_in, cos, sin, gq, gk, gvg, seg, seg64)


GMLP_UNROLL = 2


def _gmlp_kernel(gu_ref, gv_ref, ws_ref, bs_ref, o_ref, *, nch):
    head = lax.broadcasted_iota(I32, (CHUNK, GM_W), 1) >> 6

    def body(it, carry):
        for u in range(GMLP_UNROLL):
            r0 = pl.multiple_of((it * GMLP_UNROLL + u) * CHUNK, CHUNK)
            v = gv_ref[0, pl.ds(r0, CHUNK), :]
            vbd = jnp.concatenate([jnp.where(head == hh, v, jnp.zeros_like(v)) for hh in range(GM_HEADS)], axis=0)
            sp = jnp.dot(ws_ref[...], vbd, preferred_element_type=F32) + bs_ref[...]
            o_ref[0, pl.ds(r0, CHUNK), :] = (gu_ref[0, pl.ds(r0, CHUNK), :].astype(F32) * sp).astype(BF16)
        return carry

    lax.fori_loop(0, nch // GMLP_UNROLL, body, 0)


def _gmlp(gu, gv, ws_cat, bs_exp):
    b, s, _ = gu.shape
    seq = pl.BlockSpec((1, s, GM_W), lambda bi: (bi, 0, 0))
    return pl.pallas_call(
        functools.partial(_gmlp_kernel, nch=s // CHUNK),
        grid=(b,),
        in_specs=[seq, seq, pl.BlockSpec(ws_cat.shape, lambda bi: (0, 0)), pl.BlockSpec(bs_exp.shape, lambda bi: (0, 0))],
        out_specs=seq,
        out_shape=jax.ShapeDtypeStruct((b, s, GM_W), BF16),
        compiler_params=_cparams(("parallel",)),
        name="gmlp",
    )(gu, gv, ws_cat, bs_exp)


def _attn_kernel(q_ref, k_ref, v_ref, sink_ref, shift_ref, o_ref, *, nch, cl, fixed_shift):
    ncb = cl // CHUNK
    nslab = ATT_W // LANES
    lane = lax.broadcasted_iota(I32, (CHUNK, LANES), 1)
    first_head = ((lane >> 5) & 1) == 0
    low_half = lane < LANES // 2
    kc = k_ref[0, 0:cl, :]
    vc = v_ref[0, 0:cl, :]
    qi = lax.broadcasted_iota(I32, (CHUNK, CHUNK), 0)
    kl = lax.broadcasted_iota(I32, (CHUNK, CHUNK), 1)
    ninf = jnp.full((CHUNK, CHUNK), -jnp.inf, F32)
    zeros = jnp.zeros((CHUNK, CHUNK), F32)
    bias_prev = jnp.where(kl >= qi, zeros, ninf)
    bias_next = jnp.where(kl <= qi, zeros, ninf)

    def body(n, carry):
        r0 = pl.multiple_of(n * CHUNK, CHUNK)
        rp = pl.multiple_of(jnp.maximum(n - 1, ncb) * CHUNK, CHUNK)
        rn = pl.multiple_of(jnp.minimum(n + 1, nch - 1) * CHUNK, CHUNK)
        q = q_ref[0, pl.ds(r0, CHUNK), :]
        kk = jnp.concatenate([kc, k_ref[0, pl.ds(rp, CHUNK), :], k_ref[0, pl.ds(r0, CHUNK), :],
                              k_ref[0, pl.ds(rn, CHUNK), :]], axis=0)
        vv = jnp.concatenate([vc, v_ref[0, pl.ds(rp, CHUNK), :], v_ref[0, pl.ds(r0, CHUNK), :],
                              v_ref[0, pl.ds(rn, CHUNK), :]], axis=0)
        is_lat = n >= ncb
        bias = jnp.concatenate([
            jnp.where(jnp.logical_and(is_lat, n - 1 >= ncb), bias_prev, ninf),
            jnp.where(is_lat, zeros, ninf),
            jnp.where(jnp.logical_and(is_lat, n + 1 <= nch - 1), bias_next, ninf)], axis=1)
        zero = jnp.zeros((CHUNK, LANES), BF16)
        blocks = []
        for c in range(nslab):
            qc = q[:, c * LANES:(c + 1) * LANES]
            blocks += [jnp.where(first_head, qc, zero), jnp.where(first_head, zero, qc)]
        qs = jnp.concatenate(blocks, axis=0)
        sc = lax.dot_general(qs, kk, (((1,), (1,)), ((), ())), preferred_element_type=F32)
        s_ctx = sc[:, :cl]
        s_loc = sc[:, cl:] + jnp.concatenate([bias] * (2 * nslab), axis=0)
        sk = sink_ref[...]

        def lane_tiles(fn, a, b):
            tiles = [a[:, t * LANES:(t + 1) * LANES] for t in range(a.shape[1] // LANES)]
            tiles += [b[:, t * LANES:(t + 1) * LANES] for t in range(b.shape[1] // LANES)]
            return functools.reduce(fn, tiles)

        if fixed_shift:
            m = shift_ref[...]
        else:
            m = jnp.maximum(jnp.max(lane_tiles(jnp.maximum, s_ctx, s_loc), axis=-1, keepdims=True), sk)
        e_ctx = jnp.exp2(s_ctx - m)
        e_loc = jnp.exp2(s_loc - m)
        p = jnp.concatenate([e_ctx.astype(BF16), e_loc.astype(BF16)], axis=1)
        pv = jnp.dot(p, jnp.concatenate([vv, jnp.ones_like(vv)], axis=1), preferred_element_type=F32)
        o = pv[:, :LANES] * (1.0 / (pv[:, LANES:] + jnp.exp2(sk - m)))
        for c in range(nslab):
            oa = o[(2 * c) * CHUNK:(2 * c + 1) * CHUNK]
            ob = o[(2 * c + 1) * CHUNK:(2 * c + 2) * CHUNK]
            o_ref[0, pl.ds(r0, CHUNK), c * LANES:(c + 1) * LANES] = jnp.where(low_half, oa, ob).astype(BF16)
        return carry

    lax.fori_loop(0, nch, body, 0)


def _attention(q, k, v, sink_col, shift_col, *, cl, fixed_shift):
    b, s, _ = q.shape
    seq = lambda w: pl.BlockSpec((1, s, w), lambda bi: (bi, 0, 0))
    col = pl.BlockSpec(sink_col.shape, lambda bi: (0, 0))
    return pl.pallas_call(
        functools.partial(_attn_kernel, nch=s // CHUNK, cl=cl, fixed_shift=fixed_shift),
        grid=(b,),
        in_specs=[seq(ATT_W), seq(KV_W), seq(KV_W), col, col],
        out_specs=seq(ATT_W),
        out_shape=jax.ShapeDtypeStruct((b, s, ATT_W), BF16),
        compiler_params=_cparams(("parallel",)),
        name="attention_fixed_shift" if fixed_shift else "attention_row_max",
    )(q, k, v, sink_col, shift_col)


SSD_HALO = 16
SSD_CAT = 256
SSD_UNROLL = 2
AC_PIECES = 2


def _ssd_kernel(xbc_ref, dt_ref, cw_ref, cb_ref, dtb_ref, alog_ref, dsk_ref, shf_ref, sel_ref, pair_ref, y_ref,
                xc_ref, act_ref, acp_ref, e3_ref, yb_ref, st_ref, *, nch, cl):
    ncb = cl // CHUNK
    s = nch * CHUNK
    halo = SSD_HALO

    def conv_body(c, carry):
        r0 = pl.multiple_of(c * CHUNK, CHUNK)
        has_prev = jnp.logical_and(c != 0, c != ncb)
        has_next = jnp.logical_and(c != ncb - 1, c != nch - 1)
        rp = pl.multiple_of(jnp.maximum(r0 - halo, 0), halo)
        rn = pl.multiple_of(jnp.minimum(r0 + CHUNK, s - halo), halo)
        zero_halo = jnp.zeros((halo, XBC_W), BF16)
        prev = jnp.where(has_prev, xbc_ref[0, pl.ds(rp, halo), :], zero_halo)
        nxt = jnp.where(has_next, xbc_ref[0, pl.ds(rn, halo), :], zero_halo)
        cur = xbc_ref[0, pl.ds(r0, CHUNK), :]
        cat = jnp.concatenate([prev, cur, nxt, jnp.zeros((SSD_CAT - CHUNK - 2 * halo, XBC_W), BF16)], axis=0)
        sh = jnp.dot(shf_ref[...], cat, preferred_element_type=F32)
        acc = cb_ref[...] + cw_ref[CONV_K // 2:CONV_K // 2 + 1, :] * cur.astype(F32)
        for i, kk in enumerate([k for k in range(CONV_K) if k != CONV_K // 2]):
            acc = acc + cw_ref[kk:kk + 1, :] * sh[i * CHUNK:(i + 1) * CHUNK]
        xc_ref[pl.ds(r0, CHUNK), :] = _silu(acc).astype(BF16)
        dtv = jax.nn.softplus(dt_ref[0, pl.ds(r0, CHUNK), :] + dtb)
        dta = dtv * a_neg
        pre = jnp.dot(tri_f, dta, precision=HIGHEST, preferred_element_type=F32)
        fwd_col = tj < SSM_HEADS
        acum = jnp.where(fwd_col, pre, pre[CHUNK - 1:CHUNK, :] - pre + dta)
        a_end = jnp.where(fwd_col[0:1], acum[CHUNK - 1:CHUNK, :], acum[0:1, :])
        hi = acum.astype(BF16)
        lo = (acum - hi.astype(F32)).astype(BF16)
        acp_ref[pl.ds(r0, CHUNK), :] = jnp.concatenate([hi, lo], axis=1)
        act_ref[pl.ds(r0, CHUNK), :] = (hi.astype(F32) + lo.astype(F32)).T
        e3_ref[pl.ds(r0, CHUNK), :] = jnp.concatenate(
            [jnp.exp(acum), jnp.exp(a_end - acum) * dtv, dtv], axis=1).astype(BF16)
        return carry

    ti = lax.broadcasted_iota(I32, (CHUNK, CHUNK), 0)
    tj = lax.broadcasted_iota(I32, (CHUNK, CHUNK), 1)
    low_half = tj < LANES // 2
    tri_f = (ti >= tj).astype(F32)
    a_neg = -jnp.exp(alog_ref[...])
    dtb = dtb_ref[...]
    lax.fori_loop(0, nch, conv_body, 0)

    def chunk_update(c, direction, out_ref, out_lead):
        tri = (ti >= tj) if direction == 0 else (tj >= ti)
        last = CHUNK - 1 if direction == 0 else 0
        r0 = pl.multiple_of(c * CHUNK, CHUNK)
        xcv = xc_ref[pl.ds(r0, CHUNK), :]
        acum_t = act_ref[pl.ds(r0, CHUNK), :]
        acb = jnp.dot(acp_ref[pl.ds(r0, CHUNK), :], sel_ref[direction], preferred_element_type=F32)
        e3 = e3_ref[pl.ds(r0, CHUNK), :]
        prs = jnp.dot(jnp.concatenate([e3[:, i * LANES:(i + 1) * LANES] for i in range(3)], axis=0),
                      pair_ref[direction], preferred_element_type=F32)
        outs = []
        for g in range(2):
            xg = xcv[:, g * LANES:(g + 1) * LANES].astype(F32)
            bg = xcv[:, SSM_W + g * LANES:SSM_W + (g + 1) * LANES]
            cg = xcv[:, SSM_W + BC_W + g * LANES:SSM_W + BC_W + (g + 1) * LANES]
            gram = lax.dot_general(cg, bg, (((1,), (1,)), ((), ())), preferred_element_type=F32)
            eac_p = prs[0:CHUNK, g * LANES:(g + 1) * LANES]
            wend_p = prs[CHUNK:2 * CHUNK, g * LANES:(g + 1) * LANES]
            dt_p = prs[2 * CHUNK:3 * CHUNK, g * LANES:(g + 1) * LANES]
            mixes, dec = [], []
            for hh in range(2):
                h = 2 * g + hh
                col = direction * SSM_HEADS + h
                seg = jnp.exp(jnp.where(tri, acb[:, h * LANES:(h + 1) * LANES] - acum_t[col:col + 1, :], -jnp.inf))
                mixes.append((gram * seg).astype(BF16))
                dec.append(jnp.exp(acum_t[col:col + 1, last:last + 1]))
            xdt = (xg * dt_p).astype(BF16)
            y_intra = jnp.where(low_half, jnp.dot(mixes[0], xdt, preferred_element_type=F32),
                                jnp.dot(mixes[1], xdt, preferred_element_type=F32))
            stg = st_ref[direction, :, g * LANES:(g + 1) * LANES]
            y_inter = jnp.dot(cg, stg.astype(BF16), preferred_element_type=F32) * eac_p
            xw = (xg * wend_p).astype(BF16)
            contrib = lax.dot_general(bg, xw, (((0,), (0,)), ((), ())), preferred_element_type=F32)
            st_ref[direction, :, g * LANES:(g + 1) * LANES] = (
                stg * jnp.where(low_half[0:1], dec[0], dec[1]) + contrib)
            yg = y_intra + y_inter
            if direction == 0:
                yg = yg + dsk_ref[:, g * LANES:(g + 1) * LANES] * xg
            outs.append(yg)
        out_ref[out_lead + (pl.ds(r0, CHUNK), slice(None))] = jnp.concatenate(outs, axis=1)

    st_ref[...] = jnp.zeros_like(st_ref)

    def scan_body(it, carry):
        for u in range(SSD_UNROLL):
            step = it * SSD_UNROLL + u
            chunk_update(step, 0, y_ref, (0,))
            cb = jnp.where(step < ncb, ncb - 1 - step, nch - 1 - (step - ncb))
            chunk_update(cb, 1, yb_ref, ())
        return carry

    lax.fori_loop(0, nch // SSD_UNROLL, scan_body, 0)
    y_ref[0] = y_ref[0] + yb_ref[...]


def _ssd_constants():
    taps = [k for k in range(CONV_K) if k != CONV_K // 2]
    t = np.arange(CHUNK)[:, None]
    r = np.arange(SSD_CAT)[None, :]
    shf = np.concatenate([r == t + SSD_HALO - CONV_K // 2 + k for k in taps], axis=0)
    c = np.arange(LANES)[:, None]
    lane = np.arange(LANES)[None, :]
    sel = np.zeros((2, AC_PIECES * LANES, SSM_HEADS * LANES), bool)
    pair = np.zeros((2, LANES, 2 * LANES), bool)
    for d in range(2):
        for h in range(SSM_HEADS):
            for piece in range(AC_PIECES):
                sel[d, piece * LANES:(piece + 1) * LANES, h * LANES:(h + 1) * LANES] = c == d * SSM_HEADS + h
        for g in range(2):
            pair[d, :, g * LANES:(g + 1) * LANES] = c == d * SSM_HEADS + 2 * g + (lane >= LANES // 2)
    return jnp.asarray(shf, BF16), jnp.asarray(sel, BF16), jnp.asarray(pair, BF16)


def _ssd(xbc, dt, cw, cb, dtb, alog, dsk, consts, *, cl):
    b, s, _ = xbc.shape
    full = lambda a: pl.BlockSpec(a.shape, lambda bi: (0,) * a.ndim)
    return pl.pallas_call(
        functools.partial(_ssd_kernel, nch=s // CHUNK, cl=cl),
        grid=(b,),
        in_specs=[pl.BlockSpec((1, s, XBC_W), lambda bi: (bi, 0, 0)), pl.BlockSpec((1, s, DT_PAD), lambda bi: (bi, 0, 0)),
                  full(cw), full(cb), full(dtb), full(alog), full(dsk)] + [full(a) for a in consts],
        out_specs=pl.BlockSpec((1, s, SSM_W), lambda bi: (bi, 0, 0)),
        out_shape=jax.ShapeDtypeStruct((b, s, SSM_W), F32),
        scratch_shapes=[pltpu.VMEM((s, XBC_W), BF16), pltpu.VMEM((s, DT_PAD), F32),
                        pltpu.VMEM((s, AC_PIECES * LANES), BF16),
                        pltpu.VMEM((s, 3 * LANES), BF16), pltpu.VMEM((s, SSM_W), F32),
                        pltpu.VMEM((2, SSM_STATE, SSM_W), F32)],
        compiler_params=_cparams(("parallel",)),
        name="ssd",
    )(xbc, dt, cw, cb, dtb, alog, dsk, *consts)


TOK_ROWS = D_MODEL // LANES


def _load_tokens(ref, lead, n):
    return jnp.concatenate([ref[lead + (pl.ds(j, n, stride=TOK_ROWS), slice(None))] for j in range(TOK_ROWS)], axis=1)


def _store_tokens(ref, lead, val):
    for j in range(TOK_ROWS):
        ref[lead + (pl.ds(j, val.shape[0], stride=TOK_ROWS), slice(None))] = val[:, j * LANES:(j + 1) * LANES]


def _outproj_kernel(x_ref, mod_ref, gm_ref, att_ref, y_ref, z_ref, ng_ref, g2_ref, w_ref, xo_ref, hf_ref, *, tr, cl,
                    token_tiles):
    gated = y_ref[0] * _silu(z_ref[0].astype(F32))
    ssm = gated * lax.rsqrt(jnp.mean(gated * gated, axis=-1, keepdims=True) + EPS) * ng_ref[...]
    mix = jnp.concatenate([gm_ref[0], att_ref[0], ssm.astype(BF16)], axis=1)
    gate1, shift2, scale2 = _mod_rows(mod_ref, pl.program_id(1) * tr, tr, cl, (2, 3, 4))
    x = x_ref[0] + gate1 * jnp.dot(mix, w_ref[...], preferred_element_type=F32)
    xo_ref[0] = x
    ms = jnp.mean(x * x, axis=-1, keepdims=True)
    hf = x * lax.rsqrt(ms + EPS) * g2_ref[...] * (1.0 + scale2) + shift2
    if token_tiles:
        _store_tokens(hf_ref, (0,), hf)
    else:
        hf_ref[0] = hf.astype(hf_ref.dtype)


def _out_projection(xs, mod, gm, att, y, z, ng, g2, w_out, *, cl, nt, token_tiles):
    b, s, _ = xs.shape
    tr = s // nt
    tok = lambda w: pl.BlockSpec((1, tr, w), lambda bi, j: (bi, j, 0))
    full = lambda a: pl.BlockSpec(a.shape, lambda bi, j: (0,) * a.ndim)
    if token_tiles:
        hf_spec = pl.BlockSpec((1, tr * TOK_ROWS, LANES), lambda bi, j: (bi, j, 0))
        hf_shape = jax.ShapeDtypeStruct((b, s * TOK_ROWS, LANES), F32)
    else:
        hf_spec, hf_shape = tok(D_MODEL), jax.ShapeDtypeStruct((b, s, D_MODEL), BF16)
    return pl.pallas_call(
        functools.partial(_outproj_kernel, tr=tr, cl=cl, token_tiles=token_tiles),
        grid=(b, nt),
        in_specs=[tok(D_MODEL), pl.BlockSpec((1, 16, D_MODEL), lambda bi, j: (bi, 0, 0)), tok(GM_W), tok(ATT_W),
                  tok(SSM_W), tok(SSM_W), full(ng), full(g2), full(w_out)],
        out_specs=[tok(D_MODEL), hf_spec],
        out_shape=[jax.ShapeDtypeStruct((b, s, D_MODEL), F32), hf_shape],
        compiler_params=_cparams(("parallel", "parallel")),
        name="out_projection",
    )(xs, mod, gm, att, y, z, ng, g2, w_out)


def _swiglu_rows(h, wg_ref, wu_ref, wd_ref, act_ref):
    for f in range(D_FF // FF_BLOCK):
        cols = slice(f * FF_BLOCK, (f + 1) * FF_BLOCK)
        g = jnp.dot(h, wg_ref[0, :, cols], preferred_element_type=F32)
        u = jnp.dot(h, wu_ref[0, :, cols], preferred_element_type=F32)
        act_ref[:, cols] = (_silu(g) * u).astype(BF16)
    return jnp.dot(act_ref[...], wd_ref[0], preferred_element_type=F32)


def _ffn_kernel(x_ref, hf_ref, mod_ref, wg_ref, wu_ref, wd_ref, xo_ref, act_ref, *, tr, cl):
    (gate2,) = _mod_rows(mod_ref, pl.program_id(1) * tr, tr, cl, (5,))
    xo_ref[0] = x_ref[0] + gate2 * _swiglu_rows(hf_ref[0], wg_ref, wu_ref, wd_ref, act_ref)


def _dense_ffn(xs, hf, mod, wg, wu, wd, *, cl, nt):
    b, s, _ = xs.shape
    tr = s // nt
    tok = pl.BlockSpec((1, tr, D_MODEL), lambda bi, j: (bi, j, 0))
    full = lambda a: pl.BlockSpec(a.shape, lambda bi, j: (0,) * a.ndim)
    return pl.pallas_call(
        functools.partial(_ffn_kernel, tr=tr, cl=cl),
        grid=(b, nt),
        in_specs=[tok, tok, pl.BlockSpec((1, 16, D_MODEL), lambda bi, j: (bi, 0, 0)), full(wg), full(wu), full(wd)],
        out_specs=tok,
        out_shape=jax.ShapeDtypeStruct((b, s, D_MODEL), F32),
        scratch_shapes=[pltpu.VMEM((tr, D_FF), BF16)],
        compiler_params=_cparams(("parallel", "parallel")),
        name="dense_ffn",
    )(xs, hf, mod, wg, wu, wd)


def _router_kernel(hf_ref, wr_ref, tri_ref, mi_ref, mf_ref, cnt_ref, run_ref, *, trc):
    @pl.when(pl.program_id(0) == 0)
    def _():
        run_ref[...] = jnp.zeros_like(run_ref)

    logits = lax.dot_general(wr_ref[...], _load_tokens(hf_ref, (), trc), (((1,), (1,)), ((), ())), precision=HIGHEST,
                             preferred_element_type=F32)
    eidx = lax.broadcasted_iota(I32, (N_EXPERTS, trc), 0).astype(F32)
    m1 = jnp.max(logits, axis=0, keepdims=True)
    i1 = jnp.min(jnp.where(logits == m1, eidx, float(N_EXPERTS)), axis=0, keepdims=True)
    rest = jnp.where(eidx == i1, -jnp.inf, logits)
    m2 = jnp.max(rest, axis=0, keepdims=True)
    i2 = jnp.min(jnp.where(rest == m2, eidx, float(N_EXPERTS)), axis=0, keepdims=True)
    e2 = jnp.exp(m2 - m1)
    g1 = 1.0 / (1.0 + e2)
    g2 = e2 / (1.0 + e2)
    oh1 = (eidx == i1).astype(F32)
    oh2 = (eidx == i2).astype(F32)
    sel = oh1 + oh2
    before = jnp.dot(sel.astype(BF16), tri_ref[...], preferred_element_type=F32) + run_ref[:, 0:1]
    r1 = jnp.sum(oh1 * before, axis=0, keepdims=True)
    r2 = jnp.sum(oh2 * before, axis=0, keepdims=True)
    run_ref[...] = run_ref[...] + jnp.sum(sel, axis=1, keepdims=True)
    zi = jnp.zeros((4, trc), I32)
    mi_ref[0] = jnp.concatenate([i1.astype(I32), i2.astype(I32), r1.astype(I32), r2.astype(I32), zi], axis=0)
    mf_ref[0] = jnp.concatenate([g1, g2, jnp.zeros((6, trc), F32)], axis=0)
    cnt_ref[...] = run_ref[...].astype(I32)


def _router(hf_tiles, wr_t, tri, *, trc):
    t = hf_tiles.shape[0] // TOK_ROWS
    nt = t // trc
    meta = pl.BlockSpec((1, 8, trc), lambda i: (i, 0, 0))
    return pl.pallas_call(
        functools.partial(_router_kernel, trc=trc),
        grid=(nt,),
        in_specs=[pl.BlockSpec((trc * TOK_ROWS, LANES), lambda i: (i, 0)), pl.BlockSpec(wr_t.shape, lambda i: (0, 0)),
                  pl.BlockSpec(tri.shape, lambda i: (0, 0))],
        out_specs=[meta, meta, pl.BlockSpec((N_EXPERTS, LANES), lambda i: (0, 0))],
        out_shape=[jax.ShapeDtypeStruct((nt, 8, trc), I32), jax.ShapeDtypeStruct((nt, 8, trc), F32),
                   jax.ShapeDtypeStruct((N_EXPERTS, LANES), I32)],
        scratch_shapes=[pltpu.VMEM((N_EXPERTS, LANES), F32)],
        compiler_params=_cparams(("arbitrary",)),
        name="moe_router",
    )(hf_tiles, wr_t, tri)


DMA_UNROLL = 8


def _token_copy(src_ref, src_tok, dst_ref, dst_tok, sem):
    src = src_ref.at[pl.ds(pl.multiple_of(src_tok * TOK_ROWS, TOK_ROWS), TOK_ROWS), :]
    dst = dst_ref.at[pl.ds(pl.multiple_of(dst_tok * TOK_ROWS, TOK_ROWS), TOK_ROWS), :]
    return pltpu.make_async_copy(src, dst, sem)


def _for_tokens(n, fn):
    def block(i, carry):
        for u in range(DMA_UNROLL):
            fn(i * DMA_UNROLL + u)
        return carry

    lax.fori_loop(0, n // DMA_UNROLL, block, 0)


def _dispatch_kernel(pad_lo_ref, pad_n_ref, nact_ref, slot_ref, hf_ref, xs_ref, zero_ref, sem, zsem, *, trc, ntile):
    @pl.when(pl.program_id(0) == 0)
    def _():
        zero_ref[...] = jnp.zeros_like(zero_ref)
        tile_rows = EXPERT_TILE * TOK_ROWS

        def tile_copy(i):
            return pltpu.make_async_copy(
                zero_ref, xs_ref.at[pl.ds(pl.multiple_of(i * tile_rows, tile_rows), tile_rows), :], zsem)

        def fill_tile(i, carry):
            tile_copy(i).start()
            return carry

        def drain_tile(i, carry):
            tile_copy(i).wait()
            return carry

        lax.fori_loop(nact_ref[0], ntile, fill_tile, 0)
        lax.fori_loop(nact_ref[0], ntile, drain_tile, 0)
        for e in range(N_EXPERTS):
            lo = pad_lo_ref[e]
            n = pad_n_ref[e]

            def fill(r, carry):
                _token_copy(zero_ref, 0, xs_ref, lo + r, zsem).start()
                return carry

            def drain(r, carry):
                _token_copy(zero_ref, 0, xs_ref, lo + r, zsem).wait()
                return carry

            lax.fori_loop(0, n, fill, 0)
            lax.fori_loop(0, n, drain, 0)

    copies = lambda r: [_token_copy(hf_ref, r, xs_ref, slot_ref[0, k, r], sem) for k in range(2)]
    _for_tokens(trc, lambda r: [cp.start() for cp in copies(r)])
    _for_tokens(trc, lambda r: [cp.wait() for cp in copies(r)])


def _dispatch(pad_lo, pad_n, nact, slots, hf_tiles, nslot, *, trc):
    t = hf_tiles.shape[0] // TOK_ROWS
    nt = t // trc
    grid_spec = pltpu.PrefetchScalarGridSpec(
        num_scalar_prefetch=3,
        grid=(nt,),
        in_specs=[pl.BlockSpec((1, 2, trc), lambda i, lo, n, na: (i, 0, 0), memory_space=pltpu.SMEM),
                  pl.BlockSpec((trc * TOK_ROWS, LANES), lambda i, lo, n, na: (i, 0))],
        out_specs=pl.BlockSpec(memory_space=pl.ANY),
        scratch_shapes=[pltpu.VMEM((EXPERT_TILE * TOK_ROWS, LANES), F32), pltpu.SemaphoreType.DMA(()),
                        pltpu.SemaphoreType.DMA(())],
    )
    return pl.pallas_call(
        functools.partial(_dispatch_kernel, trc=trc, ntile=nslot // EXPERT_TILE),
        grid_spec=grid_spec,
        out_shape=jax.ShapeDtypeStruct((nslot * TOK_ROWS, LANES), F32),
        compiler_params=_cparams(("arbitrary",)),
        name="moe_dispatch",
    )(pad_lo, pad_n, nact, slots, hf_tiles)


def _expert_kernel(texp_ref, nact_ref, xs_ref, wg_ref, wu_ref, wd_ref, ys_ref, act_ref):
    active = pl.program_id(0) < nact_ref[0]

    @pl.when(active)
    def _():
        h = _load_tokens(xs_ref, (), EXPERT_TILE).astype(BF16)
        _store_tokens(ys_ref, (), _swiglu_rows(h, wg_ref, wu_ref, wd_ref, act_ref))

    @pl.when(jnp.logical_not(active))
    def _():
        ys_ref[...] = jnp.zeros_like(ys_ref)


def _expert_ffn(tile_expert, nact, xs, wg, wu, wd):
    ntile = xs.shape[0] // (EXPERT_TILE * TOK_ROWS)
    wspec = lambda shp: pl.BlockSpec((1,) + shp, lambda i, te, na: (te[i], 0, 0))
    tile = pl.BlockSpec((EXPERT_TILE * TOK_ROWS, LANES), lambda i, te, na: (i, 0))
    grid_spec = pltpu.PrefetchScalarGridSpec(
        num_scalar_prefetch=2,
        grid=(ntile,),
        in_specs=[tile, wspec((D_MODEL, D_FF)), wspec((D_MODEL, D_FF)), wspec((D_FF, D_MODEL))],
        out_specs=tile,
        scratch_shapes=[pltpu.VMEM((EXPERT_TILE, D_FF), BF16)],
    )
    return pl.pallas_call(
        _expert_kernel,
        grid_spec=grid_spec,
        out_shape=jax.ShapeDtypeStruct(xs.shape, F32),
        compiler_params=_cparams(("arbitrary",)),
        name="moe_experts",
    )(tile_expert, nact, xs, wg, wu, wd)


def _combine_kernel(slot_ref, x_ref, gate_ref, mod_ref, ys_ref, xo_ref, buf_ref, sem, *, trc, nt, cl):
    copies = lambda r: [_token_copy(ys_ref, slot_ref[0, k, r], buf_ref.at[k], r, sem) for k in range(2)]
    _for_tokens(trc, lambda r: [cp.start() for cp in copies(r)])
    _for_tokens(trc, lambda r: [cp.wait() for cp in copies(r)])
    (gate2,) = _mod_rows(mod_ref, (pl.program_id(0) % nt) * trc, trc, cl, (5,))
    gt = gate_ref[0]
    f = gt[:, 0:1] * _load_tokens(buf_ref, (0,), trc) + gt[:, 1:2] * _load_tokens(buf_ref, (1,), trc)
    xo_ref[...] = x_ref[...] + gate2 * f


def _combine(slots, x_flat, gates_t, mod, ys, *, trc, nt, cl):
    t = x_flat.shape[0]
    ntile = t // trc
    tok = pl.BlockSpec((trc, D_MODEL), lambda i: (i, 0))
    return pl.pallas_call(
        functools.partial(_combine_kernel, trc=trc, nt=nt, cl=cl),
        grid=(ntile,),
        in_specs=[pl.BlockSpec((1, 2, trc), lambda i: (i, 0, 0), memory_space=pltpu.SMEM), tok,
                  pl.BlockSpec((1, trc, 2), lambda i: (i, 0, 0)),
                  pl.BlockSpec((1, 16, D_MODEL), lambda i: (i // nt, 0, 0)), pl.BlockSpec(memory_space=pl.ANY)],
        out_specs=tok,
        out_shape=jax.ShapeDtypeStruct((t, D_MODEL), F32),
        scratch_shapes=[pltpu.VMEM((2, trc * TOK_ROWS, LANES), F32), pltpu.SemaphoreType.DMA(())],
        compiler_params=_cparams(("arbitrary",)),
        name="moe_combine",
    )(slots, x_flat, gates_t, mod, ys)


def _moe_ffn(xs, hf, mod, wr_t, tri, wg, wu, wd, *, cl, nt):
    b, s, _ = xs.shape
    t = b * s
    trr = tri.shape[0]
    trc = s // nt
    hf_flat = hf.reshape(t * TOK_ROWS, LANES)
    meta_i, meta_f, counts = _router(hf_flat, wr_t, tri, trc=trr)
    counts = counts[:, 0]
    padded = (counts + EXPERT_TILE - 1) // EXPERT_TILE * EXPERT_TILE
    ends = jnp.cumsum(padded)
    starts = ends - padded
    eid = meta_i[:, 0:2, :]
    group_start = sum(jnp.where(eid == e, starts[e], 0) for e in range(N_EXPERTS))
    slots = group_start + meta_i[:, 2:4, :]
    nslot = 2 * t + N_EXPERTS * EXPERT_TILE
    ntile = nslot // EXPERT_TILE
    nact = (ends[-1] // EXPERT_TILE).astype(I32)
    tile_lo = jnp.minimum(jnp.arange(ntile, dtype=I32), nact - 1) * EXPERT_TILE
    tile_expert = jnp.minimum(jnp.sum(tile_lo[:, None] >= ends[None, :], axis=1), N_EXPERTS - 1).astype(I32)
    nact = nact.reshape(1)
    xs_sorted = _dispatch((starts + counts).astype(I32), (padded - counts).astype(I32), nact, slots, hf_flat, nslot,
                          trc=trr)
    ys = _expert_ffn(tile_expert, nact, xs_sorted, wg, wu, wd)
    retile = lambda a: jnp.swapaxes(jnp.swapaxes(a, 0, 1).reshape(2, t // trc, trc), 0, 1)
    gates_t = jnp.swapaxes(retile(meta_f[:, 0:2, :]), 1, 2)
    out = _combine(retile(slots), xs.reshape(t, D_MODEL), gates_t, mod, ys, trc=trc, nt=nt, cl=cl)
    return out.reshape(b, s, D_MODEL)


def _slab_perm(t, lead):
    nh = t.shape[-1] // HEAD_DIM
    per = nh // 2
    t = t.reshape(lead + (2, per, 2, 2, 16))
    t = jnp.moveaxis(t, (-5, -4, -3, -2, -1), (-3, -5, -2, -4, -1))
    return t.reshape(lead + (nh * HEAD_DIM,))


def _rope_tables(s, cl):
    l = s - cl
    pos = np.arange(l)
    lane = np.arange(LANES)
    i = lane % 32
    freq = ROPE_BASE ** (-(i % 16).astype(np.float32) / 16.0)
    p = np.where(i[None, :] < 16, (pos // GRID_W)[:, None], (pos % GRID_W)[:, None]).astype(np.float32)
    ang = p * freq[None, :].astype(np.float32)
    sign = np.where(lane < LANES // 2, -1.0, 1.0).astype(np.float32)
    cos = np.concatenate([np.ones((cl, LANES), np.float32), np.cos(ang)], axis=0)
    sin = np.concatenate([np.zeros((cl, LANES), np.float32), np.sin(ang) * sign[None, :]], axis=0)
    return jnp.asarray(cos, F32), jnp.asarray(sin, F32)


def kernel(x, c, ctx, c_ctx, w_mod, b_mod, norm1_g, norm2_g, w_in, w_out, gm_v_g, gm_ws, gm_bs, att_q_g, att_k_g, att_sink, ssm_conv_w, ssm_conv_b, ssm_dt_bias, ssm_a_log, ssm_d, ssm_norm_g, ffn_w_gate, ffn_w_up, ffn_w_down, moe_router, moe_w_gate, moe_w_up, moe_w_down):
    b, l, _ = x.shape
    cl = ctx.shape[1]
    s = cl + l
    depth = w_mod.shape[0]
    nt = 8
    nt_moe = 4
    assert s % (nt * 16) == 0 and cl % CHUNK == 0 and l % CHUNK == 0 and l % GRID_W == 0 and b < 16
    assert (s // CHUNK) % SSD_UNROLL == 0 and (s // CHUNK) % GMLP_UNROLL == 0

    cvec = jnp.concatenate([c, c_ctx[None, :], jnp.zeros((16 - b - 1, D_MODEL), F32)], axis=0)
    mods = _modulation(cvec, w_mod, b_mod)
    lat = jnp.moveaxis(mods[:, :, :b, :], 2, 1)
    con = jnp.broadcast_to(mods[:, None, :, b, :], (depth, b, 6, D_MODEL))
    pad2 = jnp.zeros((depth, b, 2, D_MODEL), F32)
    modtab = jnp.concatenate([lat, pad2, con, pad2], axis=2)

    wi = w_in
    w_in_p = jnp.concatenate([
        wi[:, :, 0:2 * GM_W],
        _slab_perm(wi[:, :, 512:1024], (depth, D_MODEL)),
        _slab_perm(wi[:, :, 1024:1152], (depth, D_MODEL)),
        wi[:, :, 1152:2304],
        jnp.pad(wi[:, :, 2304:2312], ((0, 0), (0, 0), (0, DT_PAD - 2 * SSM_HEADS))),
    ], axis=2).astype(BF16)
    wo_att = w_out[:, GM_W:GM_W + ATT_W, :].reshape(depth, 2, 4, HEAD_DIM, D_MODEL)
    wo_att = jnp.swapaxes(wo_att, 1, 2).reshape(depth, ATT_W, D_MODEL)
    w_out_p = jnp.concatenate([w_out[:, :GM_W], wo_att, w_out[:, GM_W + ATT_W:]], axis=1).astype(BF16)
    gq = _slab_perm(jnp.tile(att_q_g, (1, 2)), (depth,))[:, None, :]
    gk = _slab_perm(jnp.tile(att_k_g, (1, 2)), (depth,))[:, None, :]
    lane = np.arange(LANES)
    seg = jnp.asarray(((lane[:, None] // 32) % 2 == (lane[None, :] // 32) % 2), BF16)
    lane2 = np.arange(GM_W)
    seg64 = jnp.asarray(lane2[:, None] // GM_HD == lane2[None, :] // GM_HD, BF16)
    cos, sin = _rope_tables(s, cl)
    ws_cat = jnp.swapaxes(gm_ws, 1, 2).reshape(depth, CHUNK, GM_HEADS * CHUNK).astype(BF16)
    bs_exp = jnp.repeat(jnp.swapaxes(gm_bs, 1, 2), GM_HD, axis=2)
    sink_heads = att_sink.reshape(depth, 2, 4).swapaxes(1, 2).reshape(depth, ATT_HEADS)
    sink_col = jnp.repeat(sink_heads, CHUNK, axis=1)[:, :, None] * LOG2E
    score_bound = (8.0 * (1.0 + 2.0 ** -6)) * jnp.max(jnp.abs(att_q_g), axis=1) * jnp.max(jnp.abs(att_k_g), axis=1)
    shift_col = jnp.maximum(sink_col, score_bound[:, None, None] * LOG2E)
    cw = jnp.pad(ssm_conv_w, ((0, 0), (0, 8 - CONV_K), (0, 0)))
    cb = ssm_conv_b[:, None, :]
    pad_dt = lambda t: jnp.pad(t.reshape(depth, 1, 2 * SSM_HEADS), ((0, 0), (0, 0), (0, DT_PAD - 2 * SSM_HEADS)))
    dtb = pad_dt(ssm_dt_bias)
    alog = pad_dt(ssm_a_log)
    dsk = jnp.repeat(ssm_d, SSM_HD, axis=1)[:, None, :]
    ssd_consts = _ssd_constants()
    wr_t = jnp.swapaxes(moe_router, 1, 2)
    trr = 1024 if (b * s) % 1024 == 0 else 512
    assert (b * s) % trr == 0
    tidx = np.arange(trr)
    tri = jnp.asarray(tidx[:, None] < tidx[None, :], BF16)
    ffn_g, ffn_u, ffn_d = ffn_w_gate.astype(BF16), ffn_w_up.astype(BF16), ffn_w_down.astype(BF16)
    moe_g, moe_u, moe_d = moe_w_gate.astype(BF16), moe_w_up.astype(BF16), moe_w_down.astype(BF16)

    xs = jnp.concatenate([ctx, x], axis=1)
    for i in range(depth):
        moe = i % 2 == 1
        mod = modtab[i]
        gu, gv, q, k, v, z, xbc, dt = _in_projection(
            xs, mod, norm1_g[i][None, :], w_in_p[i], cos, sin, gq[i], gk[i], gm_v_g[i][None, :], seg, seg64,
            cl=cl, nt=nt)
        gm = _gmlp(gu, gv, ws_cat[i], bs_exp[i])
        att = lax.cond(score_bound[i] <= SAFE_SCORE_BOUND,
                       functools.partial(_attention, cl=cl, fixed_shift=True),
                       functools.partial(_attention, cl=cl, fixed_shift=False),
                       q, k, v, sink_col[i], shift_col[i])
        y = _ssd(xbc, dt, cw[i], cb[i], dtb[i], alog[i], dsk[i], ssd_consts, cl=cl)
        xs, hf = _out_projection(xs, mod, gm, att, y, z, ssm_norm_g[i][None, :], norm2_g[i][None, :], w_out_p[i],
                                 cl=cl, nt=nt, token_tiles=moe)
        j = i // 2
        if moe:
            xs = _moe_ffn(xs, hf, mod, wr_t[j], tri, moe_g[j], moe_u[j], moe_d[j], cl=cl, nt=nt_moe)
        else:
            xs = _dense_ffn(xs, hf, mod, ffn_g[j:j + 1], ffn_u[j:j + 1], ffn_d[j:j + 1], cl=cl, nt=nt)
    return xs[:, cl:, :]
```

```python
import functools
import math

import numpy as np
import jax
import jax.numpy as jnp
from jax import lax
from jax.experimental import pallas as pl
from jax.experimental.pallas import tpu as pltpu

F32 = jnp.float32
BF16 = jnp.bfloat16
I32 = jnp.int32
HIGHEST = lax.Precision.HIGHEST

D_MODEL = 1024
CHUNK = 128
GM_HEADS, GM_HD, GM_W = 4, 64, 256
ATT_HEADS, KV_HEADS, HEAD_DIM = 8, 2, 64
ATT_W, KV_W = 512, 128
GRID_W = 64
ROPE_BASE = 10000.0
SSM_HEADS, SSM_HD, SSM_W = 4, 64, 256
SSM_STATE, BC_W, CONV_K, XBC_W = 128, 256, 5, 768
D_FF = 2816
N_EXPERTS = 8
EPS = 1e-6
LANES = 128
DT_PAD = 128
C_GU, C_GV, C_Q, C_K, C_V, C_Z, C_XBC, C_DT, IN_WP = 0, 256, 512, 1024, 1152, 1280, 1536, 2304, 2432
FF_BLOCK = 256
EXPERT_TILE = 512
VMEM_LIMIT = 56 * 1024 * 1024
LOG2E = math.log2(math.e)
SAFE_SCORE_BOUND = 40.0


def _cparams(sem):
    return pltpu.CompilerParams(dimension_semantics=sem, vmem_limit_bytes=VMEM_LIMIT)


def _silu(x):
    return x * jax.nn.sigmoid(x)


def _mod_kernel(c_ref, w_ref, b_ref, o_ref):
    cs = _silu(c_ref[...])
    o_ref[0, 0] = jnp.dot(cs, w_ref[0], precision=HIGHEST, preferred_element_type=F32) + b_ref[0, 0]


def _modulation(cvec, w_mod, b_mod):
    depth = w_mod.shape[0]
    r = cvec.shape[0]
    return pl.pallas_call(
        _mod_kernel,
        grid=(depth, 6),
        in_specs=[
            pl.BlockSpec((r, D_MODEL), lambda i, n: (0, 0)),
            pl.BlockSpec((1, D_MODEL, D_MODEL), lambda i, n: (i, 0, n)),
            pl.BlockSpec((1, 1, 1, D_MODEL), lambda i, n: (i, n, 0, 0)),
        ],
        out_specs=pl.BlockSpec((1, 1, r, D_MODEL), lambda i, n: (i, n, 0, 0)),
        out_shape=jax.ShapeDtypeStruct((depth, 6, r, D_MODEL), F32),
        compiler_params=_cparams(("arbitrary", "arbitrary")),
        name="modulation",
    )(cvec, w_mod, b_mod.reshape(depth, 6, 1, D_MODEL))


def _mod_rows(mod_ref, tile_row0, rows, cl, lo):
    ridx = tile_row0 + lax.broadcasted_iota(I32, (rows, 1), 0)
    is_ctx = ridx < cl
    m = mod_ref[0]
    return [jnp.where(is_ctx, m[8 + k:9 + k], m[k:k + 1]) for k in lo]


def _inproj_kernel(x_ref, mod_ref, g1_ref, w_ref, cos_ref, sin_ref, gq_ref, gk_ref, gvg_ref, seg_ref, seg64_ref,
                   gu_ref, gv_ref, q_ref, k_ref, v_ref, z_ref, xbc_ref, dt_ref, *, tr, cl):
    x = x_ref[0]
    ms = jnp.mean(x * x, axis=-1, keepdims=True)
    y = x * lax.rsqrt(ms + EPS) * g1_ref[...]
    shift, scale = _mod_rows(mod_ref, pl.program_id(1) * tr, tr, cl, (0, 1))
    h = (y * (1.0 + scale) + shift).astype(BF16)

    full = jnp.dot(h, w_ref[...], preferred_element_type=F32)

    def proj(a, b):
        return full[:, a:b]

    guv = proj(C_GU, C_Q)
    gu_ref[0] = jax.nn.gelu(guv[:, :GM_W]).astype(BF16)
    gv = jax.nn.gelu(guv[:, GM_W:])
    ssv = jnp.dot((gv * gv).astype(BF16), seg64_ref[...], preferred_element_type=F32)
    gv_ref[0] = (gv * lax.rsqrt(ssv * (1.0 / GM_HD) + EPS) * gvg_ref[...]).astype(BF16)

    cos = cos_ref[...]
    sin = sin_ref[...]

    def norm_rope(t, gain):
        ss = jnp.dot((t * t).astype(BF16), seg_ref[...], preferred_element_type=F32)
        tn = t * lax.rsqrt(ss * (1.0 / HEAD_DIM) + EPS) * gain
        return tn * cos + pltpu.roll(tn, LANES // 2, axis=1) * sin

    qkv = proj(C_Q, C_Z)
    gq = gq_ref[...] * (HEAD_DIM ** -0.5 * LOG2E)
    for c in range(ATT_W // LANES):
        q_ref[0, :, c * LANES:(c + 1) * LANES] = norm_rope(qkv[:, c * LANES:(c + 1) * LANES], gq).astype(BF16)
    k_ref[0] = norm_rope(qkv[:, ATT_W:ATT_W + KV_W], gk_ref[...]).astype(BF16)
    v_ref[0] = qkv[:, ATT_W + KV_W:].astype(BF16)
    z_ref[0] = proj(C_Z, C_XBC).astype(BF16)
    xbc_ref[0] = proj(C_XBC, C_DT).astype(BF16)
    dt_ref[0] = proj(C_DT, IN_WP)


def _in_projection(xs, mod, g1, w_in, cos, sin, gq, gk, gvg, seg, seg64, *, cl, nt):
    b, s, _ = xs.shape
    tr = s // nt
    tok = lambda w: pl.BlockSpec((1, tr, w), lambda bi, j: (bi, j, 0))
    full = lambda a: pl.BlockSpec(a.shape, lambda bi, j: (0,) * a.ndim)
    widths = (GM_W, GM_W, ATT_W, KV_W, KV_W, SSM_W, XBC_W)
    return pl.pallas_call(
        functools.partial(_inproj_kernel, tr=tr, cl=cl),
        grid=(b, nt),
        in_specs=[tok(D_MODEL), pl.BlockSpec((1, 16, D_MODEL), lambda bi, j: (bi, 0, 0)), full(g1), full(w_in),
                  pl.BlockSpec((tr, LANES), lambda bi, j: (j, 0)), pl.BlockSpec((tr, LANES), lambda bi, j: (j, 0)),
                  full(gq), full(gk), full(gvg), full(seg), full(seg64)],
        out_specs=[tok(w) for w in widths] + [tok(DT_PAD)],
        out_shape=[jax.ShapeDtypeStruct((b, s, w), BF16) for w in widths]
        + [jax.ShapeDtypeStruct((b, s, DT_PAD), F32)],
        compiler_params=_cparams(("parallel", "parallel")),
        name="in_projection",
    )(xs, mod, g1, w_in, cos, sin, gq, gk, gvg, seg, seg64)


GMLP_UNROLL = 2


def _gmlp_kernel(gu_ref, gv_ref, ws_ref, bs_ref, o_ref, *, nch):
    head = lax.broadcasted_iota(I32, (CHUNK, GM_W), 1) >> 6

    def body(it, carry):
        for u in range(GMLP_UNROLL):
            r0 = pl.multiple_of((it * GMLP_UNROLL + u) * CHUNK, CHUNK)
            v = gv_ref[0, pl.ds(r0, CHUNK), :]
            vbd = jnp.concatenate([jnp.where(head == hh, v, jnp.zeros_like(v)) for hh in range(GM_HEADS)], axis=0)
            sp = jnp.dot(ws_ref[...], vbd, preferred_element_type=F32) + bs_ref[...]
            o_ref[0, pl.ds(r0, CHUNK), :] = (gu_ref[0, pl.ds(r0, CHUNK), :].astype(F32) * sp).astype(BF16)
        return carry

    lax.fori_loop(0, nch // GMLP_UNROLL, body, 0)


def _gmlp(gu, gv, ws_cat, bs_exp):
    b, s, _ = gu.shape
    seq = pl.BlockSpec((1, s, GM_W), lambda bi: (bi, 0, 0))
    return pl.pallas_call(
        functools.partial(_gmlp_kernel, nch=s // CHUNK),
        grid=(b,),
        in_specs=[seq, seq, pl.BlockSpec(ws_cat.shape, lambda bi: (0, 0)), pl.BlockSpec(bs_exp.shape, lambda bi: (0, 0))],
        out_specs=seq,
        out_shape=jax.ShapeDtypeStruct((b, s, GM_W), BF16),
        compiler_params=_cparams(("parallel",)),
        name="gmlp",
    )(gu, gv, ws_cat, bs_exp)


def _attn_kernel(q_ref, k_ref, v_ref, sink_ref, shift_ref, o_ref, *, nch, cl, fixed_shift):
    ncb = cl // CHUNK
    nslab = ATT_W // LANES
    lane = lax.broadcasted_iota(I32, (CHUNK, LANES), 1)
    first_head = ((lane >> 5) & 1) == 0
    low_half = lane < LANES // 2
    kc = k_ref[0, 0:cl, :]
    vc = v_ref[0, 0:cl, :]
    qi = lax.broadcasted_iota(I32, (CHUNK, CHUNK), 0)
    kl = lax.broadcasted_iota(I32, (CHUNK, CHUNK), 1)
    ninf = jnp.full((CHUNK, CHUNK), -jnp.inf, F32)
    zeros = jnp.zeros((CHUNK, CHUNK), F32)
    bias_prev = jnp.where(kl >= qi, zeros, ninf)
    bias_next = jnp.where(kl <= qi, zeros, ninf)

    def body(n, carry):
        r0 = pl.multiple_of(n * CHUNK, CHUNK)
        rp = pl.multiple_of(jnp.maximum(n - 1, ncb) * CHUNK, CHUNK)
        rn = pl.multiple_of(jnp.minimum(n + 1, nch - 1) * CHUNK, CHUNK)
        q = q_ref[0, pl.ds(r0, CHUNK), :]
        kk = jnp.concatenate([kc, k_ref[0, pl.ds(rp, CHUNK), :], k_ref[0, pl.ds(r0, CHUNK), :],
                              k_ref[0, pl.ds(rn, CHUNK), :]], axis=0)
        vv = jnp.concatenate([vc, v_ref[0, pl.ds(rp, CHUNK), :], v_ref[0, pl.ds(r0, CHUNK), :],
                              v_ref[0, pl.ds(rn, CHUNK), :]], axis=0)
        is_lat = n >= ncb
        bias = jnp.concatenate([
            jnp.where(jnp.logical_and(is_lat, n - 1 >= ncb), bias_prev, ninf),
            jnp.where(is_lat, zeros, ninf),
            jnp.where(jnp.logical_and(is_lat, n + 1 <= nch - 1), bias_next, ninf)], axis=1)
        zero = jnp.zeros((CHUNK, LANES), BF16)
        blocks = []
        for c in range(nslab):
            qc = q[:, c * LANES:(c + 1) * LANES]
            blocks += [jnp.where(first_head, qc, zero), jnp.where(first_head, zero, qc)]
        qs = jnp.concatenate(blocks, axis=0)
        sc = lax.dot_general(qs, kk, (((1,), (1,)), ((), ())), preferred_element_type=F32)
        s_ctx = sc[:, :cl]
        s_loc = sc[:, cl:] + jnp.concatenate([bias] * (2 * nslab), axis=0)
        sk = sink_ref[...]

        def lane_tiles(fn, a, b):
            tiles = [a[:, t * LANES:(t + 1) * LANES] for t in range(a.shape[1] // LANES)]
            tiles += [b[:, t * LANES:(t + 1) * LANES] for t in range(b.shape[1] // LANES)]
            return functools.reduce(fn, tiles)

        if fixed_shift:
            m = shift_ref[...]
        else:
            m = jnp.maximum(jnp.max(lane_tiles(jnp.maximum, s_ctx, s_loc), axis=-1, keepdims=True), sk)
        e_ctx = jnp.exp2(s_ctx - m)
        e_loc = jnp.exp2(s_loc - m)
        p = jnp.concatenate([e_ctx.astype(BF16), e_loc.astype(BF16)], axis=1)
        pv = jnp.dot(p, jnp.concatenate([vv, jnp.ones_like(vv)], axis=1), preferred_element_type=F32)
        o = pv[:, :LANES] * (1.0 / (pv[:, LANES:] + jnp.exp2(sk - m)))
        for c in range(nslab):
            oa = o[(2 * c) * CHUNK:(2 * c + 1) * CHUNK]
            ob = o[(2 * c + 1) * CHUNK:(2 * c + 2) * CHUNK]
            o_ref[0, pl.ds(r0, CHUNK), c * LANES:(c + 1) * LANES] = jnp.where(low_half, oa, ob).astype(BF16)
        return carry

    lax.fori_loop(0, nch, body, 0)


def _attention(q, k, v, sink_col, shift_col, *, cl, fixed_shift):
    b, s, _ = q.shape
    seq = lambda w: pl.BlockSpec((1, s, w), lambda bi: (bi, 0, 0))
    col = pl.BlockSpec(sink_col.shape, lambda bi: (0, 0))
    return pl.pallas_call(
        functools.partial(_attn_kernel, nch=s // CHUNK, cl=cl, fixed_shift=fixed_shift),
        grid=(b,),
        in_specs=[seq(ATT_W), seq(KV_W), seq(KV_W), col, col],
        out_specs=seq(ATT_W),
        out_shape=jax.ShapeDtypeStruct((b, s, ATT_W), BF16),
        compiler_params=_cparams(("parallel",)),
        name="attention_fixed_shift" if fixed_shift else "attention_row_max",
    )(q, k, v, sink_col, shift_col)


SSD_HALO = 16
SSD_CAT = 256
SSD_UNROLL = 2
AC_PIECES = 2


def _ssd_kernel(xbc_ref, dt_ref, cw_ref, cb_ref, dtb_ref, alog_ref, dsk_ref, shf_ref, sel_ref, pair_ref, y_ref,
                xc_ref, act_ref, acp_ref, e3_ref, yb_ref, st_ref, *, nch, cl):
    ncb = cl // CHUNK
    s = nch * CHUNK
    halo = SSD_HALO

    def conv_body(c, carry):
        r0 = pl.multiple_of(c * CHUNK, CHUNK)
        has_prev = jnp.logical_and(c != 0, c != ncb)
        has_next = jnp.logical_and(c != ncb - 1, c != nch - 1)
        rp = pl.multiple_of(jnp.maximum(r0 - halo, 0), halo)
        rn = pl.multiple_of(jnp.minimum(r0 + CHUNK, s - halo), halo)
        zero_halo = jnp.zeros((halo, XBC_W), BF16)
        prev = jnp.where(has_prev, xbc_ref[0, pl.ds(rp, halo), :], zero_halo)
        nxt = jnp.where(has_next, xbc_ref[0, pl.ds(rn, halo), :], zero_halo)
        cur = xbc_ref[0, pl.ds(r0, CHUNK), :]
        cat = jnp.concatenate([prev, cur, nxt, jnp.zeros((SSD_CAT - CHUNK - 2 * halo, XBC_W), BF16)], axis=0)
        sh = jnp.dot(shf_ref[...], cat, preferred_element_type=F32)
        acc = cb_ref[...] + cw_ref[CONV_K // 2:CONV_K // 2 + 1, :] * cur.astype(F32)
        for i, kk in enumerate([k for k in range(CONV_K) if k != CONV_K // 2]):
            acc = acc + cw_ref[kk:kk + 1, :] * sh[i * CHUNK:(i + 1) * CHUNK]
        xc_ref[pl.ds(r0, CHUNK), :] = _silu(acc).astype(BF16)
        dtv = jax.nn.softplus(dt_ref[0, pl.ds(r0, CHUNK), :] + dtb)
        dta = dtv * a_neg
        pre = jnp.dot(tri_f, dta, precision=HIGHEST, preferred_element_type=F32)
        fwd_col = tj < SSM_HEADS
        acum = jnp.where(fwd_col, pre, pre[CHUNK - 1:CHUNK, :] - pre + dta)
        a_end = jnp.where(fwd_col[0:1], acum[CHUNK - 1:CHUNK, :], acum[0:1, :])
        hi = acum.astype(BF16)
        lo = (acum - hi.astype(F32)).astype(BF16)
        acp_ref[pl.ds(r0, CHUNK), :] = jnp.concatenate([hi, lo], axis=1)
        act_ref[pl.ds(r0, CHUNK), :] = (hi.astype(F32) + lo.astype(F32)).T
        e3_ref[pl.ds(r0, CHUNK), :] = jnp.concatenate(
            [jnp.exp(acum), jnp.exp(a_end - acum) * dtv, dtv], axis=1).astype(BF16)
        return carry

    ti = lax.broadcasted_iota(I32, (CHUNK, CHUNK), 0)
    tj = lax.broadcasted_iota(I32, (CHUNK, CHUNK), 1)
    low_half = tj < LANES // 2
    tri_f = (ti >= tj).astype(F32)
    a_neg = -jnp.exp(alog_ref[...])
    dtb = dtb_ref[...]

    def conv_pair(it, carry):
        for u in range(SSD_UNROLL):
            conv_body(it * SSD_UNROLL + u, carry)
        return carry

    lax.fori_loop(0, nch // SSD_UNROLL, conv_pair, 0)

    def chunk_update(c, direction, out_ref, out_lead):
        tri = (ti >= tj) if direction == 0 else (tj >= ti)
        last = CHUNK - 1 if direction == 0 else 0
        r0 = pl.multiple_of(c * CHUNK, CHUNK)
        xcv = xc_ref[pl.ds(r0, CHUNK), :]
        acum_t = act_ref[pl.ds(r0, CHUNK), :]
        acb = jnp.dot(acp_ref[pl.ds(r0, CHUNK), :], sel_ref[direction], preferred_element_type=F32)
        e3 = e3_ref[pl.ds(r0, CHUNK), :]
        prs = jnp.dot(jnp.concatenate([e3[:, i * LANES:(i + 1) * LANES] for i in range(3)], axis=0),
                      pair_ref[direction], preferred_element_type=F32)
        outs = []
        for g in range(2):
            xg = xcv[:, g * LANES:(g + 1) * LANES].astype(F32)
            bg = xcv[:, SSM_W + g * LANES:SSM_W + (g + 1) * LANES]
            cg = xcv[:, SSM_W + BC_W + g * LANES:SSM_W + BC_W + (g + 1) * LANES]
            gram = lax.dot_general(cg, bg, (((1,), (1,)), ((), ())), preferred_element_type=F32)
            eac_p = prs[0:CHUNK, g * LANES:(g + 1) * LANES]
            wend_p = prs[CHUNK:2 * CHUNK, g * LANES:(g + 1) * LANES]
            dt_p = prs[2 * CHUNK:3 * CHUNK, g * LANES:(g + 1) * LANES]
            mixes, dec = [], []
            for hh in range(2):
                h = 2 * g + hh
                col = direction * SSM_HEADS + h
                seg = jnp.exp(jnp.where(tri, acb[:, h * LANES:(h + 1) * LANES] - acum_t[col:col + 1, :], -jnp.inf))
                mixes.append((gram * seg).astype(BF16))
                dec.append(jnp.exp(acum_t[col:col + 1, last:last + 1]))
            xdt = (xg * dt_p).astype(BF16)
            y_intra = jnp.where(low_half, jnp.dot(mixes[0], xdt, preferred_element_type=F32),
                                jnp.dot(mixes[1], xdt, preferred_element_type=F32))
            stg = st_ref[direction, :, g * LANES:(g + 1) * LANES]
            y_inter = jnp.dot(cg, stg.astype(BF16), preferred_element_type=F32) * eac_p
            xw = (xg * wend_p).astype(BF16)
            contrib = lax.dot_general(bg, xw, (((0,), (0,)), ((), ())), preferred_element_type=F32)
            st_ref[direction, :, g * LANES:(g + 1) * LANES] = (
                stg * jnp.where(low_half[0:1], dec[0], dec[1]) + contrib)
            yg = y_intra + y_inter
            if direction == 0:
                yg = yg + dsk_ref[:, g * LANES:(g + 1) * LANES] * xg
            outs.append(yg)
        out_ref[out_lead + (pl.ds(r0, CHUNK), slice(None))] = jnp.concatenate(outs, axis=1)

    st_ref[...] = jnp.zeros_like(st_ref)

    def scan_body(it, carry):
        for u in range(SSD_UNROLL):
            step = it * SSD_UNROLL + u
            chunk_update(step, 0, y_ref, (0,))
            cb = jnp.where(step < ncb, ncb - 1 - step, nch - 1 - (step - ncb))
            chunk_update(cb, 1, yb_ref, ())
        return carry

    lax.fori_loop(0, nch // SSD_UNROLL, scan_body, 0)
    y_ref[0] = y_ref[0] + yb_ref[...]


def _ssd_constants():
    taps = [k for k in range(CONV_K) if k != CONV_K // 2]
    t = np.arange(CHUNK)[:, None]
    r = np.arange(SSD_CAT)[None, :]
    shf = np.concatenate([r == t + SSD_HALO - CONV_K // 2 + k for k in taps], axis=0)
    c = np.arange(LANES)[:, None]
    lane = np.arange(LANES)[None, :]
    sel = np.zeros((2, AC_PIECES * LANES, SSM_HEADS * LANES), bool)
    pair = np.zeros((2, LANES, 2 * LANES), bool)
    for d in range(2):
        for h in range(SSM_HEADS):
            for piece in range(AC_PIECES):
                sel[d, piece * LANES:(piece + 1) * LANES, h * LANES:(h + 1) * LANES] = c == d * SSM_HEADS + h
        for g in range(2):
            pair[d, :, g * LANES:(g + 1) * LANES] = c == d * SSM_HEADS + 2 * g + (lane >= LANES // 2)
    return jnp.asarray(shf, BF16), jnp.asarray(sel, BF16), jnp.asarray(pair, BF16)


def _ssd(xbc, dt, cw, cb, dtb, alog, dsk, consts, *, cl):
    b, s, _ = xbc.shape
    full = lambda a: pl.BlockSpec(a.shape, lambda bi: (0,) * a.ndim)
    return pl.pallas_call(
        functools.partial(_ssd_kernel, nch=s // CHUNK, cl=cl),
        grid=(b,),
        in_specs=[pl.BlockSpec((1, s, XBC_W), lambda bi: (bi, 0, 0)), pl.BlockSpec((1, s, DT_PAD), lambda bi: (bi, 0, 0)),
                  full(cw), full(cb), full(dtb), full(alog), full(dsk)] + [full(a) for a in consts],
        out_specs=pl.BlockSpec((1, s, SSM_W), lambda bi: (bi, 0, 0)),
        out_shape=jax.ShapeDtypeStruct((b, s, SSM_W), F32),
        scratch_shapes=[pltpu.VMEM((s, XBC_W), BF16), pltpu.VMEM((s, DT_PAD), F32),
                        pltpu.VMEM((s, AC_PIECES * LANES), BF16),
                        pltpu.VMEM((s, 3 * LANES), BF16), pltpu.VMEM((s, SSM_W), F32),
                        pltpu.VMEM((2, SSM_STATE, SSM_W), F32)],
        compiler_params=_cparams(("parallel",)),
        name="ssd",
    )(xbc, dt, cw, cb, dtb, alog, dsk, *consts)


TOK_ROWS = D_MODEL // LANES


def _load_tokens(ref, lead, n):
    return jnp.concatenate([ref[lead + (pl.ds(j, n, stride=TOK_ROWS), slice(None))] for j in range(TOK_ROWS)], axis=1)


def _store_tokens(ref, lead, val):
    for j in range(TOK_ROWS):
        ref[lead + (pl.ds(j, val.shape[0], stride=TOK_ROWS), slice(None))] = val[:, j * LANES:(j + 1) * LANES]


def _outproj_kernel(x_ref, mod_ref, gm_ref, att_ref, y_ref, z_ref, ng_ref, g2_ref, w_ref, xo_ref, hf_ref, *, tr, cl,
                    token_tiles):
    gated = y_ref[0] * _silu(z_ref[0].astype(F32))
    ssm = gated * lax.rsqrt(jnp.mean(gated * gated, axis=-1, keepdims=True) + EPS) * ng_ref[...]
    mix = jnp.concatenate([gm_ref[0], att_ref[0], ssm.astype(BF16)], axis=1)
    gate1, shift2, scale2 = _mod_rows(mod_ref, pl.program_id(1) * tr, tr, cl, (2, 3, 4))
    x = x_ref[0] + gate1 * jnp.dot(mix, w_ref[...], preferred_element_type=F32)
    xo_ref[0] = x
    ms = jnp.mean(x * x, axis=-1, keepdims=True)
    hf = x * lax.rsqrt(ms + EPS) * g2_ref[...] * (1.0 + scale2) + shift2
    if token_tiles:
        _store_tokens(hf_ref, (0,), hf)
    else:
        hf_ref[0] = hf.astype(hf_ref.dtype)


def _out_projection(xs, mod, gm, att, y, z, ng, g2, w_out, *, cl, nt, token_tiles):
    b, s, _ = xs.shape
    tr = s // nt
    tok = lambda w: pl.BlockSpec((1, tr, w), lambda bi, j: (bi, j, 0))
    full = lambda a: pl.BlockSpec(a.shape, lambda bi, j: (0,) * a.ndim)
    if token_tiles:
        hf_spec = pl.BlockSpec((1, tr * TOK_ROWS, LANES), lambda bi, j: (bi, j, 0))
        hf_shape = jax.ShapeDtypeStruct((b, s * TOK_ROWS, LANES), F32)
    else:
        hf_spec, hf_shape = tok(D_MODEL), jax.ShapeDtypeStruct((b, s, D_MODEL), BF16)
    return pl.pallas_call(
        functools.partial(_outproj_kernel, tr=tr, cl=cl, token_tiles=token_tiles),
        grid=(b, nt),
        in_specs=[tok(D_MODEL), pl.BlockSpec((1, 16, D_MODEL), lambda bi, j: (bi, 0, 0)), tok(GM_W), tok(ATT_W),
                  tok(SSM_W), tok(SSM_W), full(ng), full(g2), full(w_out)],
        out_specs=[tok(D_MODEL), hf_spec],
        out_shape=[jax.ShapeDtypeStruct((b, s, D_MODEL), F32), hf_shape],
        compiler_params=_cparams(("parallel", "parallel")),
        name="out_projection",
    )(xs, mod, gm, att, y, z, ng, g2, w_out)


def _swiglu_rows(h, wg_ref, wu_ref, wd_ref, act_ref):
    for f in range(D_FF // FF_BLOCK):
        cols = slice(f * FF_BLOCK, (f + 1) * FF_BLOCK)
        g = jnp.dot(h, wg_ref[0, :, cols], preferred_element_type=F32)
        u = jnp.dot(h, wu_ref[0, :, cols], preferred_element_type=F32)
        act_ref[:, cols] = (_silu(g) * u).astype(BF16)
    return jnp.dot(act_ref[...], wd_ref[0], preferred_element_type=F32)


def _ffn_kernel(x_ref, hf_ref, mod_ref, wg_ref, wu_ref, wd_ref, xo_ref, act_ref, *, tr, cl):
    (gate2,) = _mod_rows(mod_ref, pl.program_id(1) * tr, tr, cl, (5,))
    xo_ref[0] = x_ref[0] + gate2 * _swiglu_rows(hf_ref[0], wg_ref, wu_ref, wd_ref, act_ref)


def _dense_ffn(xs, hf, mod, wg, wu, wd, *, cl, nt):
    b, s, _ = xs.shape
    tr = s // nt
    tok = pl.BlockSpec((1, tr, D_MODEL), lambda bi, j: (bi, j, 0))
    full = lambda a: pl.BlockSpec(a.shape, lambda bi, j: (0,) * a.ndim)
    return pl.pallas_call(
        functools.partial(_ffn_kernel, tr=tr, cl=cl),
        grid=(b, nt),
        in_specs=[tok, tok, pl.BlockSpec((1, 16, D_MODEL), lambda bi, j: (bi, 0, 0)), full(wg), full(wu), full(wd)],
        out_specs=tok,
        out_shape=jax.ShapeDtypeStruct((b, s, D_MODEL), F32),
        scratch_shapes=[pltpu.VMEM((tr, D_FF), BF16)],
        compiler_params=_cparams(("parallel", "parallel")),
        name="dense_ffn",
    )(xs, hf, mod, wg, wu, wd)


def _router_kernel(hf_ref, wr_ref, tri_ref, mi_ref, mf_ref, cnt_ref, run_ref, *, trc):
    @pl.when(pl.program_id(0) == 0)
    def _():
        run_ref[...] = jnp.zeros_like(run_ref)

    logits = lax.dot_general(wr_ref[...], _load_tokens(hf_ref, (), trc), (((1,), (1,)), ((), ())), precision=HIGHEST,
                             preferred_element_type=F32)
    eidx = lax.broadcasted_iota(I32, (N_EXPERTS, trc), 0).astype(F32)
    m1 = jnp.max(logits, axis=0, keepdims=True)
    i1 = jnp.min(jnp.where(logits == m1, eidx, float(N_EXPERTS)), axis=0, keepdims=True)
    rest = jnp.where(eidx == i1, -jnp.inf, logits)
    m2 = jnp.max(rest, axis=0, keepdims=True)
    i2 = jnp.min(jnp.where(rest == m2, eidx, float(N_EXPERTS)), axis=0, keepdims=True)
    e2 = jnp.exp(m2 - m1)
    g1 = 1.0 / (1.0 + e2)
    g2 = e2 / (1.0 + e2)
    oh1 = (eidx == i1).astype(F32)
    oh2 = (eidx == i2).astype(F32)
    sel = oh1 + oh2
    before = jnp.dot(sel.astype(BF16), tri_ref[...], preferred_element_type=F32) + run_ref[:, 0:1]
    r1 = jnp.sum(oh1 * before, axis=0, keepdims=True)
    r2 = jnp.sum(oh2 * before, axis=0, keepdims=True)
    run_ref[...] = run_ref[...] + jnp.sum(sel, axis=1, keepdims=True)
    zi = jnp.zeros((4, trc), I32)
    mi_ref[0] = jnp.concatenate([i1.astype(I32), i2.astype(I32), r1.astype(I32), r2.astype(I32), zi], axis=0)
    mf_ref[0] = jnp.concatenate([g1, g2, jnp.zeros((6, trc), F32)], axis=0)
    cnt_ref[...] = run_ref[...].astype(I32)


def _router(hf_tiles, wr_t, tri, *, trc):
    t = hf_tiles.shape[0] // TOK_ROWS
    nt = t // trc
    meta = pl.BlockSpec((1, 8, trc), lambda i: (i, 0, 0))
    return pl.pallas_call(
        functools.partial(_router_kernel, trc=trc),
        grid=(nt,),
        in_specs=[pl.BlockSpec((trc * TOK_ROWS, LANES), lambda i: (i, 0)), pl.BlockSpec(wr_t.shape, lambda i: (0, 0)),
                  pl.BlockSpec(tri.shape, lambda i: (0, 0))],
        out_specs=[meta, meta, pl.BlockSpec((N_EXPERTS, LANES), lambda i: (0, 0))],
        out_shape=[jax.ShapeDtypeStruct((nt, 8, trc), I32), jax.ShapeDtypeStruct((nt, 8, trc), F32),
                   jax.ShapeDtypeStruct((N_EXPERTS, LANES), I32)],
        scratch_shapes=[pltpu.VMEM((N_EXPERTS, LANES), F32)],
        compiler_params=_cparams(("arbitrary",)),
        name="moe_router",
    )(hf_tiles, wr_t, tri)


DMA_UNROLL = 8


def _token_copy(src_ref, src_tok, dst_ref, dst_tok, sem):
    src = src_ref.at[pl.ds(pl.multiple_of(src_tok * TOK_ROWS, TOK_ROWS), TOK_ROWS), :]
    dst = dst_ref.at[pl.ds(pl.multiple_of(dst_tok * TOK_ROWS, TOK_ROWS), TOK_ROWS), :]
    return pltpu.make_async_copy(src, dst, sem)


def _for_tokens(n, fn):
    def block(i, carry):
        for u in range(DMA_UNROLL):
            fn(i * DMA_UNROLL + u)
        return carry

    lax.fori_loop(0, n // DMA_UNROLL, block, 0)


def _dispatch_kernel(pad_lo_ref, pad_n_ref, nact_ref, slot_ref, hf_ref, xs_ref, zero_ref, sem, zsem, *, trc, ntile):
    @pl.when(pl.program_id(0) == 0)
    def _():
        zero_ref[...] = jnp.zeros_like(zero_ref)
        tile_rows = EXPERT_TILE * TOK_ROWS

        def tile_copy(i):
            return pltpu.make_async_copy(
                zero_ref, xs_ref.at[pl.ds(pl.multiple_of(i * tile_rows, tile_rows), tile_rows), :], zsem)

        def fill_tile(i, carry):
            tile_copy(i).start()
            return carry

        def drain_tile(i, carry):
            tile_copy(i).wait()
            return carry

        lax.fori_loop(nact_ref[0], ntile, fill_tile, 0)
        lax.fori_loop(nact_ref[0], ntile, drain_tile, 0)
        for e in range(N_EXPERTS):
            lo = pad_lo_ref[e]
            n = pad_n_ref[e]

            def fill(r, carry):
                _token_copy(zero_ref, 0, xs_ref, lo + r, zsem).start()
                return carry

            def drain(r, carry):
                _token_copy(zero_ref, 0, xs_ref, lo + r, zsem).wait()
                return carry

            lax.fori_loop(0, n, fill, 0)
            lax.fori_loop(0, n, drain, 0)

    copies = lambda r: [_token_copy(hf_ref, r, xs_ref, slot_ref[0, k, r], sem) for k in range(2)]
    _for_tokens(trc, lambda r: [cp.start() for cp in copies(r)])
    _for_tokens(trc, lambda r: [cp.wait() for cp in copies(r)])


def _dispatch(pad_lo, pad_n, nact, slots, hf_tiles, nslot, *, trc):
    t = hf_tiles.shape[0] // TOK_ROWS
    nt = t // trc
    grid_spec = pltpu.PrefetchScalarGridSpec(
        num_scalar_prefetch=3,
        grid=(nt,),
        in_specs=[pl.BlockSpec((1, 2, trc), lambda i, lo, n, na: (i, 0, 0), memory_space=pltpu.SMEM),
                  pl.BlockSpec((trc * TOK_ROWS, LANES), lambda i, lo, n, na: (i, 0))],
        out_specs=pl.BlockSpec(memory_space=pl.ANY),
        scratch_shapes=[pltpu.VMEM((EXPERT_TILE * TOK_ROWS, LANES), F32), pltpu.SemaphoreType.DMA(()),
                        pltpu.SemaphoreType.DMA(())],
    )
    return pl.pallas_call(
        functools.partial(_dispatch_kernel, trc=trc, ntile=nslot // EXPERT_TILE),
        grid_spec=grid_spec,
        out_shape=jax.ShapeDtypeStruct((nslot * TOK_ROWS, LANES), F32),
        compiler_params=_cparams(("arbitrary",)),
        name="moe_dispatch",
    )(pad_lo, pad_n, nact, slots, hf_tiles)


def _expert_kernel(texp_ref, nact_ref, xs_ref, wg_ref, wu_ref, wd_ref, ys_ref, act_ref):
    active = pl.program_id(0) < nact_ref[0]

    @pl.when(active)
    def _():
        h = _load_tokens(xs_ref, (), EXPERT_TILE).astype(BF16)
        _store_tokens(ys_ref, (), _swiglu_rows(h, wg_ref, wu_ref, wd_ref, act_ref))

    @pl.when(jnp.logical_not(active))
    def _():
        ys_ref[...] = jnp.zeros_like(ys_ref)


def _expert_ffn(tile_expert, nact, xs, wg, wu, wd):
    ntile = xs.shape[0] // (EXPERT_TILE * TOK_ROWS)
    wspec = lambda shp: pl.BlockSpec((1,) + shp, lambda i, te, na: (te[i], 0, 0))
    tile = pl.BlockSpec((EXPERT_TILE * TOK_ROWS, LANES), lambda i, te, na: (i, 0))
    grid_spec = pltpu.PrefetchScalarGridSpec(
        num_scalar_prefetch=2,
        grid=(ntile,),
        in_specs=[tile, wspec((D_MODEL, D_FF)), wspec((D_MODEL, D_FF)), wspec((D_FF, D_MODEL))],
        out_specs=tile,
        scratch_shapes=[pltpu.VMEM((EXPERT_TILE, D_FF), BF16)],
    )
    return pl.pallas_call(
        _expert_kernel,
        grid_spec=grid_spec,
        out_shape=jax.ShapeDtypeStruct(xs.shape, F32),
        compiler_params=_cparams(("arbitrary",)),
        name="moe_experts",
    )(tile_expert, nact, xs, wg, wu, wd)


def _combine_kernel(slot_ref, x_ref, gate_ref, mod_ref, ys_ref, xo_ref, buf_ref, sem, *, trc, nt, cl):
    copies = lambda r: [_token_copy(ys_ref, slot_ref[0, k, r], buf_ref.at[k], r, sem) for k in range(2)]
    _for_tokens(trc, lambda r: [cp.start() for cp in copies(r)])
    _for_tokens(trc, lambda r: [cp.wait() for cp in copies(r)])
    (gate2,) = _mod_rows(mod_ref, (pl.program_id(0) % nt) * trc, trc, cl, (5,))
    gt = gate_ref[0]
    f = gt[:, 0:1] * _load_tokens(buf_ref, (0,), trc) + gt[:, 1:2] * _load_tokens(buf_ref, (1,), trc)
    xo_ref[...] = x_ref[...] + gate2 * f


def _combine(slots, x_flat, gates_t, mod, ys, *, trc, nt, cl):
    t = x_flat.shape[0]
    ntile = t // trc
    tok = pl.BlockSpec((trc, D_MODEL), lambda i: (i, 0))
    return pl.pallas_call(
        functools.partial(_combine_kernel, trc=trc, nt=nt, cl=cl),
        grid=(ntile,),
        in_specs=[pl.BlockSpec((1, 2, trc), lambda i: (i, 0, 0), memory_space=pltpu.SMEM), tok,
                  pl.BlockSpec((1, trc, 2), lambda i: (i, 0, 0)),
                  pl.BlockSpec((1, 16, D_MODEL), lambda i: (i // nt, 0, 0)), pl.BlockSpec(memory_space=pl.ANY)],
        out_specs=tok,
        out_shape=jax.ShapeDtypeStruct((t, D_MODEL), F32),
        scratch_shapes=[pltpu.VMEM((2, trc * TOK_ROWS, LANES), F32), pltpu.SemaphoreType.DMA(())],
        compiler_params=_cparams(("arbitrary",)),
        name="moe_combine",
    )(slots, x_flat, gates_t, mod, ys)


def _moe_ffn(xs, hf, mod, wr_t, tri, wg, wu, wd, *, cl, nt):
    b, s, _ = xs.shape
    t = b * s
    trr = tri.shape[0]
    trc = s // nt
    hf_flat = hf.reshape(t * TOK_ROWS, LANES)
    meta_i, meta_f, counts = _router(hf_flat, wr_t, tri, trc=trr)
    counts = counts[:, 0]
    padded = (counts + EXPERT_TILE - 1) // EXPERT_TILE * EXPERT_TILE
    ends = jnp.cumsum(padded)
    starts = ends - padded
    eid = meta_i[:, 0:2, :]
    group_start = sum(jnp.where(eid == e, starts[e], 0) for e in range(N_EXPERTS))
    slots = group_start + meta_i[:, 2:4, :]
    nslot = 2 * t + N_EXPERTS * EXPERT_TILE
    ntile = nslot // EXPERT_TILE
    nact = (ends[-1] // EXPERT_TILE).astype(I32)
    tile_lo = jnp.minimum(jnp.arange(ntile, dtype=I32), nact - 1) * EXPERT_TILE
    tile_expert = jnp.minimum(jnp.sum(tile_lo[:, None] >= ends[None, :], axis=1), N_EXPERTS - 1).astype(I32)
    nact = nact.reshape(1)
    xs_sorted = _dispatch((starts + counts).astype(I32), (padded - counts).astype(I32), nact, slots, hf_flat, nslot,
                          trc=trr)
    ys = _expert_ffn(tile_expert, nact, xs_sorted, wg, wu, wd)
    retile = lambda a: jnp.swapaxes(jnp.swapaxes(a, 0, 1).reshape(2, t // trc, trc), 0, 1)
    gates_t = jnp.swapaxes(retile(meta_f[:, 0:2, :]), 1, 2)
    out = _combine(retile(slots), xs.reshape(t, D_MODEL), gates_t, mod, ys, trc=trc, nt=nt, cl=cl)
    return out.reshape(b, s, D_MODEL)


def _slab_perm(t, lead, xp=jnp):
    nh = t.shape[-1] // HEAD_DIM
    per = nh // 2
    t = t.reshape(lead + (2, per, 2, 2, 16))
    t = xp.moveaxis(t, (-5, -4, -3, -2, -1), (-3, -5, -2, -4, -1))
    return t.reshape(lead + (nh * HEAD_DIM,))


def _rope_tables(s, cl):
    l = s - cl
    pos = np.arange(l)
    lane = np.arange(LANES)
    i = lane % 32
    freq = ROPE_BASE ** (-(i % 16).astype(np.float32) / 16.0)
    p = np.where(i[None, :] < 16, (pos // GRID_W)[:, None], (pos % GRID_W)[:, None]).astype(np.float32)
    ang = p * freq[None, :].astype(np.float32)
    sign = np.where(lane < LANES // 2, -1.0, 1.0).astype(np.float32)
    cos = np.concatenate([np.ones((cl, LANES), np.float32), np.cos(ang)], axis=0)
    sin = np.concatenate([np.zeros((cl, LANES), np.float32), np.sin(ang) * sign[None, :]], axis=0)
    return jnp.asarray(cos, F32), jnp.asarray(sin, F32)


def kernel(x, c, ctx, c_ctx, w_mod, b_mod, norm1_g, norm2_g, w_in, w_out, gm_v_g, gm_ws, gm_bs, att_q_g, att_k_g, att_sink, ssm_conv_w, ssm_conv_b, ssm_dt_bias, ssm_a_log, ssm_d, ssm_norm_g, ffn_w_gate, ffn_w_up, ffn_w_down, moe_router, moe_w_gate, moe_w_up, moe_w_down):
    b, l, _ = x.shape
    cl = ctx.shape[1]
    s = cl + l
    depth = w_mod.shape[0]
    nt = 8
    nt_moe = 4
    assert s % (nt * 16) == 0 and cl % CHUNK == 0 and l % CHUNK == 0 and l % GRID_W == 0 and b < 16
    assert (s // CHUNK) % SSD_UNROLL == 0 and (s // CHUNK) % GMLP_UNROLL == 0

    cvec = jnp.concatenate([c, c_ctx[None, :], jnp.zeros((16 - b - 1, D_MODEL), F32)], axis=0)
    mods = _modulation(cvec, w_mod, b_mod)
    lat = jnp.moveaxis(mods[:, :, :b, :], 2, 1)
    con = jnp.broadcast_to(mods[:, None, :, b, :], (depth, b, 6, D_MODEL))
    pad2 = jnp.zeros((depth, b, 2, D_MODEL), F32)
    modtab = jnp.concatenate([lat, pad2, con, pad2], axis=2)

    wi = w_in.astype(BF16)
    src = np.concatenate([_slab_perm(np.arange(ATT_W), (), np), ATT_W + _slab_perm(np.arange(KV_W), (), np)])
    perm = jnp.asarray(np.arange(ATT_W + KV_W)[:, None] == src[None, :], BF16)
    w_in_p = jnp.concatenate([
        wi[:, :, 0:2 * GM_W],
        jnp.matmul(wi[:, :, 512:1152], perm),
        wi[:, :, 1152:2304],
        jnp.pad(wi[:, :, 2304:2312], ((0, 0), (0, 0), (0, DT_PAD - 2 * SSM_HEADS))),
    ], axis=2)
    wo_att = w_out[:, GM_W:GM_W + ATT_W, :].reshape(depth, 2, 4, HEAD_DIM, D_MODEL)
    wo_att = jnp.swapaxes(wo_att, 1, 2).reshape(depth, ATT_W, D_MODEL)
    w_out_p = jnp.concatenate([w_out[:, :GM_W], wo_att, w_out[:, GM_W + ATT_W:]], axis=1).astype(BF16)
    gq = _slab_perm(jnp.tile(att_q_g, (1, 2)), (depth,))[:, None, :]
    gk = _slab_perm(jnp.tile(att_k_g, (1, 2)), (depth,))[:, None, :]
    lane = np.arange(LANES)
    seg = jnp.asarray(((lane[:, None] // 32) % 2 == (lane[None, :] // 32) % 2), BF16)
    lane2 = np.arange(GM_W)
    seg64 = jnp.asarray(lane2[:, None] // GM_HD == lane2[None, :] // GM_HD, BF16)
    cos, sin = _rope_tables(s, cl)
    ws_cat = jnp.swapaxes(gm_ws, 1, 2).reshape(depth, CHUNK, GM_HEADS * CHUNK).astype(BF16)
    bs_exp = jnp.repeat(jnp.swapaxes(gm_bs, 1, 2), GM_HD, axis=2)
    sink_heads = att_sink.reshape(depth, 2, 4).swapaxes(1, 2).reshape(depth, ATT_HEADS)
    sink_col = jnp.repeat(sink_heads, CHUNK, axis=1)[:, :, None] * LOG2E
    score_bound = (8.0 * (1.0 + 2.0 ** -6)) * jnp.max(jnp.abs(att_q_g), axis=1) * jnp.max(jnp.abs(att_k_g), axis=1)
    shift_col = jnp.maximum(sink_col, score_bound[:, None, None] * LOG2E)
    cw = jnp.pad(ssm_conv_w, ((0, 0), (0, 8 - CONV_K), (0, 0)))
    cb = ssm_conv_b[:, None, :]
    pad_dt = lambda t: jnp.pad(t.reshape(depth, 1, 2 * SSM_HEADS), ((0, 0), (0, 0), (0, DT_PAD - 2 * SSM_HEADS)))
    dtb = pad_dt(ssm_dt_bias)
    alog = pad_dt(ssm_a_log)
    dsk = jnp.repeat(ssm_d, SSM_HD, axis=1)[:, None, :]
    ssd_consts = _ssd_constants()
    wr_t = jnp.swapaxes(moe_router, 1, 2)
    trr = 1024 if (b * s) % 1024 == 0 else 512
    assert (b * s) % trr == 0
    tidx = np.arange(trr)
    tri = jnp.asarray(tidx[:, None] < tidx[None, :], BF16)
    ffn_g, ffn_u, ffn_d = ffn_w_gate.astype(BF16), ffn_w_up.astype(BF16), ffn_w_down.astype(BF16)
    moe_g, moe_u, moe_d = moe_w_gate.astype(BF16), moe_w_up.astype(BF16), moe_w_down.astype(BF16)

    xs = jnp.concatenate([ctx, x], axis=1)
    for i in range(depth):
        moe = i % 2 == 1
        mod = modtab[i]
        gu, gv, q, k, v, z, xbc, dt = _in_projection(
            xs, mod, norm1_g[i][None, :], w_in_p[i], cos, sin, gq[i], gk[i], gm_v_g[i][None, :], seg, seg64,
            cl=cl, nt=nt)
        gm = _gmlp(gu, gv, ws_cat[i], bs_exp[i])
        att = lax.cond(score_bound[i] <= SAFE_SCORE_BOUND,
                       functools.partial(_attention, cl=cl, fixed_shift=True),
                       functools.partial(_attention, cl=cl, fixed_shift=False),
                       q, k, v, sink_col[i], shift_col[i])
        y = _ssd(xbc, dt, cw[i], cb[i], dtb[i], alog[i], dsk[i], ssd_consts, cl=cl)
        xs, hf = _out_projection(xs, mod, gm, att, y, z, ssm_norm_g[i][None, :], norm2_g[i][None, :], w_out_p[i],
                                 cl=cl, nt=nt, token_tiles=moe)
        j = i // 2
        if moe:
            xs = _moe_ffn(xs, hf, mod, wr_t[j], tri, moe_g[j], moe_u[j], moe_d[j], cl=cl, nt=nt_moe)
        else:
            xs = _dense_ffn(xs, hf, mod, ffn_g[j:j + 1], ffn_u[j:j + 1], ffn_d[j:j + 1], cl=cl, nt=nt)
    return xs[:, cl:, :]
```

```python
import functools
import math

import numpy as np
import jax
import jax.numpy as jnp
from jax import lax
from jax.experimental import pallas as pl
from jax.experimental.pallas import tpu as pltpu

F32 = jnp.float32
BF16 = jnp.bfloat16
I32 = jnp.int32
HIGHEST = lax.Precision.HIGHEST

D_MODEL = 1024
CHUNK = 128
GM_HEADS, GM_HD, GM_W = 4, 64, 256
ATT_HEADS, KV_HEADS, HEAD_DIM = 8, 2, 64
ATT_W, KV_W = 512, 128
GRID_W = 64
ROPE_BASE = 10000.0
SSM_HEADS, SSM_HD, SSM_W = 4, 64, 256
SSM_STATE, BC_W, CONV_K, XBC_W = 128, 256, 5, 768
D_FF = 2816
N_EXPERTS = 8
EPS = 1e-6
LANES = 128
DT_PAD = 128
C_GU, C_GV, C_Q, C_K, C_V, C_Z, C_XBC, C_DT, IN_WP = 0, 256, 512, 1024, 1152, 1280, 1536, 2304, 2432
FF_BLOCK = 256
EXPERT_TILE = 512
VMEM_LIMIT = 56 * 1024 * 1024
LOG2E = math.log2(math.e)
SAFE_SCORE_BOUND = 40.0


def _cparams(sem):
    return pltpu.CompilerParams(dimension_semantics=sem, vmem_limit_bytes=VMEM_LIMIT)


def _silu(x):
    return x * jax.nn.sigmoid(x)


def _mod_kernel(c_ref, w_ref, b_ref, o_ref):
    cs = _silu(c_ref[...])
    o_ref[0, 0] = jnp.dot(cs, w_ref[0], precision=HIGHEST, preferred_element_type=F32) + b_ref[0, 0]


def _modulation(cvec, w_mod, b_mod):
    depth = w_mod.shape[0]
    r = cvec.shape[0]
    return pl.pallas_call(
        _mod_kernel,
        grid=(depth, 6),
        in_specs=[
            pl.BlockSpec((r, D_MODEL), lambda i, n: (0, 0)),
            pl.BlockSpec((1, D_MODEL, D_MODEL), lambda i, n: (i, 0, n)),
            pl.BlockSpec((1, 1, 1, D_MODEL), lambda i, n: (i, n, 0, 0)),
        ],
        out_specs=pl.BlockSpec((1, 1, r, D_MODEL), lambda i, n: (i, n, 0, 0)),
        out_shape=jax.ShapeDtypeStruct((depth, 6, r, D_MODEL), F32),
        compiler_params=_cparams(("arbitrary", "arbitrary")),
        name="modulation",
    )(cvec, w_mod, b_mod.reshape(depth, 6, 1, D_MODEL))


def _mod_rows(mod_ref, tile_row0, rows, cl, lo):
    ridx = tile_row0 + lax.broadcasted_iota(I32, (rows, 1), 0)
    is_ctx = ridx < cl
    m = mod_ref[0]
    return [jnp.where(is_ctx, m[8 + k:9 + k], m[k:k + 1]) for k in lo]


def _inproj_kernel(x_ref, mod_ref, g1_ref, w_ref, cos_ref, sin_ref, gq_ref, gk_ref, gvg_ref, seg_ref, seg64_ref,
                   gu_ref, gv_ref, q_ref, k_ref, v_ref, z_ref, xbc_ref, dt_ref, *, tr, cl):
    x = x_ref[0]
    ms = jnp.mean(x * x, axis=-1, keepdims=True)
    y = x * lax.rsqrt(ms + EPS) * g1_ref[...]
    shift, scale = _mod_rows(mod_ref, pl.program_id(1) * tr, tr, cl, (0, 1))
    h = (y * (1.0 + scale) + shift).astype(BF16)

    full = jnp.dot(h, w_ref[...], preferred_element_type=F32)

    def proj(a, b):
        return full[:, a:b]

    guv = proj(C_GU, C_Q)
    gu_ref[0] = jax.nn.gelu(guv[:, :GM_W]).astype(BF16)
    gv = jax.nn.gelu(guv[:, GM_W:])
    ssv = jnp.dot((gv * gv).astype(BF16), seg64_ref[...], preferred_element_type=F32)
    gv_ref[0] = (gv * lax.rsqrt(ssv * (1.0 / GM_HD) + EPS) * gvg_ref[...]).astype(BF16)

    cos = cos_ref[...]
    sin = sin_ref[...]

    def norm_rope(t, gain):
        ss = jnp.dot((t * t).astype(BF16), seg_ref[...], preferred_element_type=F32)
        tn = t * lax.rsqrt(ss * (1.0 / HEAD_DIM) + EPS) * gain
        return tn * cos + pltpu.roll(tn, LANES // 2, axis=1) * sin

    qkv = proj(C_Q, C_Z)
    gq = gq_ref[...] * (HEAD_DIM ** -0.5 * LOG2E)
    for c in range(ATT_W // LANES):
        q_ref[0, :, c * LANES:(c + 1) * LANES] = norm_rope(qkv[:, c * LANES:(c + 1) * LANES], gq).astype(BF16)
    k_ref[0] = norm_rope(qkv[:, ATT_W:ATT_W + KV_W], gk_ref[...]).astype(BF16)
    v_ref[0] = qkv[:, ATT_W + KV_W:].astype(BF16)
    z_ref[0] = proj(C_Z, C_XBC).astype(BF16)
    xbc_ref[0] = proj(C_XBC, C_DT).astype(BF16)
    dt_ref[0] = proj(C_DT, IN_WP)


def _layer_block(w, layer):
    return pl.BlockSpec((None,) + w.shape[1:], lambda *_: (layer,) + (0,) * (w.ndim - 1))


def _in_projection(xs, mod, g1, w_in, layer, cos, sin, gq, gk, gvg, seg, seg64, *, cl, nt):
    b, s, _ = xs.shape
    tr = s // nt
    tok = lambda w: pl.BlockSpec((1, tr, w), lambda bi, j: (bi, j, 0))
    full = lambda a: pl.BlockSpec(a.shape, lambda bi, j: (0,) * a.ndim)
    widths = (GM_W, GM_W, ATT_W, KV_W, KV_W, SSM_W, XBC_W)
    return pl.pallas_call(
        functools.partial(_inproj_kernel, tr=tr, cl=cl),
        grid=(b, nt),
        in_specs=[tok(D_MODEL), pl.BlockSpec((1, 16, D_MODEL), lambda bi, j: (bi, 0, 0)), full(g1),
                  _layer_block(w_in, layer),
                  pl.BlockSpec((tr, LANES), lambda bi, j: (j, 0)), pl.BlockSpec((tr, LANES), lambda bi, j: (j, 0)),
                  full(gq), full(gk), full(gvg), full(seg), full(seg64)],
        out_specs=[tok(w) for w in widths] + [tok(DT_PAD)],
        out_shape=[jax.ShapeDtypeStruct((b, s, w), BF16) for w in widths]
        + [jax.ShapeDtypeStruct((b, s, DT_PAD), F32)],
        compiler_params=_cparams(("parallel", "parallel")),
        name="in_projection",
    )(xs, mod, g1, w_in, cos, sin, gq, gk, gvg, seg, seg64)


GMLP_UNROLL = 2


def _gmlp_kernel(gu_ref, gv_ref, ws_ref, bs_ref, o_ref, *, nch):
    head = lax.broadcasted_iota(I32, (CHUNK, GM_W), 1) >> 6

    def body(it, carry):
        for u in range(GMLP_UNROLL):
            r0 = pl.multiple_of((it * GMLP_UNROLL + u) * CHUNK, CHUNK)
            v = gv_ref[0, pl.ds(r0, CHUNK), :]
            vbd = jnp.concatenate([jnp.where(head == hh, v, jnp.zeros_like(v)) for hh in range(GM_HEADS)], axis=0)
            sp = jnp.dot(ws_ref[...], vbd, preferred_element_type=F32) + bs_ref[...]
            o_ref[0, pl.ds(r0, CHUNK), :] = (gu_ref[0, pl.ds(r0, CHUNK), :].astype(F32) * sp).astype(BF16)
        return carry

    lax.fori_loop(0, nch // GMLP_UNROLL, body, 0)


def _gmlp(gu, gv, ws_cat, bs_exp):
    b, s, _ = gu.shape
    seq = pl.BlockSpec((1, s, GM_W), lambda bi: (bi, 0, 0))
    return pl.pallas_call(
        functools.partial(_gmlp_kernel, nch=s // CHUNK),
        grid=(b,),
        in_specs=[seq, seq, pl.BlockSpec(ws_cat.shape, lambda bi: (0, 0)), pl.BlockSpec(bs_exp.shape, lambda bi: (0, 0))],
        out_specs=seq,
        out_shape=jax.ShapeDtypeStruct((b, s, GM_W), BF16),
        compiler_params=_cparams(("parallel",)),
        name="gmlp",
    )(gu, gv, ws_cat, bs_exp)


def _attn_kernel(q_ref, k_ref, v_ref, sink_ref, shift_ref, o_ref, *, nch, cl, fixed_shift):
    ncb = cl // CHUNK
    nslab = ATT_W // LANES
    lane = lax.broadcasted_iota(I32, (CHUNK, LANES), 1)
    first_head = ((lane >> 5) & 1) == 0
    low_half = lane < LANES // 2
    kc = k_ref[0, 0:cl, :]
    vc = v_ref[0, 0:cl, :]
    qi = lax.broadcasted_iota(I32, (CHUNK, CHUNK), 0)
    kl = lax.broadcasted_iota(I32, (CHUNK, CHUNK), 1)
    ninf = jnp.full((CHUNK, CHUNK), -jnp.inf, F32)
    zeros = jnp.zeros((CHUNK, CHUNK), F32)
    bias_prev = jnp.where(kl >= qi, zeros, ninf)
    bias_next = jnp.where(kl <= qi, zeros, ninf)

    def body(n, carry):
        r0 = pl.multiple_of(n * CHUNK, CHUNK)
        rp = pl.multiple_of(jnp.maximum(n - 1, ncb) * CHUNK, CHUNK)
        rn = pl.multiple_of(jnp.minimum(n + 1, nch - 1) * CHUNK, CHUNK)
        q = q_ref[0, pl.ds(r0, CHUNK), :]
        kk = jnp.concatenate([kc, k_ref[0, pl.ds(rp, CHUNK), :], k_ref[0, pl.ds(r0, CHUNK), :],
                              k_ref[0, pl.ds(rn, CHUNK), :]], axis=0)
        vv = jnp.concatenate([vc, v_ref[0, pl.ds(rp, CHUNK), :], v_ref[0, pl.ds(r0, CHUNK), :],
                              v_ref[0, pl.ds(rn, CHUNK), :]], axis=0)
        is_lat = n >= ncb
        bias = jnp.concatenate([
            jnp.where(jnp.logical_and(is_lat, n - 1 >= ncb), bias_prev, ninf),
            jnp.where(is_lat, zeros, ninf),
            jnp.where(jnp.logical_and(is_lat, n + 1 <= nch - 1), bias_next, ninf)], axis=1)
        zero = jnp.zeros((CHUNK, LANES), BF16)
        blocks = []
        for c in range(nslab):
            qc = q[:, c * LANES:(c + 1) * LANES]
            blocks += [jnp.where(first_head, qc, zero), jnp.where(first_head, zero, qc)]
        qs = jnp.concatenate(blocks, axis=0)
        sc = lax.dot_general(qs, kk, (((1,), (1,)), ((), ())), preferred_element_type=F32)
        s_ctx = sc[:, :cl]
        s_loc = sc[:, cl:] + jnp.concatenate([bias] * (2 * nslab), axis=0)
        sk = sink_ref[...]

        def lane_tiles(fn, a, b):
            tiles = [a[:, t * LANES:(t + 1) * LANES] for t in range(a.shape[1] // LANES)]
            tiles += [b[:, t * LANES:(t + 1) * LANES] for t in range(b.shape[1] // LANES)]
            return functools.reduce(fn, tiles)

        if fixed_shift:
            m = shift_ref[...]
        else:
            m = jnp.maximum(jnp.max(lane_tiles(jnp.maximum, s_ctx, s_loc), axis=-1, keepdims=True), sk)
        e_ctx = jnp.exp2(s_ctx - m)
        e_loc = jnp.exp2(s_loc - m)
        p = jnp.concatenate([e_ctx.astype(BF16), e_loc.astype(BF16)], axis=1)
        pv = jnp.dot(p, jnp.concatenate([vv, jnp.ones_like(vv)], axis=1), preferred_element_type=F32)
        o = pv[:, :LANES] * (1.0 / (pv[:, LANES:] + jnp.exp2(sk - m)))
        for c in range(nslab):
            oa = o[(2 * c) * CHUNK:(2 * c + 1) * CHUNK]
            ob = o[(2 * c + 1) * CHUNK:(2 * c + 2) * CHUNK]
            o_ref[0, pl.ds(r0, CHUNK), c * LANES:(c + 1) * LANES] = jnp.where(low_half, oa, ob).astype(BF16)
        return carry

    lax.fori_loop(0, nch, body, 0)


def _attention(q, k, v, sink_col, shift_col, *, cl, fixed_shift):
    b, s, _ = q.shape
    seq = lambda w: pl.BlockSpec((1, s, w), lambda bi: (bi, 0, 0))
    col = pl.BlockSpec(sink_col.shape, lambda bi: (0, 0))
    return pl.pallas_call(
        functools.partial(_attn_kernel, nch=s // CHUNK, cl=cl, fixed_shift=fixed_shift),
        grid=(b,),
        in_specs=[seq(ATT_W), seq(KV_W), seq(KV_W), col, col],
        out_specs=seq(ATT_W),
        out_shape=jax.ShapeDtypeStruct((b, s, ATT_W), BF16),
        compiler_params=_cparams(("parallel",)),
        name="attention_fixed_shift" if fixed_shift else "attention_row_max",
    )(q, k, v, sink_col, shift_col)


SSD_HALO = 16
SSD_CAT = 256
SSD_UNROLL = 2
AC_PIECES = 2


def _ssd_kernel(xbc_ref, dt_ref, cw_ref, cb_ref, dtb_ref, alog_ref, dsk_ref, shf_ref, sel_ref, pair_ref, y_ref,
                xc_ref, act_ref, acp_ref, e3_ref, yb_ref, st_ref, *, nch, cl):
    ncb = cl // CHUNK
    s = nch * CHUNK
    halo = SSD_HALO

    def conv_body(c, carry):
        r0 = pl.multiple_of(c * CHUNK, CHUNK)
        has_prev = jnp.logical_and(c != 0, c != ncb)
        has_next = jnp.logical_and(c != ncb - 1, c != nch - 1)
        rp = pl.multiple_of(jnp.maximum(r0 - halo, 0), halo)
        rn = pl.multiple_of(jnp.minimum(r0 + CHUNK, s - halo), halo)
        zero_halo = jnp.zeros((halo, XBC_W), BF16)
        prev = jnp.where(has_prev, xbc_ref[0, pl.ds(rp, halo), :], zero_halo)
        nxt = jnp.where(has_next, xbc_ref[0, pl.ds(rn, halo), :], zero_halo)
        cur = xbc_ref[0, pl.ds(r0, CHUNK), :]
        cat = jnp.concatenate([prev, cur, nxt, jnp.zeros((SSD_CAT - CHUNK - 2 * halo, XBC_W), BF16)], axis=0)
        sh = jnp.dot(shf_ref[...], cat, preferred_element_type=F32)
        acc = cb_ref[...] + cw_ref[CONV_K // 2:CONV_K // 2 + 1, :] * cur.astype(F32)
        for i, kk in enumerate([k for k in range(CONV_K) if k != CONV_K // 2]):
            acc = acc + cw_ref[kk:kk + 1, :] * sh[i * CHUNK:(i + 1) * CHUNK]
        xc_ref[pl.ds(r0, CHUNK), :] = _silu(acc).astype(BF16)
        dtv = jax.nn.softplus(dt_ref[0, pl.ds(r0, CHUNK), :] + dtb)
        dta = dtv * a_neg
        pre = jnp.dot(tri_f, dta, precision=HIGHEST, preferred_element_type=F32)
        fwd_col = tj < SSM_HEADS
        acum = jnp.where(fwd_col, pre, pre[CHUNK - 1:CHUNK, :] - pre + dta)
        a_end = jnp.where(fwd_col[0:1], acum[CHUNK - 1:CHUNK, :], acum[0:1, :])
        hi = acum.astype(BF16)
        lo = (acum - hi.astype(F32)).astype(BF16)
        acp_ref[pl.ds(r0, CHUNK), :] = jnp.concatenate([hi, lo], axis=1)
        act_ref[pl.ds(r0, CHUNK), :] = (hi.astype(F32) + lo.astype(F32)).T
        e3_ref[pl.ds(r0, CHUNK), :] = jnp.concatenate(
            [jnp.exp(acum), jnp.exp(a_end - acum) * dtv, dtv], axis=1).astype(BF16)
        return carry

    ti = lax.broadcasted_iota(I32, (CHUNK, CHUNK), 0)
    tj = lax.broadcasted_iota(I32, (CHUNK, CHUNK), 1)
    low_half = tj < LANES // 2
    tri_f = (ti >= tj).astype(F32)
    a_neg = -jnp.exp(alog_ref[...])
    dtb = dtb_ref[...]

    def conv_pair(it, carry):
        for u in range(SSD_UNROLL):
            conv_body(it * SSD_UNROLL + u, carry)
        return carry

    lax.fori_loop(0, nch // SSD_UNROLL, conv_pair, 0)

    def chunk_update(c, direction, out_ref, out_lead):
        tri = (ti >= tj) if direction == 0 else (tj >= ti)
        last = CHUNK - 1 if direction == 0 else 0
        r0 = pl.multiple_of(c * CHUNK, CHUNK)
        xcv = xc_ref[pl.ds(r0, CHUNK), :]
        acum_t = act_ref[pl.ds(r0, CHUNK), :]
        acb = jnp.dot(acp_ref[pl.ds(r0, CHUNK), :], sel_ref[direction], preferred_element_type=F32)
        e3 = e3_ref[pl.ds(r0, CHUNK), :]
        prs = jnp.dot(jnp.concatenate([e3[:, i * LANES:(i + 1) * LANES] for i in range(3)], axis=0),
                      pair_ref[direction], preferred_element_type=F32)
        outs = []
        for g in range(2):
            xg = xcv[:, g * LANES:(g + 1) * LANES].astype(F32)
            bg = xcv[:, SSM_W + g * LANES:SSM_W + (g + 1) * LANES]
            cg = xcv[:, SSM_W + BC_W + g * LANES:SSM_W + BC_W + (g + 1) * LANES]
            gram = lax.dot_general(cg, bg, (((1,), (1,)), ((), ())), preferred_element_type=F32)
            eac_p = prs[0:CHUNK, g * LANES:(g + 1) * LANES]
            wend_p = prs[CHUNK:2 * CHUNK, g * LANES:(g + 1) * LANES]
            dt_p = prs[2 * CHUNK:3 * CHUNK, g * LANES:(g + 1) * LANES]
            mixes, dec = [], []
            for hh in range(2):
                h = 2 * g + hh
                col = direction * SSM_HEADS + h
                seg = jnp.exp(jnp.where(tri, acb[:, h * LANES:(h + 1) * LANES] - acum_t[col:col + 1, :], -jnp.inf))
                mixes.append((gram * seg).astype(BF16))
                dec.append(jnp.exp(acum_t[col:col + 1, last:last + 1]))
            xdt = (xg * dt_p).astype(BF16)
            y_intra = jnp.where(low_half, jnp.dot(mixes[0], xdt, preferred_element_type=F32),
                                jnp.dot(mixes[1], xdt, preferred_element_type=F32))
            stg = st_ref[direction, :, g * LANES:(g + 1) * LANES]
            y_inter = jnp.dot(cg, stg.astype(BF16), preferred_element_type=F32) * eac_p
            xw = (xg * wend_p).astype(BF16)
            contrib = lax.dot_general(bg, xw, (((0,), (0,)), ((), ())), preferred_element_type=F32)
            st_ref[direction, :, g * LANES:(g + 1) * LANES] = (
                stg * jnp.where(low_half[0:1], dec[0], dec[1]) + contrib)
            yg = y_intra + y_inter
            if direction == 0:
                yg = yg + dsk_ref[:, g * LANES:(g + 1) * LANES] * xg
            outs.append(yg)
        out_ref[out_lead + (pl.ds(r0, CHUNK), slice(None))] = jnp.concatenate(outs, axis=1)

    st_ref[...] = jnp.zeros_like(st_ref)

    def scan_body(it, carry):
        for u in range(SSD_UNROLL):
            step = it * SSD_UNROLL + u
            chunk_update(step, 0, y_ref, (0,))
            cb = jnp.where(step < ncb, ncb - 1 - step, nch - 1 - (step - ncb))
            chunk_update(cb, 1, yb_ref, ())
        return carry

    lax.fori_loop(0, nch // SSD_UNROLL, scan_body, 0)
    y_ref[0] = y_ref[0] + yb_ref[...]


def _ssd_constants():
    taps = [k for k in range(CONV_K) if k != CONV_K // 2]
    t = np.arange(CHUNK)[:, None]
    r = np.arange(SSD_CAT)[None, :]
    shf = np.concatenate([r == t + SSD_HALO - CONV_K // 2 + k for k in taps], axis=0)
    c = np.arange(LANES)[:, None]
    lane = np.arange(LANES)[None, :]
    sel = np.zeros((2, AC_PIECES * LANES, SSM_HEADS * LANES), bool)
    pair = np.zeros((2, LANES, 2 * LANES), bool)
    for d in range(2):
        for h in range(SSM_HEADS):
            for piece in range(AC_PIECES):
                sel[d, piece * LANES:(piece + 1) * LANES, h * LANES:(h + 1) * LANES] = c == d * SSM_HEADS + h
        for g in range(2):
            pair[d, :, g * LANES:(g + 1) * LANES] = c == d * SSM_HEADS + 2 * g + (lane >= LANES // 2)
    return jnp.asarray(shf, BF16), jnp.asarray(sel, BF16), jnp.asarray(pair, BF16)


def _ssd(xbc, dt, cw, cb, dtb, alog, dsk, consts, *, cl):
    b, s, _ = xbc.shape
    full = lambda a: pl.BlockSpec(a.shape, lambda bi: (0,) * a.ndim)
    return pl.pallas_call(
        functools.partial(_ssd_kernel, nch=s // CHUNK, cl=cl),
        grid=(b,),
        in_specs=[pl.BlockSpec((1, s, XBC_W), lambda bi: (bi, 0, 0)), pl.BlockSpec((1, s, DT_PAD), lambda bi: (bi, 0, 0)),
                  full(cw), full(cb), full(dtb), full(alog), full(dsk)] + [full(a) for a in consts],
        out_specs=pl.BlockSpec((1, s, SSM_W), lambda bi: (bi, 0, 0)),
        out_shape=jax.ShapeDtypeStruct((b, s, SSM_W), F32),
        scratch_shapes=[pltpu.VMEM((s, XBC_W), BF16), pltpu.VMEM((s, DT_PAD), F32),
                        pltpu.VMEM((s, AC_PIECES * LANES), BF16),
                        pltpu.VMEM((s, 3 * LANES), BF16), pltpu.VMEM((s, SSM_W), F32),
                        pltpu.VMEM((2, SSM_STATE, SSM_W), F32)],
        compiler_params=_cparams(("parallel",)),
        name="ssd",
    )(xbc, dt, cw, cb, dtb, alog, dsk, *consts)


TOK_ROWS = D_MODEL // LANES


def _load_tokens(ref, lead, n):
    return jnp.concatenate([ref[lead + (pl.ds(j, n, stride=TOK_ROWS), slice(None))] for j in range(TOK_ROWS)], axis=1)


def _store_tokens(ref, lead, val):
    for j in range(TOK_ROWS):
        ref[lead + (pl.ds(j, val.shape[0], stride=TOK_ROWS), slice(None))] = val[:, j * LANES:(j + 1) * LANES]


def _outproj_kernel(x_ref, mod_ref, gm_ref, att_ref, y_ref, z_ref, ng_ref, g2_ref, w_ref, xo_ref, hf_ref, *, tr, cl,
                    token_tiles):
    gated = y_ref[0] * _silu(z_ref[0].astype(F32))
    ssm = gated * lax.rsqrt(jnp.mean(gated * gated, axis=-1, keepdims=True) + EPS) * ng_ref[...]
    mix = jnp.concatenate([gm_ref[0], att_ref[0], ssm.astype(BF16)], axis=1)
    gate1, shift2, scale2 = _mod_rows(mod_ref, pl.program_id(1) * tr, tr, cl, (2, 3, 4))
    x = x_ref[0] + gate1 * jnp.dot(mix, w_ref[...], preferred_element_type=F32)
    xo_ref[0] = x
    ms = jnp.mean(x * x, axis=-1, keepdims=True)
    hf = x * lax.rsqrt(ms + EPS) * g2_ref[...] * (1.0 + scale2) + shift2
    if token_tiles:
        _store_tokens(hf_ref, (0,), hf)
    else:
        hf_ref[0] = hf.astype(hf_ref.dtype)


def _out_projection(xs, mod, gm, att, y, z, ng, g2, w_out, layer, *, cl, nt, token_tiles):
    b, s, _ = xs.shape
    tr = s // nt
    tok = lambda w: pl.BlockSpec((1, tr, w), lambda bi, j: (bi, j, 0))
    full = lambda a: pl.BlockSpec(a.shape, lambda bi, j: (0,) * a.ndim)
    if token_tiles:
        hf_spec = pl.BlockSpec((1, tr * TOK_ROWS, LANES), lambda bi, j: (bi, j, 0))
        hf_shape = jax.ShapeDtypeStruct((b, s * TOK_ROWS, LANES), F32)
    else:
        hf_spec, hf_shape = tok(D_MODEL), jax.ShapeDtypeStruct((b, s, D_MODEL), BF16)
    return pl.pallas_call(
        functools.partial(_outproj_kernel, tr=tr, cl=cl, token_tiles=token_tiles),
        grid=(b, nt),
        in_specs=[tok(D_MODEL), pl.BlockSpec((1, 16, D_MODEL), lambda bi, j: (bi, 0, 0)), tok(GM_W), tok(ATT_W),
                  tok(SSM_W), tok(SSM_W), full(ng), full(g2), _layer_block(w_out, layer)],
        out_specs=[tok(D_MODEL), hf_spec],
        out_shape=[jax.ShapeDtypeStruct((b, s, D_MODEL), F32), hf_shape],
        compiler_params=_cparams(("parallel", "parallel")),
        name="out_projection",
    )(xs, mod, gm, att, y, z, ng, g2, w_out)


def _swiglu_rows(h, wg_ref, wu_ref, wd_ref, act_ref):
    for f in range(D_FF // FF_BLOCK):
        cols = slice(f * FF_BLOCK, (f + 1) * FF_BLOCK)
        g = jnp.dot(h, wg_ref[0, :, cols], preferred_element_type=F32)
        u = jnp.dot(h, wu_ref[0, :, cols], preferred_element_type=F32)
        act_ref[:, cols] = (_silu(g) * u).astype(BF16)
    return jnp.dot(act_ref[...], wd_ref[0], preferred_element_type=F32)


def _ffn_kernel(x_ref, hf_ref, mod_ref, wg_ref, wu_ref, wd_ref, xo_ref, act_ref, *, tr, cl):
    (gate2,) = _mod_rows(mod_ref, pl.program_id(1) * tr, tr, cl, (5,))
    xo_ref[0] = x_ref[0] + gate2 * _swiglu_rows(hf_ref[0], wg_ref, wu_ref, wd_ref, act_ref)


def _dense_ffn(xs, hf, mod, wg, wu, wd, layer, *, cl, nt):
    b, s, _ = xs.shape
    tr = s // nt
    tok = pl.BlockSpec((1, tr, D_MODEL), lambda bi, j: (bi, j, 0))
    wspec = lambda w: pl.BlockSpec((1,) + w.shape[1:], lambda bi, j: (layer, 0, 0))
    return pl.pallas_call(
        functools.partial(_ffn_kernel, tr=tr, cl=cl),
        grid=(b, nt),
        in_specs=[tok, tok, pl.BlockSpec((1, 16, D_MODEL), lambda bi, j: (bi, 0, 0)), wspec(wg), wspec(wu), wspec(wd)],
        out_specs=tok,
        out_shape=jax.ShapeDtypeStruct((b, s, D_MODEL), F32),
        scratch_shapes=[pltpu.VMEM((tr, D_FF), BF16)],
        compiler_params=_cparams(("parallel", "parallel")),
        name="dense_ffn",
    )(xs, hf, mod, wg, wu, wd)


def _router_kernel(hf_ref, wr_ref, tri_ref, mi_ref, mf_ref, cnt_ref, run_ref, *, trc):
    @pl.when(pl.program_id(0) == 0)
    def _():
        run_ref[...] = jnp.zeros_like(run_ref)

    logits = lax.dot_general(wr_ref[...], _load_tokens(hf_ref, (), trc), (((1,), (1,)), ((), ())), precision=HIGHEST,
                             preferred_element_type=F32)
    eidx = lax.broadcasted_iota(I32, (N_EXPERTS, trc), 0).astype(F32)
    m1 = jnp.max(logits, axis=0, keepdims=True)
    i1 = jnp.min(jnp.where(logits == m1, eidx, float(N_EXPERTS)), axis=0, keepdims=True)
    rest = jnp.where(eidx == i1, -jnp.inf, logits)
    m2 = jnp.max(rest, axis=0, keepdims=True)
    i2 = jnp.min(jnp.where(rest == m2, eidx, float(N_EXPERTS)), axis=0, keepdims=True)
    e2 = jnp.exp(m2 - m1)
    g1 = 1.0 / (1.0 + e2)
    g2 = e2 / (1.0 + e2)
    oh1 = (eidx == i1).astype(F32)
    oh2 = (eidx == i2).astype(F32)
    sel = oh1 + oh2
    before = jnp.dot(sel.astype(BF16), tri_ref[...], preferred_element_type=F32) + run_ref[:, 0:1]
    r1 = jnp.sum(oh1 * before, axis=0, keepdims=True)
    r2 = jnp.sum(oh2 * before, axis=0, keepdims=True)
    run_ref[...] = run_ref[...] + jnp.sum(sel, axis=1, keepdims=True)
    zi = jnp.zeros((4, trc), I32)
    mi_ref[0] = jnp.concatenate([i1.astype(I32), i2.astype(I32), r1.astype(I32), r2.astype(I32), zi], axis=0)
    mf_ref[0] = jnp.concatenate([g1, g2, jnp.zeros((6, trc), F32)], axis=0)
    cnt_ref[...] = run_ref[...].astype(I32)


def _router(hf_tiles, wr_t, tri, *, trc):
    t = hf_tiles.shape[0] // TOK_ROWS
    nt = t // trc
    meta = pl.BlockSpec((1, 8, trc), lambda i: (i, 0, 0))
    return pl.pallas_call(
        functools.partial(_router_kernel, trc=trc),
        grid=(nt,),
        in_specs=[pl.BlockSpec((trc * TOK_ROWS, LANES), lambda i: (i, 0)), pl.BlockSpec(wr_t.shape, lambda i: (0, 0)),
                  pl.BlockSpec(tri.shape, lambda i: (0, 0))],
        out_specs=[meta, meta, pl.BlockSpec((N_EXPERTS, LANES), lambda i: (0, 0))],
        out_shape=[jax.ShapeDtypeStruct((nt, 8, trc), I32), jax.ShapeDtypeStruct((nt, 8, trc), F32),
                   jax.ShapeDtypeStruct((N_EXPERTS, LANES), I32)],
        scratch_shapes=[pltpu.VMEM((N_EXPERTS, LANES), F32)],
        compiler_params=_cparams(("arbitrary",)),
        name="moe_router",
    )(hf_tiles, wr_t, tri)


DMA_UNROLL = 8


def _token_copy(src_ref, src_tok, dst_ref, dst_tok, sem):
    src = src_ref.at[pl.ds(pl.multiple_of(src_tok * TOK_ROWS, TOK_ROWS), TOK_ROWS), :]
    dst = dst_ref.at[pl.ds(pl.multiple_of(dst_tok * TOK_ROWS, TOK_ROWS), TOK_ROWS), :]
    return pltpu.make_async_copy(src, dst, sem)


def _for_tokens(n, fn):
    def block(i, carry):
        for u in range(DMA_UNROLL):
            fn(i * DMA_UNROLL + u)
        return carry

    lax.fori_loop(0, n // DMA_UNROLL, block, 0)


def _dispatch_kernel(pad_lo_ref, pad_n_ref, nact_ref, slot_ref, hf_ref, xs_ref, zero_ref, sem, zsem, *, trc, ntile):
    @pl.when(pl.program_id(0) == 0)
    def _():
        zero_ref[...] = jnp.zeros_like(zero_ref)
        tile_rows = EXPERT_TILE * TOK_ROWS

        def tile_copy(i):
            return pltpu.make_async_copy(
                zero_ref, xs_ref.at[pl.ds(pl.multiple_of(i * tile_rows, tile_rows), tile_rows), :], zsem)

        def fill_tile(i, carry):
            tile_copy(i).start()
            return carry

        def drain_tile(i, carry):
            tile_copy(i).wait()
            return carry

        lax.fori_loop(nact_ref[0], ntile, fill_tile, 0)
        lax.fori_loop(nact_ref[0], ntile, drain_tile, 0)
        for e in range(N_EXPERTS):
            lo = pad_lo_ref[e]
            n = pad_n_ref[e]

            def fill(r, carry):
                _token_copy(zero_ref, 0, xs_ref, lo + r, zsem).start()
                return carry

            def drain(r, carry):
                _token_copy(zero_ref, 0, xs_ref, lo + r, zsem).wait()
                return carry

            lax.fori_loop(0, n, fill, 0)
            lax.fori_loop(0, n, drain, 0)

    copies = lambda r: [_token_copy(hf_ref, r, xs_ref, slot_ref[0, k, r], sem) for k in range(2)]
    _for_tokens(trc, lambda r: [cp.start() for cp in copies(r)])
    _for_tokens(trc, lambda r: [cp.wait() for cp in copies(r)])


def _dispatch(pad_lo, pad_n, nact, slots, hf_tiles, nslot, *, trc):
    t = hf_tiles.shape[0] // TOK_ROWS
    nt = t // trc
    grid_spec = pltpu.PrefetchScalarGridSpec(
        num_scalar_prefetch=3,
        grid=(nt,),
        in_specs=[pl.BlockSpec((1, 2, trc), lambda i, lo, n, na: (i, 0, 0), memory_space=pltpu.SMEM),
                  pl.BlockSpec((trc * TOK_ROWS, LANES), lambda i, lo, n, na: (i, 0))],
        out_specs=pl.BlockSpec(memory_space=pl.ANY),
        scratch_shapes=[pltpu.VMEM((EXPERT_TILE * TOK_ROWS, LANES), F32), pltpu.SemaphoreType.DMA(()),
                        pltpu.SemaphoreType.DMA(())],
    )
    return pl.pallas_call(
        functools.partial(_dispatch_kernel, trc=trc, ntile=nslot // EXPERT_TILE),
        grid_spec=grid_spec,
        out_shape=jax.ShapeDtypeStruct((nslot * TOK_ROWS, LANES), F32),
        compiler_params=_cparams(("arbitrary",)),
        name="moe_dispatch",
    )(pad_lo, pad_n, nact, slots, hf_tiles)


def _expert_kernel(texp_ref, nact_ref, xs_ref, wg_ref, wu_ref, wd_ref, ys_ref, act_ref):
    active = pl.program_id(0) < nact_ref[0]

    @pl.when(active)
    def _():
        h = _load_tokens(xs_ref, (), EXPERT_TILE).astype(BF16)
        _store_tokens(ys_ref, (), _swiglu_rows(h, wg_ref, wu_ref, wd_ref, act_ref))

    @pl.when(jnp.logical_not(active))
    def _():
        ys_ref[...] = jnp.zeros_like(ys_ref)


def _expert_ffn(tile_expert, nact, xs, wg, wu, wd, layer):
    ntile = xs.shape[0] // (EXPERT_TILE * TOK_ROWS)
    wspec = lambda shp: pl.BlockSpec((None, 1) + shp, lambda i, te, na: (layer, te[i], 0, 0))
    tile = pl.BlockSpec((EXPERT_TILE * TOK_ROWS, LANES), lambda i, te, na: (i, 0))
    grid_spec = pltpu.PrefetchScalarGridSpec(
        num_scalar_prefetch=2,
        grid=(ntile,),
        in_specs=[tile, wspec((D_MODEL, D_FF)), wspec((D_MODEL, D_FF)), wspec((D_FF, D_MODEL))],
        out_specs=tile,
        scratch_shapes=[pltpu.VMEM((EXPERT_TILE, D_FF), BF16)],
    )
    return pl.pallas_call(
        _expert_kernel,
        grid_spec=grid_spec,
        out_shape=jax.ShapeDtypeStruct(xs.shape, F32),
        compiler_params=_cparams(("arbitrary",)),
        name="moe_experts",
    )(tile_expert, nact, xs, wg, wu, wd)


def _combine_kernel(slot_ref, x_ref, gate_ref, mod_ref, ys_ref, xo_ref, buf_ref, sem, *, trc, nt, cl):
    copies = lambda r: [_token_copy(ys_ref, slot_ref[0, k, r], buf_ref.at[k], r, sem) for k in range(2)]
    _for_tokens(trc, lambda r: [cp.start() for cp in copies(r)])
    _for_tokens(trc, lambda r: [cp.wait() for cp in copies(r)])
    (gate2,) = _mod_rows(mod_ref, (pl.program_id(0) % nt) * trc, trc, cl, (5,))
    gt = gate_ref[0]
    f = gt[:, 0:1] * _load_tokens(buf_ref, (0,), trc) + gt[:, 1:2] * _load_tokens(buf_ref, (1,), trc)
    xo_ref[...] = x_ref[...] + gate2 * f


def _combine(slots, x_flat, gates_t, mod, ys, *, trc, nt, cl):
    t = x_flat.shape[0]
    ntile = t // trc
    tok = pl.BlockSpec((trc, D_MODEL), lambda i: (i, 0))
    return pl.pallas_call(
        functools.partial(_combine_kernel, trc=trc, nt=nt, cl=cl),
        grid=(ntile,),
        in_specs=[pl.BlockSpec((1, 2, trc), lambda i: (i, 0, 0), memory_space=pltpu.SMEM), tok,
                  pl.BlockSpec((1, trc, 2), lambda i: (i, 0, 0)),
                  pl.BlockSpec((1, 16, D_MODEL), lambda i: (i // nt, 0, 0)), pl.BlockSpec(memory_space=pl.ANY)],
        out_specs=tok,
        out_shape=jax.ShapeDtypeStruct((t, D_MODEL), F32),
        scratch_shapes=[pltpu.VMEM((2, trc * TOK_ROWS, LANES), F32), pltpu.SemaphoreType.DMA(())],
        compiler_params=_cparams(("arbitrary",)),
        name="moe_combine",
    )(slots, x_flat, gates_t, mod, ys)


def _moe_ffn(xs, hf, mod, wr_t, tri, wg, wu, wd, layer, *, cl, nt):
    b, s, _ = xs.shape
    t = b * s
    trr = tri.shape[0]
    trc = s // nt
    hf_flat = hf.reshape(t * TOK_ROWS, LANES)
    meta_i, meta_f, counts = _router(hf_flat, wr_t, tri, trc=trr)
    counts = counts[:, 0]
    padded = (counts + EXPERT_TILE - 1) // EXPERT_TILE * EXPERT_TILE
    ends = jnp.cumsum(padded)
    starts = ends - padded
    eid = meta_i[:, 0:2, :]
    group_start = sum(jnp.where(eid == e, starts[e], 0) for e in range(N_EXPERTS))
    slots = group_start + meta_i[:, 2:4, :]
    nslot = 2 * t + N_EXPERTS * EXPERT_TILE
    ntile = nslot // EXPERT_TILE
    nact = (ends[-1] // EXPERT_TILE).astype(I32)
    tile_lo = jnp.minimum(jnp.arange(ntile, dtype=I32), nact - 1) * EXPERT_TILE
    tile_expert = jnp.minimum(jnp.sum(tile_lo[:, None] >= ends[None, :], axis=1), N_EXPERTS - 1).astype(I32)
    nact = nact.reshape(1)
    xs_sorted = _dispatch((starts + counts).astype(I32), (padded - counts).astype(I32), nact, slots, hf_flat, nslot,
                          trc=trr)
    ys = _expert_ffn(tile_expert, nact, xs_sorted, wg, wu, wd, layer)
    retile = lambda a: jnp.swapaxes(jnp.swapaxes(a, 0, 1).reshape(2, t // trc, trc), 0, 1)
    gates_t = jnp.swapaxes(retile(meta_f[:, 0:2, :]), 1, 2)
    out = _combine(retile(slots), xs.reshape(t, D_MODEL), gates_t, mod, ys, trc=trc, nt=nt, cl=cl)
    return out.reshape(b, s, D_MODEL)


def _slab_perm(t, lead, xp=jnp):
    nh = t.shape[-1] // HEAD_DIM
    per = nh // 2
    t = t.reshape(lead + (2, per, 2, 2, 16))
    t = xp.moveaxis(t, (-5, -4, -3, -2, -1), (-3, -5, -2, -4, -1))
    return t.reshape(lead + (nh * HEAD_DIM,))


def _rope_tables(s, cl):
    l = s - cl
    pos = np.arange(l)
    lane = np.arange(LANES)
    i = lane % 32
    freq = ROPE_BASE ** (-(i % 16).astype(np.float32) / 16.0)
    p = np.where(i[None, :] < 16, (pos // GRID_W)[:, None], (pos % GRID_W)[:, None]).astype(np.float32)
    ang = p * freq[None, :].astype(np.float32)
    sign = np.where(lane < LANES // 2, -1.0, 1.0).astype(np.float32)
    cos = np.concatenate([np.ones((cl, LANES), np.float32), np.cos(ang)], axis=0)
    sin = np.concatenate([np.zeros((cl, LANES), np.float32), np.sin(ang) * sign[None, :]], axis=0)
    return jnp.asarray(cos, F32), jnp.asarray(sin, F32)


def kernel(x, c, ctx, c_ctx, w_mod, b_mod, norm1_g, norm2_g, w_in, w_out, gm_v_g, gm_ws, gm_bs, att_q_g, att_k_g, att_sink, ssm_conv_w, ssm_conv_b, ssm_dt_bias, ssm_a_log, ssm_d, ssm_norm_g, ffn_w_gate, ffn_w_up, ffn_w_down, moe_router, moe_w_gate, moe_w_up, moe_w_down):
    b, l, _ = x.shape
    cl = ctx.shape[1]
    s = cl + l
    depth = w_mod.shape[0]
    nt = 8
    nt_moe = 4
    assert s % (nt * 16) == 0 and cl % CHUNK == 0 and l % CHUNK == 0 and l % GRID_W == 0 and b < 16
    assert (s // CHUNK) % SSD_UNROLL == 0 and (s // CHUNK) % GMLP_UNROLL == 0

    cvec = jnp.concatenate([c, c_ctx[None, :], jnp.zeros((16 - b - 1, D_MODEL), F32)], axis=0)
    mods = _modulation(cvec, w_mod, b_mod)
    lat = jnp.moveaxis(mods[:, :, :b, :], 2, 1)
    con = jnp.broadcast_to(mods[:, None, :, b, :], (depth, b, 6, D_MODEL))
    pad2 = jnp.zeros((depth, b, 2, D_MODEL), F32)
    modtab = jnp.concatenate([lat, pad2, con, pad2], axis=2)

    wi = w_in.astype(BF16)
    src = np.concatenate([_slab_perm(np.arange(ATT_W), (), np), ATT_W + _slab_perm(np.arange(KV_W), (), np)])
    perm = jnp.asarray(np.arange(ATT_W + KV_W)[:, None] == src[None, :], BF16)
    w_in_p = jnp.concatenate([
        wi[:, :, 0:2 * GM_W],
        jnp.matmul(wi[:, :, 512:1152], perm),
        wi[:, :, 1152:2304],
        jnp.pad(wi[:, :, 2304:2312], ((0, 0), (0, 0), (0, DT_PAD - 2 * SSM_HEADS))),
    ], axis=2)
    wo_att = w_out[:, GM_W:GM_W + ATT_W, :].reshape(depth, 2, 4, HEAD_DIM, D_MODEL)
    wo_att = jnp.swapaxes(wo_att, 1, 2).reshape(depth, ATT_W, D_MODEL)
    w_out_p = jnp.concatenate([w_out[:, :GM_W], wo_att, w_out[:, GM_W + ATT_W:]], axis=1).astype(BF16)
    gq = _slab_perm(jnp.tile(att_q_g, (1, 2)), (depth,))[:, None, :]
    gk = _slab_perm(jnp.tile(att_k_g, (1, 2)), (depth,))[:, None, :]
    lane = np.arange(LANES)
    seg = jnp.asarray(((lane[:, None] // 32) % 2 == (lane[None, :] // 32) % 2), BF16)
    lane2 = np.arange(GM_W)
    seg64 = jnp.asarray(lane2[:, None] // GM_HD == lane2[None, :] // GM_HD, BF16)
    cos, sin = _rope_tables(s, cl)
    ws_cat = jnp.swapaxes(gm_ws, 1, 2).reshape(depth, CHUNK, GM_HEADS * CHUNK).astype(BF16)
    bs_exp = jnp.repeat(jnp.swapaxes(gm_bs, 1, 2), GM_HD, axis=2)
    sink_heads = att_sink.reshape(depth, 2, 4).swapaxes(1, 2).reshape(depth, ATT_HEADS)
    sink_col = jnp.repeat(sink_heads, CHUNK, axis=1)[:, :, None] * LOG2E
    score_bound = (8.0 * (1.0 + 2.0 ** -6)) * jnp.max(jnp.abs(att_q_g), axis=1) * jnp.max(jnp.abs(att_k_g), axis=1)
    shift_col = jnp.maximum(sink_col, score_bound[:, None, None] * LOG2E)
    cw = jnp.pad(ssm_conv_w, ((0, 0), (0, 8 - CONV_K), (0, 0)))
    cb = ssm_conv_b[:, None, :]
    pad_dt = lambda t: jnp.pad(t.reshape(depth, 1, 2 * SSM_HEADS), ((0, 0), (0, 0), (0, DT_PAD - 2 * SSM_HEADS)))
    dtb = pad_dt(ssm_dt_bias)
    alog = pad_dt(ssm_a_log)
    dsk = jnp.repeat(ssm_d, SSM_HD, axis=1)[:, None, :]
    ssd_consts = _ssd_constants()
    wr_t = jnp.swapaxes(moe_router, 1, 2)
    trr = 1024 if (b * s) % 1024 == 0 else 512
    assert (b * s) % trr == 0
    tidx = np.arange(trr)
    tri = jnp.asarray(tidx[:, None] < tidx[None, :], BF16)
    ffn_g, ffn_u, ffn_d = ffn_w_gate.astype(BF16), ffn_w_up.astype(BF16), ffn_w_down.astype(BF16)
    moe_g, moe_u, moe_d = moe_w_gate.astype(BF16), moe_w_up.astype(BF16), moe_w_down.astype(BF16)

    xs = jnp.concatenate([ctx, x], axis=1)
    for i in range(depth):
        moe = i % 2 == 1
        mod = modtab[i]
        gu, gv, q, k, v, z, xbc, dt = _in_projection(
            xs, mod, norm1_g[i][None, :], w_in_p, i, cos, sin, gq[i], gk[i], gm_v_g[i][None, :], seg, seg64,
            cl=cl, nt=nt)
        gm = _gmlp(gu, gv, ws_cat[i], bs_exp[i])
        att = lax.cond(score_bound[i] <= SAFE_SCORE_BOUND,
                       functools.partial(_attention, cl=cl, fixed_shift=True),
                       functools.partial(_attention, cl=cl, fixed_shift=False),
                       q, k, v, sink_col[i], shift_col[i])
        y = _ssd(xbc, dt, cw[i], cb[i], dtb[i], alog[i], dsk[i], ssd_consts, cl=cl)
        xs, hf = _out_projection(xs, mod, gm, att, y, z, ssm_norm_g[i][None, :], norm2_g[i][None, :], w_out_p, i,
                                 cl=cl, nt=nt, token_tiles=moe)
        j = i // 2
        if moe:
            xs = _moe_ffn(xs, hf, mod, wr_t[j], tri, moe_g, moe_u, moe_d, j, cl=cl, nt=nt_moe)
        else:
            xs = _dense_ffn(xs, hf, mod, ffn_g, ffn_u, ffn_d, j, cl=cl, nt=nt)
    return xs[:, cl:, :]
```

```python
import functools
import math

import numpy as np
import jax
import jax.numpy as jnp
from jax import lax
from jax.experimental import pallas as pl
from jax.experimental.pallas import tpu as pltpu

F32 = jnp.float32
BF16 = jnp.bfloat16
I32 = jnp.int32
HIGHEST = lax.Precision.HIGHEST

D_MODEL = 1024
CHUNK = 128
GM_HEADS, GM_HD, GM_W = 4, 64, 256
ATT_HEADS, KV_HEADS, HEAD_DIM = 8, 2, 64
ATT_W, KV_W = 512, 128
GRID_W = 64
ROPE_BASE = 10000.0
SSM_HEADS, SSM_HD, SSM_W = 4, 64, 256
SSM_STATE, BC_W, CONV_K, XBC_W = 128, 256, 5, 768
D_FF = 2816
N_EXPERTS = 8
EPS = 1e-6
LANES = 128
DT_PAD = 128
C_GU, C_GV, C_Q, C_K, C_V, C_Z, C_XBC, C_DT, IN_WP = 0, 256, 512, 1024, 1152, 1280, 1536, 2304, 2432
FF_BLOCK = 256
EXPERT_TILE = 512
VMEM_LIMIT = 56 * 1024 * 1024
LOG2E = math.log2(math.e)
SAFE_SCORE_BOUND = 40.0


def _cparams(sem):
    return pltpu.CompilerParams(dimension_semantics=sem, vmem_limit_bytes=VMEM_LIMIT)


def _silu(x):
    return x * jax.nn.sigmoid(x)


def _mod_kernel(c_ref, w_ref, b_ref, o_ref):
    cs = _silu(c_ref[...])
    o_ref[0, 0] = jnp.dot(cs, w_ref[0], precision=HIGHEST, preferred_element_type=F32) + b_ref[0, 0]


def _modulation(cvec, w_mod, b_mod):
    depth = w_mod.shape[0]
    r = cvec.shape[0]
    return pl.pallas_call(
        _mod_kernel,
        grid=(depth, 6),
        in_specs=[
            pl.BlockSpec((r, D_MODEL), lambda i, n: (0, 0)),
            pl.BlockSpec((1, D_MODEL, D_MODEL), lambda i, n: (i, 0, n)),
            pl.BlockSpec((1, 1, 1, D_MODEL), lambda i, n: (i, n, 0, 0)),
        ],
        out_specs=pl.BlockSpec((1, 1, r, D_MODEL), lambda i, n: (i, n, 0, 0)),
        out_shape=jax.ShapeDtypeStruct((depth, 6, r, D_MODEL), F32),
        compiler_params=_cparams(("arbitrary", "arbitrary")),
        name="modulation",
    )(cvec, w_mod, b_mod.reshape(depth, 6, 1, D_MODEL))


def _mod_rows(mod_ref, tile_row0, rows, cl, lo):
    ridx = tile_row0 + lax.broadcasted_iota(I32, (rows, 1), 0)
    is_ctx = ridx < cl
    m = mod_ref[0]
    return [jnp.where(is_ctx, m[8 + k:9 + k], m[k:k + 1]) for k in lo]


def _inproj_kernel(x_ref, mod_ref, g1_ref, w_ref, cos_ref, sin_ref, gq_ref, gk_ref, gvg_ref, seg_ref, seg64_ref,
                   gu_ref, gv_ref, q_ref, k_ref, v_ref, z_ref, xbc_ref, dt_ref, *, tr, cl):
    x = x_ref[0]
    ms = jnp.mean(x * x, axis=-1, keepdims=True)
    y = x * lax.rsqrt(ms + EPS) * g1_ref[...]
    shift, scale = _mod_rows(mod_ref, pl.program_id(1) * tr, tr, cl, (0, 1))
    h = (y * (1.0 + scale) + shift).astype(BF16)

    full = jnp.dot(h, w_ref[...], preferred_element_type=F32)

    def proj(a, b):
        return full[:, a:b]

    guv = proj(C_GU, C_Q)
    gu_ref[0] = jax.nn.gelu(guv[:, :GM_W]).astype(BF16)
    gv = jax.nn.gelu(guv[:, GM_W:])
    ssv = jnp.dot((gv * gv).astype(BF16), seg64_ref[...], preferred_element_type=F32)
    gv_ref[0] = (gv * lax.rsqrt(ssv * (1.0 / GM_HD) + EPS) * gvg_ref[...]).astype(BF16)

    cos = cos_ref[...]
    sin = sin_ref[...]

    def norm_rope(t, gain):
        ss = jnp.dot((t * t).astype(BF16), seg_ref[...], preferred_element_type=F32)
        tn = t * lax.rsqrt(ss * (1.0 / HEAD_DIM) + EPS) * gain
        return tn * cos + pltpu.roll(tn, LANES // 2, axis=1) * sin

    qkv = proj(C_Q, C_Z)
    gq = gq_ref[...] * (HEAD_DIM ** -0.5 * LOG2E)
    for c in range(ATT_W // LANES):
        q_ref[0, :, c * LANES:(c + 1) * LANES] = norm_rope(qkv[:, c * LANES:(c + 1) * LANES], gq).astype(BF16)
    k_ref[0] = norm_rope(qkv[:, ATT_W:ATT_W + KV_W], gk_ref[...]).astype(BF16)
    v_ref[0] = qkv[:, ATT_W + KV_W:].astype(BF16)
    z_ref[0] = proj(C_Z, C_XBC).astype(BF16)
    xbc_ref[0] = proj(C_XBC, C_DT).astype(BF16)
    dt_ref[0] = proj(C_DT, IN_WP)


def _layer_block(w, layer):
    return pl.BlockSpec((None,) + w.shape[1:], lambda *_: (layer,) + (0,) * (w.ndim - 1))


def _in_projection(xs, mod, g1, w_in, layer, cos, sin, gq, gk, gvg, seg, seg64, *, cl, nt):
    b, s, _ = xs.shape
    tr = s // nt
    tok = lambda w: pl.BlockSpec((1, tr, w), lambda bi, j: (bi, j, 0))
    full = lambda a: pl.BlockSpec(a.shape, lambda bi, j: (0,) * a.ndim)
    widths = (GM_W, GM_W, ATT_W, KV_W, KV_W, SSM_W, XBC_W)
    return pl.pallas_call(
        functools.partial(_inproj_kernel, tr=tr, cl=cl),
        grid=(b, nt),
        in_specs=[tok(D_MODEL), pl.BlockSpec((1, 16, D_MODEL), lambda bi, j: (bi, 0, 0)), full(g1),
                  _layer_block(w_in, layer),
                  pl.BlockSpec((tr, LANES), lambda bi, j: (j, 0)), pl.BlockSpec((tr, LANES), lambda bi, j: (j, 0)),
                  full(gq), full(gk), full(gvg), full(seg), full(seg64)],
        out_specs=[tok(w) for w in widths] + [tok(DT_PAD)],
        out_shape=[jax.ShapeDtypeStruct((b, s, w), BF16) for w in widths]
        + [jax.ShapeDtypeStruct((b, s, DT_PAD), F32)],
        compiler_params=_cparams(("parallel", "parallel")),
        name="in_projection",
    )(xs, mod, g1, w_in, cos, sin, gq, gk, gvg, seg, seg64)


GMLP_UNROLL = 2


def _gmlp_kernel(gu_ref, gv_ref, ws_ref, bs_ref, o_ref, *, nch):
    head = lax.broadcasted_iota(I32, (CHUNK, GM_W), 1) >> 6

    def body(it, carry):
        for u in range(GMLP_UNROLL):
            r0 = pl.multiple_of((it * GMLP_UNROLL + u) * CHUNK, CHUNK)
            v = gv_ref[0, pl.ds(r0, CHUNK), :]
            vbd = jnp.concatenate([jnp.where(head == hh, v, jnp.zeros_like(v)) for hh in range(GM_HEADS)], axis=0)
            sp = jnp.dot(ws_ref[...], vbd, preferred_element_type=F32) + bs_ref[...]
            o_ref[0, pl.ds(r0, CHUNK), :] = (gu_ref[0, pl.ds(r0, CHUNK), :].astype(F32) * sp).astype(BF16)
        return carry

    lax.fori_loop(0, nch // GMLP_UNROLL, body, 0)


def _gmlp(gu, gv, ws_cat, bs_exp):
    b, s, _ = gu.shape
    seq = pl.BlockSpec((1, s, GM_W), lambda bi: (bi, 0, 0))
    return pl.pallas_call(
        functools.partial(_gmlp_kernel, nch=s // CHUNK),
        grid=(b,),
        in_specs=[seq, seq, pl.BlockSpec(ws_cat.shape, lambda bi: (0, 0)), pl.BlockSpec(bs_exp.shape, lambda bi: (0, 0))],
        out_specs=seq,
        out_shape=jax.ShapeDtypeStruct((b, s, GM_W), BF16),
        compiler_params=_cparams(("parallel",)),
        name="gmlp",
    )(gu, gv, ws_cat, bs_exp)


def _attn_kernel(q_ref, k_ref, v_ref, sink_ref, shift_ref, o_ref, *, nch, cl, fixed_shift):
    ncb = cl // CHUNK
    nslab = ATT_W // LANES
    lane = lax.broadcasted_iota(I32, (CHUNK, LANES), 1)
    first_head = ((lane >> 5) & 1) == 0
    low_half = lane < LANES // 2
    kc = k_ref[0, 0:cl, :]
    vc = v_ref[0, 0:cl, :]
    qi = lax.broadcasted_iota(I32, (CHUNK, CHUNK), 0)
    kl = lax.broadcasted_iota(I32, (CHUNK, CHUNK), 1)
    ninf = jnp.full((CHUNK, CHUNK), -jnp.inf, F32)
    zeros = jnp.zeros((CHUNK, CHUNK), F32)
    bias_prev = jnp.where(kl >= qi, zeros, ninf)
    bias_next = jnp.where(kl <= qi, zeros, ninf)

    def body(n, carry):
        r0 = pl.multiple_of(n * CHUNK, CHUNK)
        rp = pl.multiple_of(jnp.maximum(n - 1, ncb) * CHUNK, CHUNK)
        rn = pl.multiple_of(jnp.minimum(n + 1, nch - 1) * CHUNK, CHUNK)
        q = q_ref[0, pl.ds(r0, CHUNK), :]
        kk = jnp.concatenate([kc, k_ref[0, pl.ds(rp, CHUNK), :], k_ref[0, pl.ds(r0, CHUNK), :],
                              k_ref[0, pl.ds(rn, CHUNK), :]], axis=0)
        vv = jnp.concatenate([vc, v_ref[0, pl.ds(rp, CHUNK), :], v_ref[0, pl.ds(r0, CHUNK), :],
                              v_ref[0, pl.ds(rn, CHUNK), :]], axis=0)
        is_lat = n >= ncb
        bias = jnp.concatenate([
            jnp.where(jnp.logical_and(is_lat, n - 1 >= ncb), bias_prev, ninf),
            jnp.where(is_lat, zeros, ninf),
            jnp.where(jnp.logical_and(is_lat, n + 1 <= nch - 1), bias_next, ninf)], axis=1)
        zero = jnp.zeros((CHUNK, LANES), BF16)
        blocks = []
        for c in range(nslab):
            qc = q[:, c * LANES:(c + 1) * LANES]
            blocks += [jnp.where(first_head, qc, zero), jnp.where(first_head, zero, qc)]
        qs = jnp.concatenate(blocks, axis=0)
        sc = lax.dot_general(qs, kk, (((1,), (1,)), ((), ())), preferred_element_type=F32)
        s_ctx = sc[:, :cl]
        s_loc = sc[:, cl:] + jnp.concatenate([bias] * (2 * nslab), axis=0)
        sk = sink_ref[...]

        def lane_tiles(fn, a, b):
            tiles = [a[:, t * LANES:(t + 1) * LANES] for t in range(a.shape[1] // LANES)]
            tiles += [b[:, t * LANES:(t + 1) * LANES] for t in range(b.shape[1] // LANES)]
            return functools.reduce(fn, tiles)

        if fixed_shift:
            m = shift_ref[...]
        else:
            m = jnp.maximum(jnp.max(lane_tiles(jnp.maximum, s_ctx, s_loc), axis=-1, keepdims=True), sk)
        e_ctx = jnp.exp2(s_ctx - m)
        e_loc = jnp.exp2(s_loc - m)
        p = jnp.concatenate([e_ctx.astype(BF16), e_loc.astype(BF16)], axis=1)
        pv = jnp.dot(p, jnp.concatenate([vv, jnp.ones_like(vv)], axis=1), preferred_element_type=F32)
        o = pv[:, :LANES] * (1.0 / (pv[:, LANES:] + jnp.exp2(sk - m)))
        for c in range(nslab):
            oa = o[(2 * c) * CHUNK:(2 * c + 1) * CHUNK]
            ob = o[(2 * c + 1) * CHUNK:(2 * c + 2) * CHUNK]
            o_ref[0, pl.ds(r0, CHUNK), c * LANES:(c + 1) * LANES] = jnp.where(low_half, oa, ob).astype(BF16)
        return carry

    lax.fori_loop(0, nch, body, 0)


def _attention(q, k, v, sink_col, shift_col, *, cl, fixed_shift):
    b, s, _ = q.shape
    seq = lambda w: pl.BlockSpec((1, s, w), lambda bi: (bi, 0, 0))
    col = pl.BlockSpec(sink_col.shape, lambda bi: (0, 0))
    return pl.pallas_call(
        functools.partial(_attn_kernel, nch=s // CHUNK, cl=cl, fixed_shift=fixed_shift),
        grid=(b,),
        in_specs=[seq(ATT_W), seq(KV_W), seq(KV_W), col, col],
        out_specs=seq(ATT_W),
        out_shape=jax.ShapeDtypeStruct((b, s, ATT_W), BF16),
        compiler_params=_cparams(("parallel",)),
        name="attention_fixed_shift" if fixed_shift else "attention_row_max",
    )(q, k, v, sink_col, shift_col)


SSD_HALO = 16
SSD_CAT = 256
SSD_UNROLL = 2
AC_PIECES = 2


def _ssd_kernel(xbc_ref, dt_ref, cw_ref, cb_ref, dtb_ref, alog_ref, dsk_ref, shf_ref, sel_ref, pair_ref, y_ref,
                xc_ref, act_ref, acp_ref, e3_ref, yb_ref, st_ref, *, nch, cl):
    ncb = cl // CHUNK
    s = nch * CHUNK
    halo = SSD_HALO

    def conv_body(c, carry):
        r0 = pl.multiple_of(c * CHUNK, CHUNK)
        has_prev = jnp.logical_and(c != 0, c != ncb)
        has_next = jnp.logical_and(c != ncb - 1, c != nch - 1)
        rp = pl.multiple_of(jnp.maximum(r0 - halo, 0), halo)
        rn = pl.multiple_of(jnp.minimum(r0 + CHUNK, s - halo), halo)
        zero_halo = jnp.zeros((halo, XBC_W), BF16)
        prev = jnp.where(has_prev, xbc_ref[0, pl.ds(rp, halo), :], zero_halo)
        nxt = jnp.where(has_next, xbc_ref[0, pl.ds(rn, halo), :], zero_halo)
        cur = xbc_ref[0, pl.ds(r0, CHUNK), :]
        cat = jnp.concatenate([prev, cur, nxt, jnp.zeros((SSD_CAT - CHUNK - 2 * halo, XBC_W), BF16)], axis=0)
        sh = jnp.dot(shf_ref[...], cat, preferred_element_type=F32)
        acc = cb_ref[...] + cw_ref[CONV_K // 2:CONV_K // 2 + 1, :] * cur.astype(F32)
        for i, kk in enumerate([k for k in range(CONV_K) if k != CONV_K // 2]):
            acc = acc + cw_ref[kk:kk + 1, :] * sh[i * CHUNK:(i + 1) * CHUNK]
        xc_ref[pl.ds(r0, CHUNK), :] = _silu(acc).astype(BF16)
        dtv = jax.nn.softplus(dt_ref[0, pl.ds(r0, CHUNK), :] + dtb)
        dta = dtv * a_neg
        pre = jnp.dot(tri_f, dta, precision=HIGHEST, preferred_element_type=F32)
        fwd_col = tj < SSM_HEADS
        acum = jnp.where(fwd_col, pre, pre[CHUNK - 1:CHUNK, :] - pre + dta)
        a_end = jnp.where(fwd_col[0:1], acum[CHUNK - 1:CHUNK, :], acum[0:1, :])
        hi = acum.astype(BF16)
        lo = (acum - hi.astype(F32)).astype(BF16)
        acp_ref[pl.ds(r0, CHUNK), :] = jnp.concatenate([hi, lo], axis=1)
        act_ref[pl.ds(r0, CHUNK), :] = (hi.astype(F32) + lo.astype(F32)).T
        e3_ref[pl.ds(r0, CHUNK), :] = jnp.concatenate(
            [jnp.exp(acum), jnp.exp(a_end - acum) * dtv, dtv], axis=1).astype(BF16)
        return carry

    ti = lax.broadcasted_iota(I32, (CHUNK, CHUNK), 0)
    tj = lax.broadcasted_iota(I32, (CHUNK, CHUNK), 1)
    low_half = tj < LANES // 2
    tri_f = (ti >= tj).astype(F32)
    a_neg = -jnp.exp(alog_ref[...])
    dtb = dtb_ref[...]

    def conv_pair(it, carry):
        for u in range(SSD_UNROLL):
            conv_body(it * SSD_UNROLL + u, carry)
        return carry

    lax.fori_loop(0, nch // SSD_UNROLL, conv_pair, 0)

    def chunk_update(c, direction, out_ref, out_lead):
        tri = (ti >= tj) if direction == 0 else (tj >= ti)
        last = CHUNK - 1 if direction == 0 else 0
        r0 = pl.multiple_of(c * CHUNK, CHUNK)
        xcv = xc_ref[pl.ds(r0, CHUNK), :]
        acum_t = act_ref[pl.ds(r0, CHUNK), :]
        acb = jnp.dot(acp_ref[pl.ds(r0, CHUNK), :], sel_ref[direction], preferred_element_type=F32)
        e3 = e3_ref[pl.ds(r0, CHUNK), :]
        prs = jnp.dot(jnp.concatenate([e3[:, i * LANES:(i + 1) * LANES] for i in range(3)], axis=0),
                      pair_ref[direction], preferred_element_type=F32)
        outs = []
        for g in range(2):
            xg = xcv[:, g * LANES:(g + 1) * LANES].astype(F32)
            bg = xcv[:, SSM_W + g * LANES:SSM_W + (g + 1) * LANES]
            cg = xcv[:, SSM_W + BC_W + g * LANES:SSM_W + BC_W + (g + 1) * LANES]
            gram = lax.dot_general(cg, bg, (((1,), (1,)), ((), ())), preferred_element_type=F32)
            eac_p = prs[0:CHUNK, g * LANES:(g + 1) * LANES]
            wend_p = prs[CHUNK:2 * CHUNK, g * LANES:(g + 1) * LANES]
            dt_p = prs[2 * CHUNK:3 * CHUNK, g * LANES:(g + 1) * LANES]
            mixes, dec = [], []
            for hh in range(2):
                h = 2 * g + hh
                col = direction * SSM_HEADS + h
                seg = jnp.exp(jnp.where(tri, acb[:, h * LANES:(h + 1) * LANES] - acum_t[col:col + 1, :], -jnp.inf))
                mixes.append((gram * seg).astype(BF16))
                dec.append(jnp.exp(acum_t[col:col + 1, last:last + 1]))
            xdt = (xg * dt_p).astype(BF16)
            y_intra = jnp.where(low_half, jnp.dot(mixes[0], xdt, preferred_element_type=F32),
                                jnp.dot(mixes[1], xdt, preferred_element_type=F32))
            stg = st_ref[direction, :, g * LANES:(g + 1) * LANES]
            y_inter = jnp.dot(cg, stg.astype(BF16), preferred_element_type=F32) * eac_p
            xw = (xg * wend_p).astype(BF16)
            contrib = lax.dot_general(bg, xw, (((0,), (0,)), ((), ())), preferred_element_type=F32)
            st_ref[direction, :, g * LANES:(g + 1) * LANES] = (
                stg * jnp.where(low_half[0:1], dec[0], dec[1]) + contrib)
            yg = y_intra + y_inter
            if direction == 0:
                yg = yg + dsk_ref[:, g * LANES:(g + 1) * LANES] * xg
            outs.append(yg)
        out_ref[out_lead + (pl.ds(r0, CHUNK), slice(None))] = jnp.concatenate(outs, axis=1)

    st_ref[...] = jnp.zeros_like(st_ref)

    def scan_body(it, carry):
        for u in range(SSD_UNROLL):
            step = it * SSD_UNROLL + u
            chunk_update(step, 0, y_ref, (0,))
            cb = jnp.where(step < ncb, ncb - 1 - step, nch - 1 - (step - ncb))
            chunk_update(cb, 1, yb_ref, ())
        return carry

    lax.fori_loop(0, nch // SSD_UNROLL, scan_body, 0)
    y_ref[0] = y_ref[0] + yb_ref[...]


def _ssd_constants():
    taps = [k for k in range(CONV_K) if k != CONV_K // 2]
    t = np.arange(CHUNK)[:, None]
    r = np.arange(SSD_CAT)[None, :]
    shf = np.concatenate([r == t + SSD_HALO - CONV_K // 2 + k for k in taps], axis=0)
    c = np.arange(LANES)[:, None]
    lane = np.arange(LANES)[None, :]
    sel = np.zeros((2, AC_PIECES * LANES, SSM_HEADS * LANES), bool)
    pair = np.zeros((2, LANES, 2 * LANES), bool)
    for d in range(2):
        for h in range(SSM_HEADS):
            for piece in range(AC_PIECES):
                sel[d, piece * LANES:(piece + 1) * LANES, h * LANES:(h + 1) * LANES] = c == d * SSM_HEADS + h
        for g in range(2):
            pair[d, :, g * LANES:(g + 1) * LANES] = c == d * SSM_HEADS + 2 * g + (lane >= LANES // 2)
    return jnp.asarray(shf, BF16), jnp.asarray(sel, BF16), jnp.asarray(pair, BF16)


def _ssd(xbc, dt, cw, cb, dtb, alog, dsk, consts, *, cl):
    b, s, _ = xbc.shape
    full = lambda a: pl.BlockSpec(a.shape, lambda bi: (0,) * a.ndim)
    return pl.pallas_call(
        functools.partial(_ssd_kernel, nch=s // CHUNK, cl=cl),
        grid=(b,),
        in_specs=[pl.BlockSpec((1, s, XBC_W), lambda bi: (bi, 0, 0)), pl.BlockSpec((1, s, DT_PAD), lambda bi: (bi, 0, 0)),
                  full(cw), full(cb), full(dtb), full(alog), full(dsk)] + [full(a) for a in consts],
        out_specs=pl.BlockSpec((1, s, SSM_W), lambda bi: (bi, 0, 0)),
        out_shape=jax.ShapeDtypeStruct((b, s, SSM_W), F32),
        scratch_shapes=[pltpu.VMEM((s, XBC_W), BF16), pltpu.VMEM((s, DT_PAD), F32),
                        pltpu.VMEM((s, AC_PIECES * LANES), BF16),
                        pltpu.VMEM((s, 3 * LANES), BF16), pltpu.VMEM((s, SSM_W), F32),
                        pltpu.VMEM((2, SSM_STATE, SSM_W), F32)],
        compiler_params=_cparams(("parallel",)),
        name="ssd",
    )(xbc, dt, cw, cb, dtb, alog, dsk, *consts)


TOK_ROWS = D_MODEL // LANES


def _load_tokens(ref, lead, n):
    return jnp.concatenate([ref[lead + (pl.ds(j, n, stride=TOK_ROWS), slice(None))] for j in range(TOK_ROWS)], axis=1)


def _store_tokens(ref, lead, val):
    for j in range(TOK_ROWS):
        ref[lead + (pl.ds(j, val.shape[0], stride=TOK_ROWS), slice(None))] = val[:, j * LANES:(j + 1) * LANES]


def _outproj_kernel(x_ref, mod_ref, gm_ref, att_ref, y_ref, z_ref, ng_ref, g2_ref, w_ref, xo_ref, hf_ref, *, tr, cl,
                    token_tiles):
    gated = y_ref[0] * _silu(z_ref[0].astype(F32))
    ssm = gated * lax.rsqrt(jnp.mean(gated * gated, axis=-1, keepdims=True) + EPS) * ng_ref[...]
    mix = jnp.concatenate([gm_ref[0], att_ref[0], ssm.astype(BF16)], axis=1)
    gate1, shift2, scale2 = _mod_rows(mod_ref, pl.program_id(1) * tr, tr, cl, (2, 3, 4))
    x = x_ref[0] + gate1 * jnp.dot(mix, w_ref[...], preferred_element_type=F32)
    xo_ref[0] = x
    ms = jnp.mean(x * x, axis=-1, keepdims=True)
    hf = x * lax.rsqrt(ms + EPS) * g2_ref[...] * (1.0 + scale2) + shift2
    if token_tiles:
        _store_tokens(hf_ref, (0,), hf)
    else:
        hf_ref[0] = hf.astype(hf_ref.dtype)


def _out_projection(xs, mod, gm, att, y, z, ng, g2, w_out, layer, *, cl, nt, token_tiles):
    b, s, _ = xs.shape
    tr = s // nt
    tok = lambda w: pl.BlockSpec((1, tr, w), lambda bi, j: (bi, j, 0))
    full = lambda a: pl.BlockSpec(a.shape, lambda bi, j: (0,) * a.ndim)
    if token_tiles:
        hf_spec = pl.BlockSpec((1, tr * TOK_ROWS, LANES), lambda bi, j: (bi, j, 0))
        hf_shape = jax.ShapeDtypeStruct((b, s * TOK_ROWS, LANES), F32)
    else:
        hf_spec, hf_shape = tok(D_MODEL), jax.ShapeDtypeStruct((b, s, D_MODEL), BF16)
    return pl.pallas_call(
        functools.partial(_outproj_kernel, tr=tr, cl=cl, token_tiles=token_tiles),
        grid=(b, nt),
        in_specs=[tok(D_MODEL), pl.BlockSpec((1, 16, D_MODEL), lambda bi, j: (bi, 0, 0)), tok(GM_W), tok(ATT_W),
                  tok(SSM_W), tok(SSM_W), full(ng), full(g2), _layer_block(w_out, layer)],
        out_specs=[tok(D_MODEL), hf_spec],
        out_shape=[jax.ShapeDtypeStruct((b, s, D_MODEL), F32), hf_shape],
        compiler_params=_cparams(("parallel", "parallel")),
        name="out_projection",
    )(xs, mod, gm, att, y, z, ng, g2, w_out)


def _swiglu_rows(h, wg_ref, wu_ref, wd_ref, act_ref):
    for f in range(D_FF // FF_BLOCK):
        cols = slice(f * FF_BLOCK, (f + 1) * FF_BLOCK)
        g = jnp.dot(h, wg_ref[0, :, cols], preferred_element_type=F32)
        u = jnp.dot(h, wu_ref[0, :, cols], preferred_element_type=F32)
        act_ref[:, cols] = (_silu(g) * u).astype(BF16)
    return jnp.dot(act_ref[...], wd_ref[0], preferred_element_type=F32)


def _ffn_kernel(x_ref, hf_ref, mod_ref, wg_ref, wu_ref, wd_ref, xo_ref, act_ref, *, tr, cl):
    (gate2,) = _mod_rows(mod_ref, pl.program_id(1) * tr, tr, cl, (5,))
    xo_ref[0] = x_ref[0] + gate2 * _swiglu_rows(hf_ref[0], wg_ref, wu_ref, wd_ref, act_ref)


def _dense_ffn(xs, hf, mod, wg, wu, wd, layer, *, cl, nt):
    b, s, _ = xs.shape
    tr = s // nt
    tok = pl.BlockSpec((1, tr, D_MODEL), lambda bi, j: (bi, j, 0))
    wspec = lambda w: pl.BlockSpec((1,) + w.shape[1:], lambda bi, j: (layer, 0, 0))
    return pl.pallas_call(
        functools.partial(_ffn_kernel, tr=tr, cl=cl),
        grid=(b, nt),
        in_specs=[tok, tok, pl.BlockSpec((1, 16, D_MODEL), lambda bi, j: (bi, 0, 0)), wspec(wg), wspec(wu), wspec(wd)],
        out_specs=tok,
        out_shape=jax.ShapeDtypeStruct((b, s, D_MODEL), F32),
        scratch_shapes=[pltpu.VMEM((tr, D_FF), BF16)],
        compiler_params=_cparams(("parallel", "parallel")),
        name="dense_ffn",
    )(xs, hf, mod, wg, wu, wd)


def _router_kernel(hf_ref, wr_ref, tri_ref, mi_ref, mf_ref, cnt_ref, run_ref, *, trc):
    @pl.when(pl.program_id(0) == 0)
    def _():
        run_ref[...] = jnp.zeros_like(run_ref)

    x = _load_tokens(hf_ref, (), trc)
    w = wr_ref[...]
    xh = x.astype(BF16)
    xl = (x - xh.astype(F32)).astype(BF16)
    wh = w.astype(BF16)
    wl = (w - wh.astype(F32)).astype(BF16)
    nt_dims = (((1,), (1,)), ((), ()))
    both = lax.dot_general(jnp.concatenate([wh, wl], axis=0), xh, nt_dims, preferred_element_type=F32)
    logits = (both[:N_EXPERTS] + both[N_EXPERTS:]
              + lax.dot_general(wh, xl, nt_dims, preferred_element_type=F32))
    eidx = lax.broadcasted_iota(I32, (N_EXPERTS, trc), 0).astype(F32)
    m1 = jnp.max(logits, axis=0, keepdims=True)
    i1 = jnp.min(jnp.where(logits == m1, eidx, float(N_EXPERTS)), axis=0, keepdims=True)
    rest = jnp.where(eidx == i1, -jnp.inf, logits)
    m2 = jnp.max(rest, axis=0, keepdims=True)
    i2 = jnp.min(jnp.where(rest == m2, eidx, float(N_EXPERTS)), axis=0, keepdims=True)
    e2 = jnp.exp(m2 - m1)
    g1 = 1.0 / (1.0 + e2)
    g2 = e2 / (1.0 + e2)
    oh1 = (eidx == i1).astype(F32)
    oh2 = (eidx == i2).astype(F32)
    sel = oh1 + oh2
    before = jnp.dot(sel.astype(BF16), tri_ref[...], preferred_element_type=F32) + run_ref[:, 0:1]
    r1 = jnp.sum(oh1 * before, axis=0, keepdims=True)
    r2 = jnp.sum(oh2 * before, axis=0, keepdims=True)
    run_ref[...] = run_ref[...] + jnp.sum(sel, axis=1, keepdims=True)
    zi = jnp.zeros((4, trc), I32)
    mi_ref[0] = jnp.concatenate([i1.astype(I32), i2.astype(I32), r1.astype(I32), r2.astype(I32), zi], axis=0)
    mf_ref[0] = jnp.concatenate([g1, g2, jnp.zeros((6, trc), F32)], axis=0)
    cnt_ref[...] = run_ref[...].astype(I32)


def _router(hf_tiles, wr_t, tri, *, trc):
    t = hf_tiles.shape[0] // TOK_ROWS
    nt = t // trc
    meta = pl.BlockSpec((1, 8, trc), lambda i: (i, 0, 0))
    return pl.pallas_call(
        functools.partial(_router_kernel, trc=trc),
        grid=(nt,),
        in_specs=[pl.BlockSpec((trc * TOK_ROWS, LANES), lambda i: (i, 0)), pl.BlockSpec(wr_t.shape, lambda i: (0, 0)),
                  pl.BlockSpec(tri.shape, lambda i: (0, 0))],
        out_specs=[meta, meta, pl.BlockSpec((N_EXPERTS, LANES), lambda i: (0, 0))],
        out_shape=[jax.ShapeDtypeStruct((nt, 8, trc), I32), jax.ShapeDtypeStruct((nt, 8, trc), F32),
                   jax.ShapeDtypeStruct((N_EXPERTS, LANES), I32)],
        scratch_shapes=[pltpu.VMEM((N_EXPERTS, LANES), F32)],
        compiler_params=_cparams(("arbitrary",)),
        name="moe_router",
    )(hf_tiles, wr_t, tri)


DMA_UNROLL = 8


def _token_copy(src_ref, src_tok, dst_ref, dst_tok, sem):
    src = src_ref.at[pl.ds(pl.multiple_of(src_tok * TOK_ROWS, TOK_ROWS), TOK_ROWS), :]
    dst = dst_ref.at[pl.ds(pl.multiple_of(dst_tok * TOK_ROWS, TOK_ROWS), TOK_ROWS), :]
    return pltpu.make_async_copy(src, dst, sem)


def _for_tokens(n, fn):
    def block(i, carry):
        for u in range(DMA_UNROLL):
            fn(i * DMA_UNROLL + u)
        return carry

    lax.fori_loop(0, n // DMA_UNROLL, block, 0)


def _dispatch_kernel(pad_lo_ref, pad_n_ref, nact_ref, slot_ref, hf_ref, xs_ref, zero_ref, sem, zsem, *, trc, ntile):
    @pl.when(pl.program_id(0) == 0)
    def _():
        zero_ref[...] = jnp.zeros_like(zero_ref)
        tile_rows = EXPERT_TILE * TOK_ROWS

        def tile_copy(i):
            return pltpu.make_async_copy(
                zero_ref, xs_ref.at[pl.ds(pl.multiple_of(i * tile_rows, tile_rows), tile_rows), :], zsem)

        def fill_tile(i, carry):
            tile_copy(i).start()
            return carry

        def drain_tile(i, carry):
            tile_copy(i).wait()
            return carry

        lax.fori_loop(nact_ref[0], ntile, fill_tile, 0)
        lax.fori_loop(nact_ref[0], ntile, drain_tile, 0)
        for e in range(N_EXPERTS):
            lo = pad_lo_ref[e]
            n = pad_n_ref[e]

            def fill(r, carry):
                _token_copy(zero_ref, 0, xs_ref, lo + r, zsem).start()
                return carry

            def drain(r, carry):
                _token_copy(zero_ref, 0, xs_ref, lo + r, zsem).wait()
                return carry

            lax.fori_loop(0, n, fill, 0)
            lax.fori_loop(0, n, drain, 0)

    copies = lambda r: [_token_copy(hf_ref, r, xs_ref, slot_ref[0, k, r], sem) for k in range(2)]
    _for_tokens(trc, lambda r: [cp.start() for cp in copies(r)])
    _for_tokens(trc, lambda r: [cp.wait() for cp in copies(r)])


def _dispatch(pad_lo, pad_n, nact, slots, hf_tiles, nslot, *, trc):
    t = hf_tiles.shape[0] // TOK_ROWS
    nt = t // trc
    grid_spec = pltpu.PrefetchScalarGridSpec(
        num_scalar_prefetch=3,
        grid=(nt,),
        in_specs=[pl.BlockSpec((1, 2, trc), lambda i, lo, n, na: (i, 0, 0), memory_space=pltpu.SMEM),
                  pl.BlockSpec((trc * TOK_ROWS, LANES), lambda i, lo, n, na: (i, 0))],
        out_specs=pl.BlockSpec(memory_space=pl.ANY),
        scratch_shapes=[pltpu.VMEM((EXPERT_TILE * TOK_ROWS, LANES), F32), pltpu.SemaphoreType.DMA(()),
                        pltpu.SemaphoreType.DMA(())],
    )
    return pl.pallas_call(
        functools.partial(_dispatch_kernel, trc=trc, ntile=nslot // EXPERT_TILE),
        grid_spec=grid_spec,
        out_shape=jax.ShapeDtypeStruct((nslot * TOK_ROWS, LANES), F32),
        compiler_params=_cparams(("arbitrary",)),
        name="moe_dispatch",
    )(pad_lo, pad_n, nact, slots, hf_tiles)


def _expert_kernel(texp_ref, nact_ref, xs_ref, wg_ref, wu_ref, wd_ref, ys_ref, act_ref):
    active = pl.program_id(0) < nact_ref[0]

    @pl.when(active)
    def _():
        h = _load_tokens(xs_ref, (), EXPERT_TILE).astype(BF16)
        _store_tokens(ys_ref, (), _swiglu_rows(h, wg_ref, wu_ref, wd_ref, act_ref))

    @pl.when(jnp.logical_not(active))
    def _():
        ys_ref[...] = jnp.zeros_like(ys_ref)


def _expert_ffn(tile_expert, nact, xs, wg, wu, wd, layer):
    ntile = xs.shape[0] // (EXPERT_TILE * TOK_ROWS)
    wspec = lambda shp: pl.BlockSpec((None, 1) + shp, lambda i, te, na: (layer, te[i], 0, 0))
    tile = pl.BlockSpec((EXPERT_TILE * TOK_ROWS, LANES), lambda i, te, na: (i, 0))
    grid_spec = pltpu.PrefetchScalarGridSpec(
        num_scalar_prefetch=2,
        grid=(ntile,),
        in_specs=[tile, wspec((D_MODEL, D_FF)), wspec((D_MODEL, D_FF)), wspec((D_FF, D_MODEL))],
        out_specs=tile,
        scratch_shapes=[pltpu.VMEM((EXPERT_TILE, D_FF), BF16)],
    )
    return pl.pallas_call(
        _expert_kernel,
        grid_spec=grid_spec,
        out_shape=jax.ShapeDtypeStruct(xs.shape, F32),
        compiler_params=_cparams(("arbitrary",)),
        name="moe_experts",
    )(tile_expert, nact, xs, wg, wu, wd)


def _combine_kernel(slot_ref, x_ref, gate_ref, mod_ref, ys_ref, xo_ref, buf_ref, sem, *, trc, nt, cl):
    copies = lambda r: [_token_copy(ys_ref, slot_ref[0, k, r], buf_ref.at[k], r, sem) for k in range(2)]
    _for_tokens(trc, lambda r: [cp.start() for cp in copies(r)])
    _for_tokens(trc, lambda r: [cp.wait() for cp in copies(r)])
    (gate2,) = _mod_rows(mod_ref, (pl.program_id(0) % nt) * trc, trc, cl, (5,))
    gt = gate_ref[0]
    f = gt[:, 0:1] * _load_tokens(buf_ref, (0,), trc) + gt[:, 1:2] * _load_tokens(buf_ref, (1,), trc)
    xo_ref[...] = x_ref[...] + gate2 * f


def _combine(slots, x_flat, gates_t, mod, ys, *, trc, nt, cl):
    t = x_flat.shape[0]
    ntile = t // trc
    tok = pl.BlockSpec((trc, D_MODEL), lambda i: (i, 0))
    return pl.pallas_call(
        functools.partial(_combine_kernel, trc=trc, nt=nt, cl=cl),
        grid=(ntile,),
        in_specs=[pl.BlockSpec((1, 2, trc), lambda i: (i, 0, 0), memory_space=pltpu.SMEM), tok,
                  pl.BlockSpec((1, trc, 2), lambda i: (i, 0, 0)),
                  pl.BlockSpec((1, 16, D_MODEL), lambda i: (i // nt, 0, 0)), pl.BlockSpec(memory_space=pl.ANY)],
        out_specs=tok,
        out_shape=jax.ShapeDtypeStruct((t, D_MODEL), F32),
        scratch_shapes=[pltpu.VMEM((2, trc * TOK_ROWS, LANES), F32), pltpu.SemaphoreType.DMA(())],
        compiler_params=_cparams(("arbitrary",)),
        name="moe_combine",
    )(slots, x_flat, gates_t, mod, ys)


def _moe_ffn(xs, hf, mod, wr_t, tri, wg, wu, wd, layer, *, cl, nt):
    b, s, _ = xs.shape
    t = b * s
    trr = tri.shape[0]
    trc = s // nt
    hf_flat = hf.reshape(t * TOK_ROWS, LANES)
    meta_i, meta_f, counts = _router(hf_flat, wr_t, tri, trc=trr)
    counts = counts[:, 0]
    padded = (counts + EXPERT_TILE - 1) // EXPERT_TILE * EXPERT_TILE
    ends = jnp.cumsum(padded)
    starts = ends - padded
    eid = meta_i[:, 0:2, :]
    group_start = sum(jnp.where(eid == e, starts[e], 0) for e in range(N_EXPERTS))
    slots = group_start + meta_i[:, 2:4, :]
    nslot = 2 * t + N_EXPERTS * EXPERT_TILE
    ntile = nslot // EXPERT_TILE
    nact = (ends[-1] // EXPERT_TILE).astype(I32)
    tile_lo = jnp.minimum(jnp.arange(ntile, dtype=I32), nact - 1) * EXPERT_TILE
    tile_expert = jnp.minimum(jnp.sum(tile_lo[:, None] >= ends[None, :], axis=1), N_EXPERTS - 1).astype(I32)
    nact = nact.reshape(1)
    xs_sorted = _dispatch((starts + counts).astype(I32), (padded - counts).astype(I32), nact, slots, hf_flat, nslot,
                          trc=trr)
    ys = _expert_ffn(tile_expert, nact, xs_sorted, wg, wu, wd, layer)
    retile = lambda a: jnp.swapaxes(jnp.swapaxes(a, 0, 1).reshape(2, t // trc, trc), 0, 1)
    gates_t = jnp.swapaxes(retile(meta_f[:, 0:2, :]), 1, 2)
    out = _combine(retile(slots), xs.reshape(t, D_MODEL), gates_t, mod, ys, trc=trc, nt=nt, cl=cl)
    return out.reshape(b, s, D_MODEL)


def _slab_perm(t, lead, xp=jnp):
    nh = t.shape[-1] // HEAD_DIM
    per = nh // 2
    t = t.reshape(lead + (2, per, 2, 2, 16))
    t = xp.moveaxis(t, (-5, -4, -3, -2, -1), (-3, -5, -2, -4, -1))
    return t.reshape(lead + (nh * HEAD_DIM,))


def _rope_tables(s, cl):
    l = s - cl
    pos = np.arange(l)
    lane = np.arange(LANES)
    i = lane % 32
    freq = ROPE_BASE ** (-(i % 16).astype(np.float32) / 16.0)
    p = np.where(i[None, :] < 16, (pos // GRID_W)[:, None], (pos % GRID_W)[:, None]).astype(np.float32)
    ang = p * freq[None, :].astype(np.float32)
    sign = np.where(lane < LANES // 2, -1.0, 1.0).astype(np.float32)
    cos = np.concatenate([np.ones((cl, LANES), np.float32), np.cos(ang)], axis=0)
    sin = np.concatenate([np.zeros((cl, LANES), np.float32), np.sin(ang) * sign[None, :]], axis=0)
    return jnp.asarray(cos, F32), jnp.asarray(sin, F32)


def kernel(x, c, ctx, c_ctx, w_mod, b_mod, norm1_g, norm2_g, w_in, w_out, gm_v_g, gm_ws, gm_bs, att_q_g, att_k_g, att_sink, ssm_conv_w, ssm_conv_b, ssm_dt_bias, ssm_a_log, ssm_d, ssm_norm_g, ffn_w_gate, ffn_w_up, ffn_w_down, moe_router, moe_w_gate, moe_w_up, moe_w_down):
    b, l, _ = x.shape
    cl = ctx.shape[1]
    s = cl + l
    depth = w_mod.shape[0]
    nt = 8
    nt_moe = 4
    assert s % (nt * 16) == 0 and cl % CHUNK == 0 and l % CHUNK == 0 and l % GRID_W == 0 and b < 16
    assert (s // CHUNK) % SSD_UNROLL == 0 and (s // CHUNK) % GMLP_UNROLL == 0

    cvec = jnp.concatenate([c, c_ctx[None, :], jnp.zeros((16 - b - 1, D_MODEL), F32)], axis=0)
    mods = _modulation(cvec, w_mod, b_mod)
    lat = jnp.moveaxis(mods[:, :, :b, :], 2, 1)
    con = jnp.broadcast_to(mods[:, None, :, b, :], (depth, b, 6, D_MODEL))
    pad2 = jnp.zeros((depth, b, 2, D_MODEL), F32)
    modtab = jnp.concatenate([lat, pad2, con, pad2], axis=2)

    wi = w_in.astype(BF16)
    src = np.concatenate([_slab_perm(np.arange(ATT_W), (), np), ATT_W + _slab_perm(np.arange(KV_W), (), np)])
    perm = jnp.asarray(np.arange(ATT_W + KV_W)[:, None] == src[None, :], BF16)
    w_in_p = jnp.concatenate([
        wi[:, :, 0:2 * GM_W],
        jnp.matmul(wi[:, :, 512:1152], perm),
        wi[:, :, 1152:2304],
        jnp.pad(wi[:, :, 2304:2312], ((0, 0), (0, 0), (0, DT_PAD - 2 * SSM_HEADS))),
    ], axis=2)
    wo_att = w_out[:, GM_W:GM_W + ATT_W, :].reshape(depth, 2, 4, HEAD_DIM, D_MODEL)
    wo_att = jnp.swapaxes(wo_att, 1, 2).reshape(depth, ATT_W, D_MODEL)
    w_out_p = jnp.concatenate([w_out[:, :GM_W], wo_att, w_out[:, GM_W + ATT_W:]], axis=1).astype(BF16)
    gq = _slab_perm(jnp.tile(att_q_g, (1, 2)), (depth,))[:, None, :]
    gk = _slab_perm(jnp.tile(att_k_g, (1, 2)), (depth,))[:, None, :]
    lane = np.arange(LANES)
    seg = jnp.asarray(((lane[:, None] // 32) % 2 == (lane[None, :] // 32) % 2), BF16)
    lane2 = np.arange(GM_W)
    seg64 = jnp.asarray(lane2[:, None] // GM_HD == lane2[None, :] // GM_HD, BF16)
    cos, sin = _rope_tables(s, cl)
    ws_cat = jnp.swapaxes(gm_ws, 1, 2).reshape(depth, CHUNK, GM_HEADS * CHUNK).astype(BF16)
    bs_exp = jnp.repeat(jnp.swapaxes(gm_bs, 1, 2), GM_HD, axis=2)
    sink_heads = att_sink.reshape(depth, 2, 4).swapaxes(1, 2).reshape(depth, ATT_HEADS)
    sink_col = jnp.repeat(sink_heads, CHUNK, axis=1)[:, :, None] * LOG2E
    score_bound = (8.0 * (1.0 + 2.0 ** -6)) * jnp.max(jnp.abs(att_q_g), axis=1) * jnp.max(jnp.abs(att_k_g), axis=1)
    shift_col = jnp.maximum(sink_col, score_bound[:, None, None] * LOG2E)
    cw = jnp.pad(ssm_conv_w, ((0, 0), (0, 8 - CONV_K), (0, 0)))
    cb = ssm_conv_b[:, None, :]
    pad_dt = lambda t: jnp.pad(t.reshape(depth, 1, 2 * SSM_HEADS), ((0, 0), (0, 0), (0, DT_PAD - 2 * SSM_HEADS)))
    dtb = pad_dt(ssm_dt_bias)
    alog = pad_dt(ssm_a_log)
    dsk = jnp.repeat(ssm_d, SSM_HD, axis=1)[:, None, :]
    ssd_consts = _ssd_constants()
    wr_t = jnp.swapaxes(moe_router, 1, 2)
    trr = 1024 if (b * s) % 1024 == 0 else 512
    assert (b * s) % trr == 0
    tidx = np.arange(trr)
    tri = jnp.asarray(tidx[:, None] < tidx[None, :], BF16)
    ffn_g, ffn_u, ffn_d = ffn_w_gate.astype(BF16), ffn_w_up.astype(BF16), ffn_w_down.astype(BF16)
    moe_g, moe_u, moe_d = moe_w_gate.astype(BF16), moe_w_up.astype(BF16), moe_w_down.astype(BF16)

    xs = jnp.concatenate([ctx, x], axis=1)
    for i in range(depth):
        moe = i % 2 == 1
        mod = modtab[i]
        gu, gv, q, k, v, z, xbc, dt = _in_projection(
            xs, mod, norm1_g[i][None, :], w_in_p, i, cos, sin, gq[i], gk[i], gm_v_g[i][None, :], seg, seg64,
            cl=cl, nt=nt)
        gm = _gmlp(gu, gv, ws_cat[i], bs_exp[i])
        att = lax.cond(score_bound[i] <= SAFE_SCORE_BOUND,
                       functools.partial(_attention, cl=cl, fixed_shift=True),
                       functools.partial(_attention, cl=cl, fixed_shift=False),
                       q, k, v, sink_col[i], shift_col[i])
        y = _ssd(xbc, dt, cw[i], cb[i], dtb[i], alog[i], dsk[i], ssd_consts, cl=cl)
        xs, hf = _out_projection(xs, mod, gm, att, y, z, ssm_norm_g[i][None, :], norm2_g[i][None, :], w_out_p, i,
                                 cl=cl, nt=nt, token_tiles=moe)
        j = i // 2
        if moe:
            xs = _moe_ffn(xs, hf, mod, wr_t[j], tri, moe_g, moe_u, moe_d, j, cl=cl, nt=nt_moe)
        else:
            xs = _dense_ffn(xs, hf, mod, ffn_g, ffn_u, ffn_d, j, cl=cl, nt=nt)
    return xs[:, cl:, :]
```

```python
import functools
import math

import numpy as np
import jax
import jax.numpy as jnp
from jax import lax
from jax.experimental import pallas as pl
from jax.experimental.pallas import tpu as pltpu

F32 = jnp.float32
BF16 = jnp.bfloat16
I32 = jnp.int32
HIGHEST = lax.Precision.HIGHEST

D_MODEL = 1024
CHUNK = 128
GM_HEADS, GM_HD, GM_W = 4, 64, 256
ATT_HEADS, KV_HEADS, HEAD_DIM = 8, 2, 64
ATT_W, KV_W = 512, 128
GRID_W = 64
ROPE_BASE = 10000.0
SSM_HEADS, SSM_HD, SSM_W = 4, 64, 256
SSM_STATE, BC_W, CONV_K, XBC_W = 128, 256, 5, 768
D_FF = 2816
N_EXPERTS = 8
EPS = 1e-6
LANES = 128
DT_PAD = 128
C_GU, C_GV, C_Q, C_K, C_V, C_Z, C_XBC, C_DT, IN_WP = 0, 256, 512, 1024, 1152, 1280, 1536, 2304, 2432
FF_BLOCK = 256
EXPERT_TILE = 512
VMEM_LIMIT = 56 * 1024 * 1024
LOG2E = math.log2(math.e)
SAFE_SCORE_BOUND = 40.0


def _cparams(sem):
    return pltpu.CompilerParams(dimension_semantics=sem, vmem_limit_bytes=VMEM_LIMIT)


def _silu(x):
    return x * jax.nn.sigmoid(x)


def _mod_kernel(c_ref, w_ref, b_ref, o_ref):
    cs = _silu(c_ref[...])
    o_ref[0, 0] = jnp.dot(cs, w_ref[0], precision=HIGHEST, preferred_element_type=F32) + b_ref[0, 0]


def _modulation(cvec, w_mod, b_mod):
    depth = w_mod.shape[0]
    r = cvec.shape[0]
    return pl.pallas_call(
        _mod_kernel,
        grid=(depth, 6),
        in_specs=[
            pl.BlockSpec((r, D_MODEL), lambda i, n: (0, 0)),
            pl.BlockSpec((1, D_MODEL, D_MODEL), lambda i, n: (i, 0, n)),
            pl.BlockSpec((1, 1, 1, D_MODEL), lambda i, n: (i, n, 0, 0)),
        ],
        out_specs=pl.BlockSpec((1, 1, r, D_MODEL), lambda i, n: (i, n, 0, 0)),
        out_shape=jax.ShapeDtypeStruct((depth, 6, r, D_MODEL), F32),
        compiler_params=_cparams(("arbitrary", "arbitrary")),
        name="modulation",
    )(cvec, w_mod, b_mod.reshape(depth, 6, 1, D_MODEL))


def _mod_rows(mod_ref, tile_row0, rows, cl, lo):
    ridx = tile_row0 + lax.broadcasted_iota(I32, (rows, 1), 0)
    is_ctx = ridx < cl
    m = mod_ref[0]
    return [jnp.where(is_ctx, m[8 + k:9 + k], m[k:k + 1]) for k in lo]


def _inproj_kernel(x_ref, mod_ref, g1_ref, w_ref, cos_ref, sin_ref, gq_ref, gk_ref, gvg_ref, seg_ref, seg64_ref,
                   gu_ref, gv_ref, q_ref, k_ref, v_ref, z_ref, xbc_ref, dt_ref, *, tr, cl):
    x = x_ref[0]
    ms = jnp.mean(x * x, axis=-1, keepdims=True)
    y = x * lax.rsqrt(ms + EPS) * g1_ref[...]
    shift, scale = _mod_rows(mod_ref, pl.program_id(1) * tr, tr, cl, (0, 1))
    h = (y * (1.0 + scale) + shift).astype(BF16)

    full = jnp.dot(h, w_ref[...], preferred_element_type=F32)

    def proj(a, b):
        return full[:, a:b]

    guv = proj(C_GU, C_Q)
    gu_ref[0] = jax.nn.gelu(guv[:, :GM_W]).astype(BF16)
    gv = jax.nn.gelu(guv[:, GM_W:])
    ssv = jnp.dot((gv * gv).astype(BF16), seg64_ref[...], preferred_element_type=F32)
    gv_ref[0] = (gv * lax.rsqrt(ssv * (1.0 / GM_HD) + EPS) * gvg_ref[...]).astype(BF16)

    cos = cos_ref[...]
    sin = sin_ref[...]

    def norm_rope(t, gain):
        ss = jnp.dot((t * t).astype(BF16), seg_ref[...], preferred_element_type=F32)
        tn = t * lax.rsqrt(ss * (1.0 / HEAD_DIM) + EPS) * gain
        return tn * cos + pltpu.roll(tn, LANES // 2, axis=1) * sin

    qkv = proj(C_Q, C_Z)
    gq = gq_ref[...] * (HEAD_DIM ** -0.5 * LOG2E)
    for c in range(ATT_W // LANES):
        q_ref[0, :, c * LANES:(c + 1) * LANES] = norm_rope(qkv[:, c * LANES:(c + 1) * LANES], gq).astype(BF16)
    k_ref[0] = norm_rope(qkv[:, ATT_W:ATT_W + KV_W], gk_ref[...]).astype(BF16)
    v_ref[0] = qkv[:, ATT_W + KV_W:].astype(BF16)
    z_ref[0] = proj(C_Z, C_XBC).astype(BF16)
    xbc_ref[0] = proj(C_XBC, C_DT).astype(BF16)
    dt_ref[0] = proj(C_DT, IN_WP)


def _layer_block(w, layer):
    return pl.BlockSpec((None,) + w.shape[1:], lambda *_: (layer,) + (0,) * (w.ndim - 1))


def _in_projection(xs, mod, g1, w_in, layer, cos, sin, gq, gk, gvg, seg, seg64, *, cl, nt):
    b, s, _ = xs.shape
    tr = s // nt
    tok = lambda w: pl.BlockSpec((1, tr, w), lambda bi, j: (bi, j, 0))
    full = lambda a: pl.BlockSpec(a.shape, lambda bi, j: (0,) * a.ndim)
    widths = (GM_W, GM_W, ATT_W, KV_W, KV_W, SSM_W, XBC_W)
    return pl.pallas_call(
        functools.partial(_inproj_kernel, tr=tr, cl=cl),
        grid=(b, nt),
        in_specs=[tok(D_MODEL), pl.BlockSpec((1, 16, D_MODEL), lambda bi, j: (bi, 0, 0)), full(g1),
                  _layer_block(w_in, layer),
                  pl.BlockSpec((tr, LANES), lambda bi, j: (j, 0)), pl.BlockSpec((tr, LANES), lambda bi, j: (j, 0)),
                  full(gq), full(gk), full(gvg), full(seg), full(seg64)],
        out_specs=[tok(w) for w in widths] + [tok(DT_PAD)],
        out_shape=[jax.ShapeDtypeStruct((b, s, w), BF16) for w in widths]
        + [jax.ShapeDtypeStruct((b, s, DT_PAD), F32)],
        compiler_params=_cparams(("parallel", "parallel")),
        name="in_projection",
    )(xs, mod, g1, w_in, cos, sin, gq, gk, gvg, seg, seg64)


GMLP_UNROLL = 2


def _gmlp_kernel(gu_ref, gv_ref, ws_ref, bs_ref, o_ref, *, nch):
    head = lax.broadcasted_iota(I32, (CHUNK, GM_W), 1) >> 6

    def body(it, carry):
        for u in range(GMLP_UNROLL):
            r0 = pl.multiple_of((it * GMLP_UNROLL + u) * CHUNK, CHUNK)
            v = gv_ref[0, pl.ds(r0, CHUNK), :]
            vbd = jnp.concatenate([jnp.where(head == hh, v, jnp.zeros_like(v)) for hh in range(GM_HEADS)], axis=0)
            sp = jnp.dot(ws_ref[...], vbd, preferred_element_type=F32) + bs_ref[...]
            o_ref[0, pl.ds(r0, CHUNK), :] = (gu_ref[0, pl.ds(r0, CHUNK), :].astype(F32) * sp).astype(BF16)
        return carry

    lax.fori_loop(0, nch // GMLP_UNROLL, body, 0)


def _gmlp(gu, gv, ws_cat, bs_exp):
    b, s, _ = gu.shape
    seq = pl.BlockSpec((1, s, GM_W), lambda bi: (bi, 0, 0))
    return pl.pallas_call(
        functools.partial(_gmlp_kernel, nch=s // CHUNK),
        grid=(b,),
        in_specs=[seq, seq, pl.BlockSpec(ws_cat.shape, lambda bi: (0, 0)), pl.BlockSpec(bs_exp.shape, lambda bi: (0, 0))],
        out_specs=seq,
        out_shape=jax.ShapeDtypeStruct((b, s, GM_W), BF16),
        compiler_params=_cparams(("parallel",)),
        name="gmlp",
    )(gu, gv, ws_cat, bs_exp)


ATT_UNROLL = 2


def _attn_kernel(q_ref, k_ref, v_ref, sink_ref, shift_ref, o_ref, *, nch, cl, fixed_shift):
    ncb = cl // CHUNK
    nslab = ATT_W // LANES
    lane = lax.broadcasted_iota(I32, (CHUNK, LANES), 1)
    first_head = ((lane >> 5) & 1) == 0
    low_half = lane < LANES // 2
    kc = k_ref[0, 0:cl, :]
    vc = v_ref[0, 0:cl, :]
    qi = lax.broadcasted_iota(I32, (CHUNK, CHUNK), 0)
    kl = lax.broadcasted_iota(I32, (CHUNK, CHUNK), 1)
    ninf = jnp.full((CHUNK, CHUNK), -jnp.inf, F32)
    zeros = jnp.zeros((CHUNK, CHUNK), F32)
    bias_prev = jnp.where(kl >= qi, zeros, ninf)
    bias_next = jnp.where(kl <= qi, zeros, ninf)

    def body(n, carry):
        r0 = pl.multiple_of(n * CHUNK, CHUNK)
        rp = pl.multiple_of(jnp.maximum(n - 1, ncb) * CHUNK, CHUNK)
        rn = pl.multiple_of(jnp.minimum(n + 1, nch - 1) * CHUNK, CHUNK)
        q = q_ref[0, pl.ds(r0, CHUNK), :]
        kk = jnp.concatenate([kc, k_ref[0, pl.ds(rp, CHUNK), :], k_ref[0, pl.ds(r0, CHUNK), :],
                              k_ref[0, pl.ds(rn, CHUNK), :]], axis=0)
        vv = jnp.concatenate([vc, v_ref[0, pl.ds(rp, CHUNK), :], v_ref[0, pl.ds(r0, CHUNK), :],
                              v_ref[0, pl.ds(rn, CHUNK), :]], axis=0)
        is_lat = n >= ncb
        bias = jnp.concatenate([
            jnp.where(jnp.logical_and(is_lat, n - 1 >= ncb), bias_prev, ninf),
            jnp.where(is_lat, zeros, ninf),
            jnp.where(jnp.logical_and(is_lat, n + 1 <= nch - 1), bias_next, ninf)], axis=1)
        zero = jnp.zeros((CHUNK, LANES), BF16)
        blocks = []
        for c in range(nslab):
            qc = q[:, c * LANES:(c + 1) * LANES]
            blocks += [jnp.where(first_head, qc, zero), jnp.where(first_head, zero, qc)]
        qs = jnp.concatenate(blocks, axis=0)
        sc = lax.dot_general(qs, kk, (((1,), (1,)), ((), ())), preferred_element_type=F32)
        s_ctx = sc[:, :cl]
        s_loc = sc[:, cl:] + jnp.concatenate([bias] * (2 * nslab), axis=0)
        sk = sink_ref[...]

        def lane_tiles(fn, a, b):
            tiles = [a[:, t * LANES:(t + 1) * LANES] for t in range(a.shape[1] // LANES)]
            tiles += [b[:, t * LANES:(t + 1) * LANES] for t in range(b.shape[1] // LANES)]
            return functools.reduce(fn, tiles)

        if fixed_shift:
            m = shift_ref[...]
        else:
            m = jnp.maximum(jnp.max(lane_tiles(jnp.maximum, s_ctx, s_loc), axis=-1, keepdims=True), sk)
        e_ctx = jnp.exp2(s_ctx - m)
        e_loc = jnp.exp2(s_loc - m)
        p = jnp.concatenate([e_ctx.astype(BF16), e_loc.astype(BF16)], axis=1)
        pv = jnp.dot(p, jnp.concatenate([vv, jnp.ones_like(vv)], axis=1), preferred_element_type=F32)
        o = pv[:, :LANES] * (1.0 / (pv[:, LANES:] + jnp.exp2(sk - m)))
        for c in range(nslab):
            oa = o[(2 * c) * CHUNK:(2 * c + 1) * CHUNK]
            ob = o[(2 * c + 1) * CHUNK:(2 * c + 2) * CHUNK]
            o_ref[0, pl.ds(r0, CHUNK), c * LANES:(c + 1) * LANES] = jnp.where(low_half, oa, ob).astype(BF16)
        return carry

    def body_pair(it, carry):
        for u in range(ATT_UNROLL):
            body(it * ATT_UNROLL + u, carry)
        return carry

    lax.fori_loop(0, nch // ATT_UNROLL, body_pair, 0)


def _attention(q, k, v, sink_col, shift_col, *, cl, fixed_shift):
    b, s, _ = q.shape
    seq = lambda w: pl.BlockSpec((1, s, w), lambda bi: (bi, 0, 0))
    col = pl.BlockSpec(sink_col.shape, lambda bi: (0, 0))
    return pl.pallas_call(
        functools.partial(_attn_kernel, nch=s // CHUNK, cl=cl, fixed_shift=fixed_shift),
        grid=(b,),
        in_specs=[seq(ATT_W), seq(KV_W), seq(KV_W), col, col],
        out_specs=seq(ATT_W),
        out_shape=jax.ShapeDtypeStruct((b, s, ATT_W), BF16),
        compiler_params=_cparams(("parallel",)),
        name="attention_fixed_shift" if fixed_shift else "attention_row_max",
    )(q, k, v, sink_col, shift_col)


SSD_HALO = 16
SSD_CAT = 256
SSD_UNROLL = 2
AC_PIECES = 2


def _ssd_kernel(xbc_ref, dt_ref, cw_ref, cb_ref, dtb_ref, alog_ref, dsk_ref, shf_ref, sel_ref, pair_ref, y_ref,
                xc_ref, act_ref, acp_ref, e3_ref, yb_ref, st_ref, *, nch, cl):
    ncb = cl // CHUNK
    s = nch * CHUNK
    halo = SSD_HALO

    def conv_body(c, carry):
        r0 = pl.multiple_of(c * CHUNK, CHUNK)
        has_prev = jnp.logical_and(c != 0, c != ncb)
        has_next = jnp.logical_and(c != ncb - 1, c != nch - 1)
        rp = pl.multiple_of(jnp.maximum(r0 - halo, 0), halo)
        rn = pl.multiple_of(jnp.minimum(r0 + CHUNK, s - halo), halo)
        zero_halo = jnp.zeros((halo, XBC_W), BF16)
        prev = jnp.where(has_prev, xbc_ref[0, pl.ds(rp, halo), :], zero_halo)
        nxt = jnp.where(has_next, xbc_ref[0, pl.ds(rn, halo), :], zero_halo)
        cur = xbc_ref[0, pl.ds(r0, CHUNK), :]
        cat = jnp.concatenate([prev, cur, nxt, jnp.zeros((SSD_CAT - CHUNK - 2 * halo, XBC_W), BF16)], axis=0)
        sh = jnp.dot(shf_ref[...], cat, preferred_element_type=F32)
        acc = cb_ref[...] + cw_ref[CONV_K // 2:CONV_K // 2 + 1, :] * cur.astype(F32)
        for i, kk in enumerate([k for k in range(CONV_K) if k != CONV_K // 2]):
            acc = acc + cw_ref[kk:kk + 1, :] * sh[i * CHUNK:(i + 1) * CHUNK]
        xc_ref[pl.ds(r0, CHUNK), :] = _silu(acc).astype(BF16)
        dtv = jax.nn.softplus(dt_ref[0, pl.ds(r0, CHUNK), :] + dtb)
        dta = dtv * a_neg
        pre = jnp.dot(tri_f, dta, precision=HIGHEST, preferred_element_type=F32)
        fwd_col = tj < SSM_HEADS
        acum = jnp.where(fwd_col, pre, pre[CHUNK - 1:CHUNK, :] - pre + dta)
        a_end = jnp.where(fwd_col[0:1], acum[CHUNK - 1:CHUNK, :], acum[0:1, :])
        hi = acum.astype(BF16)
        lo = (acum - hi.astype(F32)).astype(BF16)
        acp_ref[pl.ds(r0, CHUNK), :] = jnp.concatenate([hi, lo], axis=1)
        act_ref[pl.ds(r0, CHUNK), :] = (hi.astype(F32) + lo.astype(F32)).T
        e3_ref[pl.ds(r0, CHUNK), :] = jnp.concatenate(
            [jnp.exp(acum), jnp.exp(a_end - acum) * dtv, dtv], axis=1).astype(BF16)
        return carry

    ti = lax.broadcasted_iota(I32, (CHUNK, CHUNK), 0)
    tj = lax.broadcasted_iota(I32, (CHUNK, CHUNK), 1)
    low_half = tj < LANES // 2
    tri_f = (ti >= tj).astype(F32)
    a_neg = -jnp.exp(alog_ref[...])
    dtb = dtb_ref[...]

    def conv_pair(it, carry):
        for u in range(SSD_UNROLL):
            conv_body(it * SSD_UNROLL + u, carry)
        return carry

    lax.fori_loop(0, nch // SSD_UNROLL, conv_pair, 0)

    def chunk_update(c, direction, out_ref, out_lead):
        tri = (ti >= tj) if direction == 0 else (tj >= ti)
        last = CHUNK - 1 if direction == 0 else 0
        r0 = pl.multiple_of(c * CHUNK, CHUNK)
        xcv = xc_ref[pl.ds(r0, CHUNK), :]
        acum_t = act_ref[pl.ds(r0, CHUNK), :]
        acb = jnp.dot(acp_ref[pl.ds(r0, CHUNK), :], sel_ref[direction], preferred_element_type=F32)
        e3 = e3_ref[pl.ds(r0, CHUNK), :]
        prs = jnp.dot(jnp.concatenate([e3[:, i * LANES:(i + 1) * LANES] for i in range(3)], axis=0),
                      pair_ref[direction], preferred_element_type=F32)
        outs = []
        for g in range(2):
            xg = xcv[:, g * LANES:(g + 1) * LANES].astype(F32)
            bg = xcv[:, SSM_W + g * LANES:SSM_W + (g + 1) * LANES]
            cg = xcv[:, SSM_W + BC_W + g * LANES:SSM_W + BC_W + (g + 1) * LANES]
            gram = lax.dot_general(cg, bg, (((1,), (1,)), ((), ())), preferred_element_type=F32)
            eac_p = prs[0:CHUNK, g * LANES:(g + 1) * LANES]
            wend_p = prs[CHUNK:2 * CHUNK, g * LANES:(g + 1) * LANES]
            dt_p = prs[2 * CHUNK:3 * CHUNK, g * LANES:(g + 1) * LANES]
            mixes, dec = [], []
            for hh in range(2):
                h = 2 * g + hh
                col = direction * SSM_HEADS + h
                seg = jnp.exp(jnp.where(tri, acb[:, h * LANES:(h + 1) * LANES] - acum_t[col:col + 1, :], -jnp.inf))
                mixes.append((gram * seg).astype(BF16))
                dec.append(jnp.exp(acum_t[col:col + 1, last:last + 1]))
            xdt = (xg * dt_p).astype(BF16)
            y_intra = jnp.where(low_half, jnp.dot(mixes[0], xdt, preferred_element_type=F32),
                                jnp.dot(mixes[1], xdt, preferred_element_type=F32))
            stg = st_ref[direction, :, g * LANES:(g + 1) * LANES]
            y_inter = jnp.dot(cg, stg.astype(BF16), preferred_element_type=F32) * eac_p
            xw = (xg * wend_p).astype(BF16)
            contrib = lax.dot_general(bg, xw, (((0,), (0,)), ((), ())), preferred_element_type=F32)
            st_ref[direction, :, g * LANES:(g + 1) * LANES] = (
                stg * jnp.where(low_half[0:1], dec[0], dec[1]) + contrib)
            yg = y_intra + y_inter
            if direction == 0:
                yg = yg + dsk_ref[:, g * LANES:(g + 1) * LANES] * xg
            outs.append(yg)
        out_ref[out_lead + (pl.ds(r0, CHUNK), slice(None))] = jnp.concatenate(outs, axis=1)

    st_ref[...] = jnp.zeros_like(st_ref)

    def scan_body(it, carry):
        for u in range(SSD_UNROLL):
            step = it * SSD_UNROLL + u
            chunk_update(step, 0, y_ref, (0,))
            cb = jnp.where(step < ncb, ncb - 1 - step, nch - 1 - (step - ncb))
            chunk_update(cb, 1, yb_ref, ())
        return carry

    lax.fori_loop(0, nch // SSD_UNROLL, scan_body, 0)
    y_ref[0] = y_ref[0] + yb_ref[...]


def _ssd_constants():
    taps = [k for k in range(CONV_K) if k != CONV_K // 2]
    t = np.arange(CHUNK)[:, None]
    r = np.arange(SSD_CAT)[None, :]
    shf = np.concatenate([r == t + SSD_HALO - CONV_K // 2 + k for k in taps], axis=0)
    c = np.arange(LANES)[:, None]
    lane = np.arange(LANES)[None, :]
    sel = np.zeros((2, AC_PIECES * LANES, SSM_HEADS * LANES), bool)
    pair = np.zeros((2, LANES, 2 * LANES), bool)
    for d in range(2):
        for h in range(SSM_HEADS):
            for piece in range(AC_PIECES):
                sel[d, piece * LANES:(piece + 1) * LANES, h * LANES:(h + 1) * LANES] = c == d * SSM_HEADS + h
        for g in range(2):
            pair[d, :, g * LANES:(g + 1) * LANES] = c == d * SSM_HEADS + 2 * g + (lane >= LANES // 2)
    return jnp.asarray(shf, BF16), jnp.asarray(sel, BF16), jnp.asarray(pair, BF16)


def _ssd(xbc, dt, cw, cb, dtb, alog, dsk, consts, *, cl):
    b, s, _ = xbc.shape
    full = lambda a: pl.BlockSpec(a.shape, lambda bi: (0,) * a.ndim)
    return pl.pallas_call(
        functools.partial(_ssd_kernel, nch=s // CHUNK, cl=cl),
        grid=(b,),
        in_specs=[pl.BlockSpec((1, s, XBC_W), lambda bi: (bi, 0, 0)), pl.BlockSpec((1, s, DT_PAD), lambda bi: (bi, 0, 0)),
                  full(cw), full(cb), full(dtb), full(alog), full(dsk)] + [full(a) for a in consts],
        out_specs=pl.BlockSpec((1, s, SSM_W), lambda bi: (bi, 0, 0)),
        out_shape=jax.ShapeDtypeStruct((b, s, SSM_W), F32),
        scratch_shapes=[pltpu.VMEM((s, XBC_W), BF16), pltpu.VMEM((s, DT_PAD), F32),
                        pltpu.VMEM((s, AC_PIECES * LANES), BF16),
                        pltpu.VMEM((s, 3 * LANES), BF16), pltpu.VMEM((s, SSM_W), F32),
                        pltpu.VMEM((2, SSM_STATE, SSM_W), F32)],
        compiler_params=_cparams(("parallel",)),
        name="ssd",
    )(xbc, dt, cw, cb, dtb, alog, dsk, *consts)


TOK_ROWS = D_MODEL // LANES


def _load_tokens(ref, lead, n):
    return jnp.concatenate([ref[lead + (pl.ds(j, n, stride=TOK_ROWS), slice(None))] for j in range(TOK_ROWS)], axis=1)


def _store_tokens(ref, lead, val):
    for j in range(TOK_ROWS):
        ref[lead + (pl.ds(j, val.shape[0], stride=TOK_ROWS), slice(None))] = val[:, j * LANES:(j + 1) * LANES]


def _outproj_kernel(x_ref, mod_ref, gm_ref, att_ref, y_ref, z_ref, ng_ref, g2_ref, w_ref, xo_ref, hf_ref, *, tr, cl,
                    token_tiles):
    gated = y_ref[0] * _silu(z_ref[0].astype(F32))
    ssm = gated * lax.rsqrt(jnp.mean(gated * gated, axis=-1, keepdims=True) + EPS) * ng_ref[...]
    mix = jnp.concatenate([gm_ref[0], att_ref[0], ssm.astype(BF16)], axis=1)
    gate1, shift2, scale2 = _mod_rows(mod_ref, pl.program_id(1) * tr, tr, cl, (2, 3, 4))
    x = x_ref[0] + gate1 * jnp.dot(mix, w_ref[...], preferred_element_type=F32)
    xo_ref[0] = x
    ms = jnp.mean(x * x, axis=-1, keepdims=True)
    hf = x * lax.rsqrt(ms + EPS) * g2_ref[...] * (1.0 + scale2) + shift2
    if token_tiles:
        _store_tokens(hf_ref, (0,), hf)
    else:
        hf_ref[0] = hf.astype(hf_ref.dtype)


def _out_projection(xs, mod, gm, att, y, z, ng, g2, w_out, layer, *, cl, nt, token_tiles):
    b, s, _ = xs.shape
    tr = s // nt
    tok = lambda w: pl.BlockSpec((1, tr, w), lambda bi, j: (bi, j, 0))
    full = lambda a: pl.BlockSpec(a.shape, lambda bi, j: (0,) * a.ndim)
    if token_tiles:
        hf_spec = pl.BlockSpec((1, tr * TOK_ROWS, LANES), lambda bi, j: (bi, j, 0))
        hf_shape = jax.ShapeDtypeStruct((b, s * TOK_ROWS, LANES), F32)
    else:
        hf_spec, hf_shape = tok(D_MODEL), jax.ShapeDtypeStruct((b, s, D_MODEL), BF16)
    return pl.pallas_call(
        functools.partial(_outproj_kernel, tr=tr, cl=cl, token_tiles=token_tiles),
        grid=(b, nt),
        in_specs=[tok(D_MODEL), pl.BlockSpec((1, 16, D_MODEL), lambda bi, j: (bi, 0, 0)), tok(GM_W), tok(ATT_W),
                  tok(SSM_W), tok(SSM_W), full(ng), full(g2), _layer_block(w_out, layer)],
        out_specs=[tok(D_MODEL), hf_spec],
        out_shape=[jax.ShapeDtypeStruct((b, s, D_MODEL), F32), hf_shape],
        compiler_params=_cparams(("parallel", "parallel")),
        name="out_projection",
    )(xs, mod, gm, att, y, z, ng, g2, w_out)


def _swiglu_rows(h, wg_ref, wu_ref, wd_ref, act_ref):
    for f in range(D_FF // FF_BLOCK):
        cols = slice(f * FF_BLOCK, (f + 1) * FF_BLOCK)
        g = jnp.dot(h, wg_ref[0, :, cols], preferred_element_type=F32)
        u = jnp.dot(h, wu_ref[0, :, cols], preferred_element_type=F32)
        act_ref[:, cols] = (_silu(g) * u).astype(BF16)
    return jnp.dot(act_ref[...], wd_ref[0], preferred_element_type=F32)


def _ffn_kernel(x_ref, hf_ref, mod_ref, wg_ref, wu_ref, wd_ref, xo_ref, act_ref, *, tr, cl):
    (gate2,) = _mod_rows(mod_ref, pl.program_id(1) * tr, tr, cl, (5,))
    xo_ref[0] = x_ref[0] + gate2 * _swiglu_rows(hf_ref[0], wg_ref, wu_ref, wd_ref, act_ref)


def _dense_ffn(xs, hf, mod, wg, wu, wd, layer, *, cl, nt):
    b, s, _ = xs.shape
    tr = s // nt
    tok = pl.BlockSpec((1, tr, D_MODEL), lambda bi, j: (bi, j, 0))
    wspec = lambda w: pl.BlockSpec((1,) + w.shape[1:], lambda bi, j: (layer, 0, 0))
    return pl.pallas_call(
        functools.partial(_ffn_kernel, tr=tr, cl=cl),
        grid=(b, nt),
        in_specs=[tok, tok, pl.BlockSpec((1, 16, D_MODEL), lambda bi, j: (bi, 0, 0)), wspec(wg), wspec(wu), wspec(wd)],
        out_specs=tok,
        out_shape=jax.ShapeDtypeStruct((b, s, D_MODEL), F32),
        scratch_shapes=[pltpu.VMEM((tr, D_FF), BF16)],
        compiler_params=_cparams(("parallel", "parallel")),
        name="dense_ffn",
    )(xs, hf, mod, wg, wu, wd)


def _router_kernel(hf_ref, wr_ref, tri_ref, mi_ref, mf_ref, cnt_ref, run_ref, *, trc):
    @pl.when(pl.program_id(0) == 0)
    def _():
        run_ref[...] = jnp.zeros_like(run_ref)

    x = _load_tokens(hf_ref, (), trc)
    w = wr_ref[...]
    xh = x.astype(BF16)
    xl = (x - xh.astype(F32)).astype(BF16)
    wh = w.astype(BF16)
    wl = (w - wh.astype(F32)).astype(BF16)
    nt_dims = (((1,), (1,)), ((), ()))
    both = lax.dot_general(jnp.concatenate([wh, wl], axis=0), xh, nt_dims, preferred_element_type=F32)
    logits = (both[:N_EXPERTS] + both[N_EXPERTS:]
              + lax.dot_general(wh, xl, nt_dims, preferred_element_type=F32))
    eidx = lax.broadcasted_iota(I32, (N_EXPERTS, trc), 0).astype(F32)
    m1 = jnp.max(logits, axis=0, keepdims=True)
    i1 = jnp.min(jnp.where(logits == m1, eidx, float(N_EXPERTS)), axis=0, keepdims=True)
    rest = jnp.where(eidx == i1, -jnp.inf, logits)
    m2 = jnp.max(rest, axis=0, keepdims=True)
    i2 = jnp.min(jnp.where(rest == m2, eidx, float(N_EXPERTS)), axis=0, keepdims=True)
    e2 = jnp.exp(m2 - m1)
    g1 = 1.0 / (1.0 + e2)
    g2 = e2 / (1.0 + e2)
    oh1 = (eidx == i1).astype(F32)
    oh2 = (eidx == i2).astype(F32)
    sel = oh1 + oh2
    before = jnp.dot(sel.astype(BF16), tri_ref[...], preferred_element_type=F32) + run_ref[:, 0:1]
    r1 = jnp.sum(oh1 * before, axis=0, keepdims=True)
    r2 = jnp.sum(oh2 * before, axis=0, keepdims=True)
    run_ref[...] = run_ref[...] + jnp.sum(sel, axis=1, keepdims=True)
    zi = jnp.zeros((4, trc), I32)
    mi_ref[0] = jnp.concatenate([i1.astype(I32), i2.astype(I32), r1.astype(I32), r2.astype(I32), zi], axis=0)
    mf_ref[0] = jnp.concatenate([g1, g2, jnp.zeros((6, trc), F32)], axis=0)
    cnt_ref[...] = run_ref[...].astype(I32)


def _router(hf_tiles, wr_t, tri, *, trc):
    t = hf_tiles.shape[0] // TOK_ROWS
    nt = t // trc
    meta = pl.BlockSpec((1, 8, trc), lambda i: (i, 0, 0))
    return pl.pallas_call(
        functools.partial(_router_kernel, trc=trc),
        grid=(nt,),
        in_specs=[pl.BlockSpec((trc * TOK_ROWS, LANES), lambda i: (i, 0)), pl.BlockSpec(wr_t.shape, lambda i: (0, 0)),
                  pl.BlockSpec(tri.shape, lambda i: (0, 0))],
        out_specs=[meta, meta, pl.BlockSpec((N_EXPERTS, LANES), lambda i: (0, 0))],
        out_shape=[jax.ShapeDtypeStruct((nt, 8, trc), I32), jax.ShapeDtypeStruct((nt, 8, trc), F32),
                   jax.ShapeDtypeStruct((N_EXPERTS, LANES), I32)],
        scratch_shapes=[pltpu.VMEM((N_EXPERTS, LANES), F32)],
        compiler_params=_cparams(("arbitrary",)),
        name="moe_router",
    )(hf_tiles, wr_t, tri)


DMA_UNROLL = 8


def _token_copy(src_ref, src_tok, dst_ref, dst_tok, sem):
    src = src_ref.at[pl.ds(pl.multiple_of(src_tok * TOK_ROWS, TOK_ROWS), TOK_ROWS), :]
    dst = dst_ref.at[pl.ds(pl.multiple_of(dst_tok * TOK_ROWS, TOK_ROWS), TOK_ROWS), :]
    return pltpu.make_async_copy(src, dst, sem)


def _for_tokens(n, fn):
    def block(i, carry):
        for u in range(DMA_UNROLL):
            fn(i * DMA_UNROLL + u)
        return carry

    lax.fori_loop(0, n // DMA_UNROLL, block, 0)


def _dispatch_kernel(pad_lo_ref, pad_n_ref, nact_ref, slot_ref, hf_ref, xs_ref, zero_ref, sem, zsem, *, trc, ntile):
    @pl.when(pl.program_id(0) == 0)
    def _():
        zero_ref[...] = jnp.zeros_like(zero_ref)
        tile_rows = EXPERT_TILE * TOK_ROWS

        def tile_copy(i):
            return pltpu.make_async_copy(
                zero_ref, xs_ref.at[pl.ds(pl.multiple_of(i * tile_rows, tile_rows), tile_rows), :], zsem)

        def fill_tile(i, carry):
            tile_copy(i).start()
            return carry

        def drain_tile(i, carry):
            tile_copy(i).wait()
            return carry

        lax.fori_loop(nact_ref[0], ntile, fill_tile, 0)
        lax.fori_loop(nact_ref[0], ntile, drain_tile, 0)
        for e in range(N_EXPERTS):
            lo = pad_lo_ref[e]
            n = pad_n_ref[e]

            def fill(r, carry):
                _token_copy(zero_ref, 0, xs_ref, lo + r, zsem).start()
                return carry

            def drain(r, carry):
                _token_copy(zero_ref, 0, xs_ref, lo + r, zsem).wait()
                return carry

            lax.fori_loop(0, n, fill, 0)
            lax.fori_loop(0, n, drain, 0)

    copies = lambda r: [_token_copy(hf_ref, r, xs_ref, slot_ref[0, k, r], sem) for k in range(2)]
    _for_tokens(trc, lambda r: [cp.start() for cp in copies(r)])
    _for_tokens(trc, lambda r: [cp.wait() for cp in copies(r)])


def _dispatch(pad_lo, pad_n, nact, slots, hf_tiles, nslot, *, trc):
    t = hf_tiles.shape[0] // TOK_ROWS
    nt = t // trc
    grid_spec = pltpu.PrefetchScalarGridSpec(
        num_scalar_prefetch=3,
        grid=(nt,),
        in_specs=[pl.BlockSpec((1, 2, trc), lambda i, lo, n, na: (i, 0, 0), memory_space=pltpu.SMEM),
                  pl.BlockSpec((trc * TOK_ROWS, LANES), lambda i, lo, n, na: (i, 0))],
        out_specs=pl.BlockSpec(memory_space=pl.ANY),
        scratch_shapes=[pltpu.VMEM((EXPERT_TILE * TOK_ROWS, LANES), F32), pltpu.SemaphoreType.DMA(()),
                        pltpu.SemaphoreType.DMA(())],
    )
    return pl.pallas_call(
        functools.partial(_dispatch_kernel, trc=trc, ntile=nslot // EXPERT_TILE),
        grid_spec=grid_spec,
        out_shape=jax.ShapeDtypeStruct((nslot * TOK_ROWS, LANES), F32),
        compiler_params=_cparams(("arbitrary",)),
        name="moe_dispatch",
    )(pad_lo, pad_n, nact, slots, hf_tiles)


def _expert_kernel(texp_ref, nact_ref, xs_ref, wg_ref, wu_ref, wd_ref, ys_ref, act_ref):
    active = pl.program_id(0) < nact_ref[0]

    @pl.when(active)
    def _():
        h = _load_tokens(xs_ref, (), EXPERT_TILE).astype(BF16)
        _store_tokens(ys_ref, (), _swiglu_rows(h, wg_ref, wu_ref, wd_ref, act_ref))

    @pl.when(jnp.logical_not(active))
    def _():
        ys_ref[...] = jnp.zeros_like(ys_ref)


def _expert_ffn(tile_expert, nact, xs, wg, wu, wd, layer):
    ntile = xs.shape[0] // (EXPERT_TILE * TOK_ROWS)
    wspec = lambda shp: pl.BlockSpec((None, 1) + shp, lambda i, te, na: (layer, te[i], 0, 0))
    tile = pl.BlockSpec((EXPERT_TILE * TOK_ROWS, LANES), lambda i, te, na: (i, 0))
    grid_spec = pltpu.PrefetchScalarGridSpec(
        num_scalar_prefetch=2,
        grid=(ntile,),
        in_specs=[tile, wspec((D_MODEL, D_FF)), wspec((D_MODEL, D_FF)), wspec((D_FF, D_MODEL))],
        out_specs=tile,
        scratch_shapes=[pltpu.VMEM((EXPERT_TILE, D_FF), BF16)],
    )
    return pl.pallas_call(
        _expert_kernel,
        grid_spec=grid_spec,
        out_shape=jax.ShapeDtypeStruct(xs.shape, F32),
        compiler_params=_cparams(("arbitrary",)),
        name="moe_experts",
    )(tile_expert, nact, xs, wg, wu, wd)


def _combine_kernel(slot_ref, x_ref, gate_ref, mod_ref, ys_ref, xo_ref, buf_ref, sem, *, trc, nt, cl):
    copies = lambda r: [_token_copy(ys_ref, slot_ref[0, k, r], buf_ref.at[k], r, sem) for k in range(2)]
    _for_tokens(trc, lambda r: [cp.start() for cp in copies(r)])
    _for_tokens(trc, lambda r: [cp.wait() for cp in copies(r)])
    (gate2,) = _mod_rows(mod_ref, (pl.program_id(0) % nt) * trc, trc, cl, (5,))
    gt = gate_ref[0]
    f = gt[:, 0:1] * _load_tokens(buf_ref, (0,), trc) + gt[:, 1:2] * _load_tokens(buf_ref, (1,), trc)
    xo_ref[...] = x_ref[...] + gate2 * f


def _combine(slots, x_flat, gates_t, mod, ys, *, trc, nt, cl):
    t = x_flat.shape[0]
    ntile = t // trc
    tok = pl.BlockSpec((trc, D_MODEL), lambda i: (i, 0))
    return pl.pallas_call(
        functools.partial(_combine_kernel, trc=trc, nt=nt, cl=cl),
        grid=(ntile,),
        in_specs=[pl.BlockSpec((1, 2, trc), lambda i: (i, 0, 0), memory_space=pltpu.SMEM), tok,
                  pl.BlockSpec((1, trc, 2), lambda i: (i, 0, 0)),
                  pl.BlockSpec((1, 16, D_MODEL), lambda i: (i // nt, 0, 0)), pl.BlockSpec(memory_space=pl.ANY)],
        out_specs=tok,
        out_shape=jax.ShapeDtypeStruct((t, D_MODEL), F32),
        scratch_shapes=[pltpu.VMEM((2, trc * TOK_ROWS, LANES), F32), pltpu.SemaphoreType.DMA(())],
        compiler_params=_cparams(("arbitrary",)),
        name="moe_combine",
    )(slots, x_flat, gates_t, mod, ys)


def _moe_ffn(xs, hf, mod, wr_t, tri, wg, wu, wd, layer, *, cl, nt):
    b, s, _ = xs.shape
    t = b * s
    trr = tri.shape[0]
    trc = s // nt
    hf_flat = hf.reshape(t * TOK_ROWS, LANES)
    meta_i, meta_f, counts = _router(hf_flat, wr_t, tri, trc=trr)
    counts = counts[:, 0]
    padded = (counts + EXPERT_TILE - 1) // EXPERT_TILE * EXPERT_TILE
    ends = jnp.cumsum(padded)
    starts = ends - padded
    eid = meta_i[:, 0:2, :]
    group_start = sum(jnp.where(eid == e, starts[e], 0) for e in range(N_EXPERTS))
    slots = group_start + meta_i[:, 2:4, :]
    nslot = 2 * t + N_EXPERTS * EXPERT_TILE
    ntile = nslot // EXPERT_TILE
    nact = (ends[-1] // EXPERT_TILE).astype(I32)
    tile_lo = jnp.minimum(jnp.arange(ntile, dtype=I32), nact - 1) * EXPERT_TILE
    tile_expert = jnp.minimum(jnp.sum(tile_lo[:, None] >= ends[None, :], axis=1), N_EXPERTS - 1).astype(I32)
    nact = nact.reshape(1)
    xs_sorted = _dispatch((starts + counts).astype(I32), (padded - counts).astype(I32), nact, slots, hf_flat, nslot,
                          trc=trr)
    ys = _expert_ffn(tile_expert, nact, xs_sorted, wg, wu, wd, layer)
    retile = lambda a: jnp.swapaxes(jnp.swapaxes(a, 0, 1).reshape(2, t // trc, trc), 0, 1)
    gates_t = jnp.swapaxes(retile(meta_f[:, 0:2, :]), 1, 2)
    out = _combine(retile(slots), xs.reshape(t, D_MODEL), gates_t, mod, ys, trc=trc, nt=nt, cl=cl)
    return out.reshape(b, s, D_MODEL)


def _slab_perm(t, lead, xp=jnp):
    nh = t.shape[-1] // HEAD_DIM
    per = nh // 2
    t = t.reshape(lead + (2, per, 2, 2, 16))
    t = xp.moveaxis(t, (-5, -4, -3, -2, -1), (-3, -5, -2, -4, -1))
    return t.reshape(lead + (nh * HEAD_DIM,))


def _rope_tables(s, cl):
    l = s - cl
    pos = np.arange(l)
    lane = np.arange(LANES)
    i = lane % 32
    freq = ROPE_BASE ** (-(i % 16).astype(np.float32) / 16.0)
    p = np.where(i[None, :] < 16, (pos // GRID_W)[:, None], (pos % GRID_W)[:, None]).astype(np.float32)
    ang = p * freq[None, :].astype(np.float32)
    sign = np.where(lane < LANES // 2, -1.0, 1.0).astype(np.float32)
    cos = np.concatenate([np.ones((cl, LANES), np.float32), np.cos(ang)], axis=0)
    sin = np.concatenate([np.zeros((cl, LANES), np.float32), np.sin(ang) * sign[None, :]], axis=0)
    return jnp.asarray(cos, F32), jnp.asarray(sin, F32)


def kernel(x, c, ctx, c_ctx, w_mod, b_mod, norm1_g, norm2_g, w_in, w_out, gm_v_g, gm_ws, gm_bs, att_q_g, att_k_g, att_sink, ssm_conv_w, ssm_conv_b, ssm_dt_bias, ssm_a_log, ssm_d, ssm_norm_g, ffn_w_gate, ffn_w_up, ffn_w_down, moe_router, moe_w_gate, moe_w_up, moe_w_down):
    b, l, _ = x.shape
    cl = ctx.shape[1]
    s = cl + l
    depth = w_mod.shape[0]
    nt = 8
    nt_moe = 4
    assert s % (nt * 16) == 0 and cl % CHUNK == 0 and l % CHUNK == 0 and l % GRID_W == 0 and b < 16
    assert all((s // CHUNK) % u == 0 for u in (SSD_UNROLL, GMLP_UNROLL, ATT_UNROLL))

    cvec = jnp.concatenate([c, c_ctx[None, :], jnp.zeros((16 - b - 1, D_MODEL), F32)], axis=0)
    mods = _modulation(cvec, w_mod, b_mod)
    lat = jnp.moveaxis(mods[:, :, :b, :], 2, 1)
    con = jnp.broadcast_to(mods[:, None, :, b, :], (depth, b, 6, D_MODEL))
    pad2 = jnp.zeros((depth, b, 2, D_MODEL), F32)
    modtab = jnp.concatenate([lat, pad2, con, pad2], axis=2)

    wi = w_in.astype(BF16)
    src = np.concatenate([_slab_perm(np.arange(ATT_W), (), np), ATT_W + _slab_perm(np.arange(KV_W), (), np)])
    perm = jnp.asarray(np.arange(ATT_W + KV_W)[:, None] == src[None, :], BF16)
    w_in_p = jnp.concatenate([
        wi[:, :, 0:2 * GM_W],
        jnp.matmul(wi[:, :, 512:1152], perm),
        wi[:, :, 1152:2304],
        jnp.pad(wi[:, :, 2304:2312], ((0, 0), (0, 0), (0, DT_PAD - 2 * SSM_HEADS))),
    ], axis=2)
    wo_att = w_out[:, GM_W:GM_W + ATT_W, :].reshape(depth, 2, 4, HEAD_DIM, D_MODEL)
    wo_att = jnp.swapaxes(wo_att, 1, 2).reshape(depth, ATT_W, D_MODEL)
    w_out_p = jnp.concatenate([w_out[:, :GM_W], wo_att, w_out[:, GM_W + ATT_W:]], axis=1).astype(BF16)
    gq = _slab_perm(jnp.tile(att_q_g, (1, 2)), (depth,))[:, None, :]
    gk = _slab_perm(jnp.tile(att_k_g, (1, 2)), (depth,))[:, None, :]
    lane = np.arange(LANES)
    seg = jnp.asarray(((lane[:, None] // 32) % 2 == (lane[None, :] // 32) % 2), BF16)
    lane2 = np.arange(GM_W)
    seg64 = jnp.asarray(lane2[:, None] // GM_HD == lane2[None, :] // GM_HD, BF16)
    cos, sin = _rope_tables(s, cl)
    ws_cat = jnp.swapaxes(gm_ws, 1, 2).reshape(depth, CHUNK, GM_HEADS * CHUNK).astype(BF16)
    bs_exp = jnp.repeat(jnp.swapaxes(gm_bs, 1, 2), GM_HD, axis=2)
    sink_heads = att_sink.reshape(depth, 2, 4).swapaxes(1, 2).reshape(depth, ATT_HEADS)
    sink_col = jnp.repeat(sink_heads, CHUNK, axis=1)[:, :, None] * LOG2E
    score_bound = (8.0 * (1.0 + 2.0 ** -6)) * jnp.max(jnp.abs(att_q_g), axis=1) * jnp.max(jnp.abs(att_k_g), axis=1)
    shift_col = jnp.maximum(sink_col, score_bound[:, None, None] * LOG2E)
    cw = jnp.pad(ssm_conv_w, ((0, 0), (0, 8 - CONV_K), (0, 0)))
    cb = ssm_conv_b[:, None, :]
    pad_dt = lambda t: jnp.pad(t.reshape(depth, 1, 2 * SSM_HEADS), ((0, 0), (0, 0), (0, DT_PAD - 2 * SSM_HEADS)))
    dtb = pad_dt(ssm_dt_bias)
    alog = pad_dt(ssm_a_log)
    dsk = jnp.repeat(ssm_d, SSM_HD, axis=1)[:, None, :]
    ssd_consts = _ssd_constants()
    wr_t = jnp.swapaxes(moe_router, 1, 2)
    trr = 1024 if (b * s) % 1024 == 0 else 512
    assert (b * s) % trr == 0
    tidx = np.arange(trr)
    tri = jnp.asarray(tidx[:, None] < tidx[None, :], BF16)
    ffn_g, ffn_u, ffn_d = ffn_w_gate.astype(BF16), ffn_w_up.astype(BF16), ffn_w_down.astype(BF16)
    moe_g, moe_u, moe_d = moe_w_gate.astype(BF16), moe_w_up.astype(BF16), moe_w_down.astype(BF16)

    xs = jnp.concatenate([ctx, x], axis=1)
    for i in range(depth):
        moe = i % 2 == 1
        mod = modtab[i]
        gu, gv, q, k, v, z, xbc, dt = _in_projection(
            xs, mod, norm1_g[i][None, :], w_in_p, i, cos, sin, gq[i], gk[i], gm_v_g[i][None, :], seg, seg64,
            cl=cl, nt=nt)
        gm = _gmlp(gu, gv, ws_cat[i], bs_exp[i])
        att = lax.cond(score_bound[i] <= SAFE_SCORE_BOUND,
                       functools.partial(_attention, cl=cl, fixed_shift=True),
                       functools.partial(_attention, cl=cl, fixed_shift=False),
                       q, k, v, sink_col[i], shift_col[i])
        y = _ssd(xbc, dt, cw[i], cb[i], dtb[i], alog[i], dsk[i], ssd_consts, cl=cl)
        xs, hf = _out_projection(xs, mod, gm, att, y, z, ssm_norm_g[i][None, :], norm2_g[i][None, :], w_out_p, i,
                                 cl=cl, nt=nt, token_tiles=moe)
        j = i // 2
        if moe:
            xs = _moe_ffn(xs, hf, mod, wr_t[j], tri, moe_g, moe_u, moe_d, j, cl=cl, nt=nt_moe)
        else:
            xs = _dense_ffn(xs, hf, mod, ffn_g, ffn_u, ffn_d, j, cl=cl, nt=nt)
    return xs[:, cl:, :]
```
